```python
import math
import jax, jax.numpy as jnp
from jax import lax
import numpy as np

D_MODEL = 1024
BATCH = 8
SEQ = 4096
DEPTH = 2

D_CONV = D_MODEL // 2
CONV_WIDTH = 3
D_SSM = D_MODEL // 2
SSM_GROUP_DIM = 16
SSM_GROUPS = D_SSM // SSM_GROUP_DIM
SSM_STATE = 64
SSM_DT_MIN = 1e-3
SSM_DT_MAX = 1e-1
MLA_HEADS = 8
QK_NOPE = 64
QK_ROPE = 32
V_HEAD = 64
Q_LORA = D_MODEL // 4
KV_LORA = D_MODEL // 8
D_ATTN = MLA_HEADS * V_HEAD
ROPE_THETA = 10000.0
Q_BLOCK = 128
N_BRANCHES = 3
IN_MIX = 3 * D_CONV + D_SSM + Q_LORA + KV_LORA + QK_ROPE
N_IN = IN_MIX + N_BRANCHES * D_MODEL
IN_SPLITS = (
    D_CONV,
    2 * D_CONV,
    3 * D_CONV,
    3 * D_CONV + D_SSM,
    3 * D_CONV + D_SSM + Q_LORA,
    3 * D_CONV + D_SSM + Q_LORA + KV_LORA,
    IN_MIX,
    IN_MIX + D_MODEL,
    IN_MIX + 2 * D_MODEL,
)
N_EXPERTS = 32
N_EXPERT_GROUPS = 8
EXPERTS_PER_GROUP = N_EXPERTS // N_EXPERT_GROUPS
TOPK_GROUPS = 1
TOP_K = 2
D_EXPERT = D_MODEL // 4
EXPERT_BLOCK = 128
DEEPNORM_ALPHA = (2 * DEPTH) ** 0.25
DEEPNORM_BETA = (8 * DEPTH) ** -0.25
LN_EPS = 1e-5
RMS_EPS = 1e-6

kernel_name = "hybrid_conv_s5_mla_grouped_moe_deepnorm_adaln"


def layer_norm_plain(x):
    xf = x.astype(jnp.float32)
    mu = jnp.mean(xf, axis=-1, keepdims=True)
    var = jnp.mean(jnp.square(xf - mu), axis=-1, keepdims=True)
    return ((xf - mu) * lax.rsqrt(var + LN_EPS)).astype(x.dtype)


def layer_norm(x, g, b):
    return (layer_norm_plain(x) * g + b).astype(x.dtype)


def rms_norm(x, g):
    xf = x.astype(jnp.float32)
    y = xf * lax.rsqrt(jnp.mean(jnp.square(xf), axis=-1, keepdims=True) + RMS_EPS)
    return (y * g).astype(x.dtype)


def modulate(x, shift, scale):
    return layer_norm_plain(x) * (1.0 + scale[:, None, :]) + shift[:, None, :]


def rope_tables(positions):
    inv_freq = ROPE_THETA ** (-jnp.arange(0, QK_ROPE, 2, dtype=jnp.float32) / QK_ROPE)
    ang = positions.astype(jnp.float32)[..., None] * inv_freq
    return jnp.cos(ang), jnp.sin(ang)


def apply_rope(x, cos, sin):
    x1, x2 = jnp.split(x, 2, axis=-1)
    cos = cos.astype(x.dtype)
    sin = sin.astype(x.dtype)
    return jnp.concatenate([x1 * cos - x2 * sin, x1 * sin + x2 * cos], axis=-1)


def short_conv_branch(xc, gc, gb, conv_w):
    u = gc * xc
    out = lax.conv_general_dilated(
        u, conv_w[:, None, :], window_strides=(1,),
        padding=[(CONV_WIDTH - 1, 0)],
        dimension_numbers=("NWC", "WIO", "NWC"),
        feature_group_count=D_CONV)
    return gb * out


def s5_branch(u, a_re, a_im, b_re, b_im, c_re, c_im, d_skip, log_dt, w_glu):
    Bsz, S, _ = u.shape
    f32 = jnp.float32
    uf = u.astype(f32).reshape(Bsz, S, SSM_GROUPS, SSM_GROUP_DIM)
    ar = a_re.astype(f32)
    ai = a_im.astype(f32)
    dt = jnp.exp(log_dt.astype(f32))[:, None]
    mag = jnp.exp(ar * dt)
    lb_re = mag * jnp.cos(ai * dt)
    lb_im = mag * jnp.sin(ai * dt)
    den = ar * ar + ai * ai
    nr = lb_re - 1.0
    ni = lb_im
    f_re = (nr * ar + ni * ai) / den
    f_im = (ni * ar - nr * ai) / den
    br = b_re.astype(f32)
    bi = b_im.astype(f32)
    bb_re = f_re[..., None] * br - f_im[..., None] * bi
    bb_im = f_re[..., None] * bi + f_im[..., None] * br
    bu_re = jnp.einsum("bsgp,gnp->bsgn", uf, bb_re)
    bu_im = jnp.einsum("bsgp,gnp->bsgn", uf, bb_im)
    la_re = jnp.broadcast_to(lb_re[None, None], (1, S, SSM_GROUPS, SSM_STATE))
    la_im = jnp.broadcast_to(lb_im[None, None], (1, S, SSM_GROUPS, SSM_STATE))

    def combine(e_i, e_j):
        ar_i, ai_i, br_i, bi_i = e_i
        ar_j, ai_j, br_j, bi_j = e_j
        return (ar_j * ar_i - ai_j * ai_i,
                ar_j * ai_i + ai_j * ar_i,
                ar_j * br_i - ai_j * bi_i + br_j,
                ar_j * bi_i + ai_j * br_i + bi_j)

    _, _, s_re, s_im = lax.associative_scan(combine, (la_re, la_im, bu_re, bu_im), axis=1)
    y = (jnp.einsum("bsgn,gpn->bsgp", s_re, c_re.astype(f32))
         - jnp.einsum("bsgn,gpn->bsgp", s_im, c_im.astype(f32))
         + d_skip.astype(f32).reshape(SSM_GROUPS, SSM_GROUP_DIM) * uf)
    y = jax.nn.gelu(y.reshape(Bsz, S, D_SSM))
    val, gate = jnp.split(y @ w_glu.astype(f32), 2, axis=-1)
    return (val * jax.nn.sigmoid(gate)).astype(u.dtype)


def causal_block_attention(q, k, v):
    Bsz, S, H, Dk = q.shape
    nb = S // Q_BLOCK
    qb = q.reshape(Bsz, nb, Q_BLOCK, H, Dk).transpose(1, 0, 2, 3, 4)
    key_idx = jnp.arange(S, dtype=jnp.int32)
    scale = Dk ** -0.5

    def one_block(args):
        qi, bi = args
        s = jnp.einsum("bqhd,bkhd->bhqk", qi, k, preferred_element_type=jnp.float32) * scale
        q_idx = bi * Q_BLOCK + jnp.arange(Q_BLOCK, dtype=jnp.int32)
        mask = key_idx[None, :] <= q_idx[:, None]
        s = jnp.where(mask[None, None], s, -jnp.inf)
        p = jax.nn.softmax(s, axis=-1).astype(v.dtype)
        return jnp.einsum("bhqk,bkhd->bqhd", p, v)

    o = lax.map(one_block, (qb, jnp.arange(nb, dtype=jnp.int32)))
    return o.transpose(1, 0, 2, 3, 4).reshape(Bsz, S, H * v.shape[-1])


def mla_branch(c_q, c_kv, k_r, cos, sin, q_norm, w_uq, kv_norm, w_uk, w_uv):
    Bsz, S, _ = c_q.shape
    cq = rms_norm(c_q, q_norm)
    ckv = rms_norm(c_kv, kv_norm)
    q = (cq @ w_uq).reshape(Bsz, S, MLA_HEADS, QK_NOPE + QK_ROPE)
    q_rope = apply_rope(q[..., QK_NOPE:], cos[:, :, None, :], sin[:, :, None, :])
    q = jnp.concatenate([q[..., :QK_NOPE], q_rope], axis=-1)
    k_rope = apply_rope(k_r, cos, sin)
    k_nope = (ckv @ w_uk).reshape(Bsz, S, MLA_HEADS, QK_NOPE)
    v = (ckv @ w_uv).reshape(Bsz, S, MLA_HEADS, V_HEAD)
    k = jnp.concatenate(
        [k_nope, jnp.broadcast_to(k_rope[:, :, None, :], (Bsz, S, MLA_HEADS, QK_ROPE))], axis=-1)
    return causal_block_attention(q, k, v)


def hybrid_mixer(h, cos, sin, w_in, conv_w, a_re, a_im, b_re, b_im, c_re, c_im, d_skip,
                 log_dt, w_glu, q_norm, w_uq, kv_norm, w_uk, w_uv,
                 w_up_conv, w_up_ssm, w_up_attn, w_o):
    proj = h @ w_in
    (xc, gc, gb, u_ssm, c_q, c_kv, k_r,
     g_conv, g_ssm, g_attn) = jnp.split(proj, IN_SPLITS, axis=-1)
    y_conv = short_conv_branch(xc, gc, gb, conv_w) @ w_up_conv
    y_ssm = s5_branch(u_ssm, a_re, a_im, b_re, b_im, c_re, c_im, d_skip, log_dt, w_glu) @ w_up_ssm
    y_attn = mla_branch(c_q, c_kv, k_r, cos, sin, q_norm, w_uq, kv_norm, w_uk, w_uv) @ w_up_attn
    merged = (jax.nn.sigmoid(g_conv) * y_conv
              + jax.nn.sigmoid(g_ssm) * y_ssm
              + jax.nn.sigmoid(g_attn) * y_attn)
    return merged @ w_o


def swiglu(x, w1, w3, w2):
    return (jax.nn.silu(x @ w1) * (x @ w3)) @ w2


def grouped_top2_route(h2d, router_w, router_bias):
    T = h2d.shape[0]
    s = jax.nn.sigmoid(h2d.astype(jnp.float32) @ router_w.astype(jnp.float32))
    sel = s + router_bias.astype(jnp.float32)
    grp = sel.reshape(T, N_EXPERT_GROUPS, EXPERTS_PER_GROUP)
    grp_score = lax.top_k(grp, 2)[0].sum(axis=-1)
    _, g_idx = lax.top_k(grp_score, TOPK_GROUPS)
    g_mask = jnp.any(g_idx[:, :, None] == jnp.arange(N_EXPERT_GROUPS)[None, None, :], axis=1)
    e_mask = jnp.repeat(g_mask, EXPERTS_PER_GROUP, axis=1)
    _, e_idx = lax.top_k(jnp.where(e_mask, sel, -jnp.inf), TOP_K)
    w = jnp.take_along_axis(s, e_idx, axis=1)
    w = w / jnp.sum(w, axis=-1, keepdims=True)
    return e_idx, w


def routed_experts(h2d, e_idx, e_w, w1, w3, w2):
    T, D = h2d.shape
    M = T * TOP_K
    flat_e = e_idx.reshape(M)
    flat_tok = jnp.arange(M, dtype=jnp.int32) // TOP_K
    flat_w = e_w.reshape(M)
    order = jnp.argsort(flat_e)
    e_sorted = flat_e[order]
    tok_sorted = flat_tok[order]
    w_sorted = flat_w[order]
    counts = jnp.bincount(flat_e, length=N_EXPERTS)
    padded = (counts + EXPERT_BLOCK - 1) // EXPERT_BLOCK * EXPERT_BLOCK
    pad_end = jnp.cumsum(padded)
    pad_start = pad_end - padded
    start = jnp.cumsum(counts) - counts
    dest = pad_start[e_sorted] + (jnp.arange(M, dtype=jnp.int32) - start[e_sorted])
    n_blocks = (M + EXPERT_BLOCK - 1) // EXPERT_BLOCK + N_EXPERTS
    n_rows = n_blocks * EXPERT_BLOCK
    row_tok = jnp.zeros((n_rows,), jnp.int32).at[dest].set(tok_sorted)
    row_w = jnp.zeros((n_rows,), h2d.dtype).at[dest].set(w_sorted)
    block_starts = jnp.arange(n_blocks, dtype=pad_end.dtype) * EXPERT_BLOCK
    block_e = jnp.minimum(jnp.searchsorted(pad_end, block_starts, side="right"), N_EXPERTS - 1)
    x_rows = h2d[row_tok].reshape(n_blocks, EXPERT_BLOCK, D)

    def one_block(args):
        xb, e = args
        return swiglu(xb, w1[e], w3[e], w2[e])

    y_rows = lax.map(one_block, (x_rows, block_e)).reshape(n_rows, D)
    return jax.ops.segment_sum(y_rows * row_w[:, None], row_tok, num_segments=T)


def moe_ffn(h, router_w, router_bias, w1, w3, w2, ws1, ws3, ws2):
    Bsz, S, D = h.shape
    h2d = h.reshape(Bsz * S, D)
    e_idx, e_w = grouped_top2_route(h2d, router_w, router_bias)
    y = routed_experts(h2d, e_idx, e_w.astype(h.dtype), w1, w3, w2) + swiglu(h2d, ws1, ws3, ws2)
    return y.reshape(Bsz, S, D)


def setup_inputs(seed: int = 0) -> dict:
    key = jax.random.key(seed)
    ks = iter(jax.random.split(key, 64))
    L = DEPTH
    beta = DEEPNORM_BETA

    def nrm(shape, scale):
        return scale * jax.random.normal(next(ks), shape, jnp.float32)

    x = nrm((BATCH, SEQ, D_MODEL), 1.0)
    c = nrm((BATCH, D_MODEL), 1.0)
    offset = jax.random.randint(next(ks), (BATCH, 1), 0, 2048, dtype=jnp.int32)
    positions = offset + jnp.arange(SEQ, dtype=jnp.int32)[None, :]
    n_idx = jnp.arange(SSM_STATE, dtype=jnp.float32)
    return {
        "x": x,
        "c": c,
        "positions": positions,
        "w_ada": nrm((L, D_MODEL, 6 * D_MODEL), 0.5 * D_MODEL ** -0.5),
        "b_ada": nrm((L, 6 * D_MODEL), 0.01),
        "w_in": nrm((L, D_MODEL, N_IN), D_MODEL ** -0.5),
        "conv_w": nrm((L, CONV_WIDTH, D_CONV), CONV_WIDTH ** -0.5),
        "ssm_a_re": -0.5 + nrm((L, SSM_GROUPS, SSM_STATE), 0.01),
        "ssm_a_im": math.pi * n_idx + nrm((L, SSM_GROUPS, SSM_STATE), 0.01),
        "ssm_b_re": nrm((L, SSM_GROUPS, SSM_STATE, SSM_GROUP_DIM), (2 * SSM_GROUP_DIM) ** -0.5),
        "ssm_b_im": nrm((L, SSM_GROUPS, SSM_STATE, SSM_GROUP_DIM), (2 * SSM_GROUP_DIM) ** -0.5),
        "ssm_c_re": nrm((L, SSM_GROUPS, SSM_GROUP_DIM, SSM_STATE), (2 * SSM_STATE) ** -0.5),
        "ssm_c_im": nrm((L, SSM_GROUPS, SSM_GROUP_DIM, SSM_STATE), (2 * SSM_STATE) ** -0.5),
        "ssm_d": nrm((L, D_SSM), 1.0),
        "ssm_log_dt": jax.random.uniform(next(ks), (L, SSM_GROUPS), jnp.float32,
                                         math.log(SSM_DT_MIN), math.log(SSM_DT_MAX)),
        "ssm_w_glu": nrm((L, D_SSM, 2 * D_SSM), D_SSM ** -0.5),
        "q_norm": 1.0 + nrm((L, Q_LORA), 0.01),
        "w_uq": nrm((L, Q_LORA, MLA_HEADS * (QK_NOPE + QK_ROPE)), Q_LORA ** -0.5),
        "kv_norm": 1.0 + nrm((L, KV_LORA), 0.01),
        "w_uk": nrm((L, KV_LORA, MLA_HEADS * QK_NOPE), KV_LORA ** -0.5),
        "w_uv": nrm((L, KV_LORA, MLA_HEADS * V_HEAD), beta * KV_LORA ** -0.5),
        "w_up_conv": nrm((L, D_CONV, D_MODEL), beta * D_CONV ** -0.5),
        "w_up_ssm": nrm((L, D_SSM, D_MODEL), beta * D_SSM ** -0.5),
        "w_up_attn": nrm((L, D_ATTN, D_MODEL), beta * D_ATTN ** -0.5),
        "w_o": nrm((L, D_MODEL, D_MODEL), beta * D_MODEL ** -0.5),
        "ln_g": 1.0 + nrm((L, 2, D_MODEL), 0.01),
        "ln_b": nrm((L, 2, D_MODEL), 0.01),
        "router_w": nrm((D_MODEL, N_EXPERTS), D_MODEL ** -0.5),
        "router_bias": nrm((N_EXPERTS,), 0.01),
        "exp_w1": nrm((L, N_EXPERTS, D_MODEL, D_EXPERT), D_MODEL ** -0.5),
        "exp_w3": nrm((L, N_EXPERTS, D_MODEL, D_EXPERT), D_MODEL ** -0.5),
        "exp_w2": nrm((L, N_EXPERTS, D_EXPERT, D_MODEL), beta * D_EXPERT ** -0.5),
        "shared_w1": nrm((L, D_MODEL, D_EXPERT), D_MODEL ** -0.5),
        "shared_w3": nrm((L, D_MODEL, D_EXPERT), D_MODEL ** -0.5),
        "shared_w2": nrm((L, D_EXPERT, D_MODEL), beta * D_EXPERT ** -0.5),
    }


def reference(x, c, positions, w_ada, b_ada, w_in, conv_w, ssm_a_re, ssm_a_im, ssm_b_re,
              ssm_b_im, ssm_c_re, ssm_c_im, ssm_d, ssm_log_dt, ssm_w_glu, q_norm, w_uq,
              kv_norm, w_uk, w_uv, w_up_conv, w_up_ssm, w_up_attn, w_o, ln_g, ln_b,
              router_w, router_bias, exp_w1, exp_w3, exp_w2, shared_w1, shared_w3, shared_w2):
    cos, sin = rope_tables(positions)
    cond = jax.nn.silu(c)
    for l in range(DEPTH):
        ada = cond @ w_ada[l] + b_ada[l]
        sh1, sc1, g1, sh2, sc2, g2 = jnp.split(ada, 6, axis=-1)
        h = modulate(x, sh1, sc1)
        y = hybrid_mixer(h, cos, sin, w_in[l], conv_w[l], ssm_a_re[l], ssm_a_im[l],
                         ssm_b_re[l], ssm_b_im[l], ssm_c_re[l], ssm_c_im[l], ssm_d[l],
                         ssm_log_dt[l], ssm_w_glu[l], q_norm[l], w_uq[l], kv_norm[l],
                         w_uk[l], w_uv[l], w_up_conv[l], w_up_ssm[l], w_up_attn[l], w_o[l])
        x = layer_norm(DEEPNORM_ALPHA * x + g1[:, None, :] * y, ln_g[l, 0], ln_b[l, 0])
        h = modulate(x, sh2, sc2)
        y = moe_ffn(h, router_w, router_bias, exp_w1[l], exp_w3[l], exp_w2[l],
                    shared_w1[l], shared_w3[l], shared_w2[l])
        x = layer_norm(DEEPNORM_ALPHA * x + g2[:, None, :] * y, ln_g[l, 1], ln_b[l, 1])
    return x
```

```python
import functools
import math

import jax
import jax.numpy as jnp
from jax import lax
from jax.experimental import pallas as pl
from jax.experimental.pallas import tpu as pltpu

F32 = jnp.float32
BF16 = jnp.bfloat16

D_CONV = 512
D_SSM = 512
SSM_GROUP_DIM = 16
SSM_GROUPS = 32
SSM_STATE = 64
MLA_HEADS = 8
QK_NOPE = 64
QK_ROPE = 32
V_HEAD = 64
Q_LORA = 256
KV_LORA = 128
ROPE_THETA = 10000.0
N_EXPERTS = 32
N_EXPERT_GROUPS = 8
EXPERTS_PER_GROUP = 4
D_EXPERT = 256
LN_EPS = 1e-5
RMS_EPS = 1e-6

LANE = 128
HEAD_PAD = 128
VMEM_LIMIT = 56 * 1024 * 1024

TM_PROJ = 256
T_CHUNK = 64
TQ = 256
MOE_ROWS = 256
ROUTE_W = 128


def _cparams(n_axes):
    return pltpu.CompilerParams(dimension_semantics=("arbitrary",) * n_axes,
                                vmem_limit_bytes=VMEM_LIMIT)


def _ln_plain(x):
    mu = jnp.mean(x, axis=-1, keepdims=True)
    xc = x - mu
    var = jnp.mean(xc * xc, axis=-1, keepdims=True)
    return xc * lax.rsqrt(var + LN_EPS)


def _rms(x, g):
    return x * lax.rsqrt(jnp.mean(x * x, axis=-1, keepdims=True) + RMS_EPS) * g


def _bdot(a, b):
    return jnp.dot(a.astype(BF16), b, preferred_element_type=F32)


def _ada_kernel(c_ref, w_ref, b_ref, o_ref):
    c = c_ref[...]
    cond = c * jax.nn.sigmoid(c)
    o_ref[...] = jnp.dot(cond, w_ref[...], precision=lax.Precision.HIGHEST,
                         preferred_element_type=F32) + b_ref[...]


def _ada(c, w_ada, b_ada):
    L, D, D6 = w_ada.shape
    B = c.shape[0]
    nj = D6 // D
    return pl.pallas_call(
        _ada_kernel,
        out_shape=jax.ShapeDtypeStruct((L, B, D6), F32),
        grid=(L, nj),
        in_specs=[pl.BlockSpec((B, D), lambda l, j: (0, 0)),
                  pl.BlockSpec((None, D, D), lambda l, j: (l, 0, j)),
                  pl.BlockSpec((None, 1, D), lambda l, j: (l, 0, j))],
        out_specs=pl.BlockSpec((None, B, D), lambda l, j: (l, 0, j)),
        compiler_params=_cparams(2),
        name="ada",
    )(c, w_ada, b_ada.reshape(L, 1, D6))


def _inproj_kernel(x_ref, ada_ref, cos_ref, sin_ref,
                   w_cv, w_ssm, w_cq, w_ckv, w_kra, w_krb, w_g,
                   convw, qn, wq_a, wq_b, kvn, wk, wv, w_upc,
                   pc_ref, sgs_ref, sga_ref, u_ref, q_ref, k_ref, v_ref,
                   carry_ref):
    si = pl.program_id(1)
    D = x_ref.shape[1]
    tm = x_ref.shape[0]

    @pl.when(si == 0)
    def _():
        carry_ref[...] = jnp.zeros_like(carry_ref)

    ada = ada_ref[...]
    h = _ln_plain(x_ref[...]) * (1.0 + ada[1:2, :]) + ada[0:1, :]
    hb = h.astype(BF16)

    pcv = jnp.dot(hb, w_cv[...], preferred_element_type=F32)
    u = pcv[:, D_CONV:2 * D_CONV] * pcv[:, :D_CONV]
    gb = pcv[:, 2 * D_CONV:]
    tail = carry_ref[...]
    row8 = lax.broadcasted_iota(jnp.int32, (8, D_CONV), 0)
    r1 = pltpu.roll(u, 1, 0)
    r2 = pltpu.roll(u, 2, 0)
    t1 = pltpu.roll(tail, 1, 0)
    t2 = pltpu.roll(tail, 2, 0)
    u1 = jnp.concatenate([jnp.where(row8 < 1, t1, r1[:8]), r1[8:]], axis=0)
    u2 = jnp.concatenate([jnp.where(row8 < 2, t2, r2[:8]), r2[8:]], axis=0)
    cw = convw[...]
    yc = gb * (cw[0:1, :] * u2 + cw[1:2, :] * u1 + cw[2:3, :] * u)
    carry_ref[...] = u[tm - 8:, :]

    gates = jnp.dot(hb, w_g[...], preferred_element_type=F32)
    pc_ref[...] = (jax.nn.sigmoid(gates[:, :D]) * _bdot(yc, w_upc[...])).astype(BF16)
    sgs_ref[...] = jax.nn.sigmoid(gates[:, D:2 * D]).astype(BF16)
    sga_ref[...] = jax.nn.sigmoid(gates[:, 2 * D:]).astype(BF16)

    u_ref[...] = jnp.dot(hb, w_ssm[...], preferred_element_type=F32).astype(BF16)

    cos = cos_ref[...]
    sin = sin_ref[...]
    cq = _rms(jnp.dot(hb, w_cq[...], preferred_element_type=F32), qn[...]).astype(BF16)
    qa = jnp.dot(cq, wq_a[...], preferred_element_type=F32)
    qb = jnp.dot(cq, wq_b[...], preferred_element_type=F32)
    ckv = _rms(jnp.dot(hb, w_ckv[...], preferred_element_type=F32), kvn[...]).astype(BF16)
    kn = jnp.dot(ckv, wk[...], preferred_element_type=F32)
    kr = (jnp.dot(hb, w_kra[...], preferred_element_type=F32) * cos
          + jnp.dot(hb, w_krb[...], preferred_element_type=F32) * sin)
    for hd in range(MLA_HEADS):
        sl = slice(hd * HEAD_PAD, (hd + 1) * HEAD_PAD)
        q_ref[:, sl] = (qa[:, sl] * cos + qb[:, sl] * sin).astype(BF16)
        k_ref[:, sl] = (kn[:, sl] + kr).astype(BF16)
    v_ref[...] = jnp.dot(ckv, wv[...], preferred_element_type=F32).astype(BF16)


def _inproj(x2d, ada_l, cos128, sin128, wts, B, S):
    T, D = x2d.shape
    tm = TM_PROJ
    ns = S // tm
    row = lambda b, s: (b * ns + s, 0)
    full = lambda a: pl.BlockSpec(a.shape, lambda b, s: (0,) * a.ndim)
    names = ["w_cv", "w_ssm", "w_cq", "w_ckv", "w_kra", "w_krb", "w_g", "convw", "qn",
             "wq_a", "wq_b", "kvn", "wk", "wv", "w_upc"]
    ws = [wts[n] for n in names]
    out_shape = (
        jax.ShapeDtypeStruct((T, D), BF16),
        jax.ShapeDtypeStruct((T, D), BF16),
        jax.ShapeDtypeStruct((T, D), BF16),
        jax.ShapeDtypeStruct((S, B * D_SSM), BF16),
        jax.ShapeDtypeStruct((T, MLA_HEADS * HEAD_PAD), BF16),
        jax.ShapeDtypeStruct((T, MLA_HEADS * HEAD_PAD), BF16),
        jax.ShapeDtypeStruct((T, MLA_HEADS * V_HEAD), BF16),
    )
    out_specs = (
        pl.BlockSpec((tm, D), row), pl.BlockSpec((tm, D), row), pl.BlockSpec((tm, D), row),
        pl.BlockSpec((tm, D_SSM), lambda b, s: (s, b)),
        pl.BlockSpec((tm, MLA_HEADS * HEAD_PAD), row),
        pl.BlockSpec((tm, MLA_HEADS * HEAD_PAD), row),
        pl.BlockSpec((tm, MLA_HEADS * V_HEAD), row),
    )
    return pl.pallas_call(
        _inproj_kernel,
        out_shape=out_shape,
        grid=(B, ns),
        in_specs=[pl.BlockSpec((tm, D), row),
                  pl.BlockSpec((None, 6, D), lambda b, s: (b, 0, 0)),
                  pl.BlockSpec((tm, LANE), row), pl.BlockSpec((tm, LANE), row)]
                 + [full(a) for a in ws],
        out_specs=out_specs,
        scratch_shapes=[pltpu.VMEM((8, D_CONV), F32)],
        compiler_params=_cparams(2),
        name="inproj",
    )(x2d, ada_l, cos128, sin128, *ws)


def _ssm_kernel(u_ref, bre_ref, bim_ref, lre_ref, lim_ref, cre_ref, cim_ref, d_ref, wglu_ref,
                o_ref, sre, sim, st_re, st_im, *, batch):
    i = pl.program_id(0)
    rows = u_ref.shape[0]
    nsteps = rows // batch
    nchunk = bre_ref.shape[0]
    cw = bre_ref.shape[2]

    @pl.when(i == 0)
    def _():
        st_re[...] = jnp.zeros_like(st_re)
        st_im[...] = jnp.zeros_like(st_im)

    ub = u_ref[...]
    for c in range(nchunk):
        uc = ub[:, c * LANE:(c + 1) * LANE]
        sre[:, c * cw:(c + 1) * cw] = jnp.dot(uc, bre_ref[c], preferred_element_type=F32)
        sim[:, c * cw:(c + 1) * cw] = jnp.dot(uc, bim_ref[c], preferred_element_type=F32)

    lre = lre_ref[...]
    lim = lim_ref[...]

    def step(t, carry):
        pr, pi = carry
        r = pl.multiple_of(t * batch, batch)
        nr = lre * pr - lim * pi + sre[pl.ds(r, batch), :]
        ni = lre * pi + lim * pr + sim[pl.ds(r, batch), :]
        sre[pl.ds(r, batch), :] = nr
        sim[pl.ds(r, batch), :] = ni
        return nr, ni

    fr, fi = lax.fori_loop(0, nsteps, step, (st_re[...], st_im[...]))
    st_re[...] = fr
    st_im[...] = fi

    ys = []
    for c in range(nchunk):
        ys.append(_bdot(sre[:, c * cw:(c + 1) * cw], cre_ref[c])
                  + _bdot(sim[:, c * cw:(c + 1) * cw], cim_ref[c]))
    y = jnp.concatenate(ys, axis=1) + d_ref[...] * ub.astype(F32)
    y = 0.5 * y * (1.0 + jnp.tanh(math.sqrt(2.0 / math.pi) * (y + 0.044715 * (y * y * y))))
    z = _bdot(y, wglu_ref[...])
    o_ref[...] = (z[:, :D_SSM] * jax.nn.sigmoid(z[:, D_SSM:])).astype(BF16)


def _ssm(u_tm, wts, B, S):
    rows = T_CHUNK * B
    n = S // T_CHUNK
    nstate = SSM_GROUPS * SSM_STATE
    names = ["ssm_bre", "ssm_bim", "ssm_lre", "ssm_lim", "ssm_cre", "ssm_cim", "ssm_d", "ssm_wglu"]
    ws = [wts[k] for k in names]
    full = lambda a: pl.BlockSpec(a.shape, lambda i: (0,) * a.ndim)
    return pl.pallas_call(
        functools.partial(_ssm_kernel, batch=B),
        out_shape=jax.ShapeDtypeStruct((S * B, D_SSM), BF16),
        grid=(n,),
        in_specs=[pl.BlockSpec((rows, D_SSM), lambda i: (i, 0))] + [full(a) for a in ws],
        out_specs=pl.BlockSpec((rows, D_SSM), lambda i: (i, 0)),
        scratch_shapes=[pltpu.VMEM((rows, nstate), F32), pltpu.VMEM((rows, nstate), F32),
                        pltpu.VMEM((B, nstate), F32), pltpu.VMEM((B, nstate), F32)],
        compiler_params=_cparams(1),
        name="ssm",
    )(u_tm, *ws)


def _attn_kernel(q_ref, k_ref, v_ref, o_ref):
    qi = pl.program_id(2)
    tq = q_ref.shape[0]
    neg = -1e30
    rowi = lax.broadcasted_iota(jnp.int32, (tq, tq), 0)
    coli = lax.broadcasted_iota(jnp.int32, (tq, tq), 1)
    causal = coli <= rowi
    outs = []
    for hh in range(2):
        q = q_ref[:, hh * HEAD_PAD:(hh + 1) * HEAD_PAD]

        def block(j, carry, masked):
            m, l, acc = carry
            off = pl.multiple_of(j * tq, tq)
            kb = k_ref[pl.ds(off, tq), hh * HEAD_PAD:(hh + 1) * HEAD_PAD]
            s = lax.dot_general(q, kb, (((1,), (1,)), ((), ())), preferred_element_type=F32)
            if masked:
                s = jnp.where(causal, s, neg)
            m_new = jnp.maximum(m, jnp.max(s, axis=-1, keepdims=True))
            p = jnp.exp(s - m_new)
            a = jnp.exp(m - m_new)
            l = a * l + jnp.sum(p, axis=-1, keepdims=True)
            acc = a * acc + jnp.dot(p.astype(BF16), v_ref[pl.ds(off, tq), :],
                                    preferred_element_type=F32)
            return m_new, l, acc

        init = (jnp.full((tq, 1), neg, F32), jnp.zeros((tq, 1), F32),
                jnp.zeros((tq, 2 * V_HEAD), F32))
        carry = lax.fori_loop(0, qi, functools.partial(block, masked=False), init)
        m, l, acc = block(qi, carry, True)
        outs.append(acc / l)
    lane = lax.broadcasted_iota(jnp.int32, (tq, 2 * V_HEAD), 1)
    o_ref[...] = jnp.where(lane < V_HEAD, outs[0], outs[1]).astype(BF16)


def _attention(q, k, v, B, S):
    nq = S // TQ
    hp = MLA_HEADS // 2
    return pl.pallas_call(
        _attn_kernel,
        out_shape=jax.ShapeDtypeStruct((B * S, MLA_HEADS * V_HEAD), BF16),
        grid=(B, hp, nq),
        in_specs=[pl.BlockSpec((TQ, 2 * HEAD_PAD), lambda b, h, i: (b * nq + i, h)),
                  pl.BlockSpec((S, 2 * HEAD_PAD), lambda b, h, i: (b, h)),
                  pl.BlockSpec((S, 2 * V_HEAD), lambda b, h, i: (b, h))],
        out_specs=pl.BlockSpec((TQ, 2 * V_HEAD), lambda b, h, i: (b * nq + i, h)),
        compiler_params=_cparams(3),
        name="attention",
    )(q, k, v)


def _merge_kernel(x_ref, ada_ref, pc_ref, sgs_ref, sga_ref, ys_ref, ya_ref,
                  w_ups, w_upa, w_o, lng, lnb, rw, rb, ws1, ws3, ws2,
                  base_ref, hr_ref, *, alpha):
    D = x_ref.shape[1]
    tm = x_ref.shape[0]
    ada = ada_ref[...]
    merged = (pc_ref[...].astype(F32)
              + sgs_ref[...].astype(F32) * jnp.dot(ys_ref[...], w_ups[...], preferred_element_type=F32)
              + sga_ref[...].astype(F32) * jnp.dot(ya_ref[...], w_upa[...], preferred_element_type=F32))
    y = _bdot(merged, w_o[...])
    x1 = _ln_plain(alpha * x_ref[...] + ada[2:3, :] * y) * lng[...] + lnb[...]
    h2 = _ln_plain(x1) * (1.0 + ada[4:5, :]) + ada[3:4, :]
    hr_ref[:, :D] = h2

    logits = jnp.dot(h2, rw[...], precision=lax.Precision.HIGHEST, preferred_element_type=F32)
    sc0 = jax.nn.sigmoid(logits)
    sl0 = sc0 + rb[...]
    sc = [sc0] + [pltpu.roll(sc0, LANE - 32 * j, 1) for j in range(1, EXPERTS_PER_GROUP)]
    sl = [sl0] + [pltpu.roll(sl0, LANE - 32 * j, 1) for j in range(1, EXPERTS_PER_GROUP)]
    hi1, lo1 = jnp.maximum(sl[0], sl[1]), jnp.minimum(sl[0], sl[1])
    hi2, lo2 = jnp.maximum(sl[2], sl[3]), jnp.minimum(sl[2], sl[3])
    gscore = jnp.maximum(hi1, hi2) + jnp.maximum(jnp.minimum(hi1, hi2), jnp.maximum(lo1, lo2))
    lane = lax.broadcasted_iota(jnp.int32, (tm, LANE), 1)
    gscore = jnp.where(lane < N_EXPERT_GROUPS, gscore, -jnp.inf)
    gmax = jnp.max(gscore, axis=-1, keepdims=True)
    gidx = jnp.min(jnp.where(gscore == gmax, lane, LANE), axis=-1, keepdims=True)
    pick = lane == gidx
    vs = [jnp.sum(jnp.where(pick, a, 0.0), axis=-1, keepdims=True) for a in sl]
    ss = [jnp.sum(jnp.where(pick, a, 0.0), axis=-1, keepdims=True) for a in sc]
    chosen = []
    for j in range(EXPERTS_PER_GROUP):
        cnt = jnp.zeros((tm, 1), jnp.int32)
        for k in range(EXPERTS_PER_GROUP):
            if k == j:
                continue
            beats = (vs[k] > vs[j]) | ((vs[k] == vs[j]) & (k < j))
            cnt = cnt + beats.astype(jnp.int32)
        chosen.append(cnt < 2)
    wsel = [jnp.where(chosen[j], ss[j], 0.0) for j in range(EXPERTS_PER_GROUP)]
    wsum = wsel[0] + wsel[1] + wsel[2] + wsel[3]
    route = jnp.where(lane == EXPERTS_PER_GROUP, gidx.astype(F32), 0.0)
    for j in range(EXPERTS_PER_GROUP):
        route = jnp.where(lane == j, wsel[j] / wsum, route)
    hr_ref[:, D:] = route

    h2b = h2.astype(BF16)
    a1 = jnp.dot(h2b, ws1[...], preferred_element_type=F32)
    a3 = jnp.dot(h2b, ws3[...], preferred_element_type=F32)
    ysh = _bdot(a1 * jax.nn.sigmoid(a1) * a3, ws2[...])
    base_ref[...] = alpha * x1 + ada[5:6, :] * ysh


def _merge(x2d, ada_l, pc, sgs, sga, ys_tm, ya, wts, B, S, alpha):
    T, D = x2d.shape
    tm = TM_PROJ
    ns = S // tm
    row = lambda b, s: (b * ns + s, 0)
    names = ["w_ups", "w_upa", "w_o", "lng0", "lnb0", "rw", "rb", "ws1", "ws3", "ws2"]
    ws = [wts[n] for n in names]
    full = lambda a: pl.BlockSpec(a.shape, lambda b, s: (0,) * a.ndim)
    return pl.pallas_call(
        functools.partial(_merge_kernel, alpha=alpha),
        out_shape=(jax.ShapeDtypeStruct((T, D), F32),
                   jax.ShapeDtypeStruct((T, D + ROUTE_W), F32)),
        grid=(B, ns),
        in_specs=[pl.BlockSpec((tm, D), row),
                  pl.BlockSpec((None, 6, D), lambda b, s: (b, 0, 0)),
                  pl.BlockSpec((tm, D), row), pl.BlockSpec((tm, D), row), pl.BlockSpec((tm, D), row),
                  pl.BlockSpec((tm, D_SSM), lambda b, s: (s, b)),
                  pl.BlockSpec((tm, MLA_HEADS * V_HEAD), row)]
                 + [full(a) for a in ws],
        out_specs=(pl.BlockSpec((tm, D), row), pl.BlockSpec((tm, D + ROUTE_W), row)),
        compiler_params=_cparams(2),
        name="merge",
    )(x2d, ada_l, pc, sgs, sga, ys_tm, ya, *ws)


def _moe_kernel(bg_ref, tok_ref, hr_hbm, w1_ref, w3_ref, w2_ref, y_hbm, xbuf, ybuf, gsem, ssem):
    i = pl.program_id(0)
    R = xbuf.shape[0]
    D = ybuf.shape[1]
    base = i * R

    @pl.when(tok_ref[base] >= 0)
    def _():
        def gather(r, _):
            t = jnp.maximum(tok_ref[base + r], 0)
            pltpu.make_async_copy(hr_hbm.at[pl.ds(t, 1)], xbuf.at[pl.ds(r, 1)], gsem).start()
            return 0

        lax.fori_loop(0, R, gather, 0)
        pltpu.make_async_copy(hr_hbm.at[pl.ds(0, R)], xbuf, gsem).wait()

        xb = xbuf[:, :D].astype(BF16)
        acc = jnp.zeros((R, D), F32)
        for j in range(EXPERTS_PER_GROUP):
            a1 = jnp.dot(xb, w1_ref[j], preferred_element_type=F32)
            a3 = jnp.dot(xb, w3_ref[j], preferred_element_type=F32)
            hj = a1 * jax.nn.sigmoid(a1) * a3 * xbuf[:, D + j:D + j + 1]
            acc = acc + _bdot(hj, w2_ref[j])
        ybuf[...] = acc

        def scatter(r, n):
            t = tok_ref[base + r]

            @pl.when(t >= 0)
            def _():
                pltpu.make_async_copy(ybuf.at[pl.ds(r, 1)], y_hbm.at[pl.ds(t, 1)], ssem).start()

            return n + (t >= 0).astype(jnp.int32)

        nvalid = lax.fori_loop(0, R, scatter, 0)

        def drain(r, _):
            pltpu.make_async_copy(ybuf.at[pl.ds(0, 1)], y_hbm.at[pl.ds(0, 1)], ssem).wait()
            return 0

        lax.fori_loop(0, nvalid, drain, 0)


def _moe(hr, block_g, row_tok, w1, w3, w2):
    T = hr.shape[0]
    D = w1.shape[1]
    R = MOE_ROWS
    nb = row_tok.shape[0] // R
    wspec_in = pl.BlockSpec((EXPERTS_PER_GROUP, D, D_EXPERT), lambda i, bg, tok: (bg[i], 0, 0))
    wspec_out = pl.BlockSpec((EXPERTS_PER_GROUP, D_EXPERT, D), lambda i, bg, tok: (bg[i], 0, 0))
    return pl.pallas_call(
        _moe_kernel,
        out_shape=jax.ShapeDtypeStruct((T, D), F32),
        grid_spec=pltpu.PrefetchScalarGridSpec(
            num_scalar_prefetch=2,
            grid=(nb,),
            in_specs=[pl.BlockSpec(memory_space=pl.ANY), wspec_in, wspec_in, wspec_out],
            out_specs=pl.BlockSpec(memory_space=pl.ANY),
            scratch_shapes=[pltpu.VMEM((R, D + ROUTE_W), F32), pltpu.VMEM((R, D), F32),
                            pltpu.SemaphoreType.DMA, pltpu.SemaphoreType.DMA]),
        compiler_params=_cparams(1),
        name="moe",
    )(block_g, row_tok, hr, w1, w3, w2)


def _route_plan(gid, T):
    R = MOE_ROWS
    G = N_EXPERT_GROUPS
    onehot = (gid[:, None] == jnp.arange(G, dtype=jnp.int32)[None, :]).astype(jnp.int32)
    csum = jnp.cumsum(onehot, axis=0)
    counts = csum[-1]
    rank = jnp.sum(onehot * csum, axis=1) - 1
    padded = (counts + R - 1) // R * R
    pad_end = jnp.cumsum(padded)
    pad_start = pad_end - padded
    dest = pad_start[gid] + rank
    nb = T // R + G
    row_tok = jnp.full((nb * R,), -1, jnp.int32).at[dest].set(jnp.arange(T, dtype=jnp.int32))
    starts = jnp.arange(nb, dtype=jnp.int32) * R
    block_g = jnp.minimum(jnp.searchsorted(pad_end, starts, side="right"), G - 1).astype(jnp.int32)
    return block_g, row_tok


def _final_kernel(base_ref, y_ref, ada_ref, lng, lnb, o_ref):
    ada = ada_ref[...]
    o_ref[...] = _ln_plain(base_ref[...] + ada[5:6, :] * y_ref[...]) * lng[...] + lnb[...]


def _final(base, yr, ada_l, lng, lnb, B, S):
    T, D = base.shape
    tm = 512
    ns = S // tm
    row = lambda b, s: (b * ns + s, 0)
    return pl.pallas_call(
        _final_kernel,
        out_shape=jax.ShapeDtypeStruct((T, D), F32),
        grid=(B, ns),
        in_specs=[pl.BlockSpec((tm, D), row), pl.BlockSpec((tm, D), row),
                  pl.BlockSpec((None, 6, D), lambda b, s: (b, 0, 0)),
                  pl.BlockSpec((1, D), lambda b, s: (0, 0)), pl.BlockSpec((1, D), lambda b, s: (0, 0))],
        out_specs=pl.BlockSpec((tm, D), row),
        compiler_params=_cparams(2),
        name="final_ln",
    )(base, yr, ada_l, lng, lnb)


def _rope_tables(positions):
    inv_freq = ROPE_THETA ** (-jnp.arange(0, QK_ROPE, 2, dtype=F32) / QK_ROPE)
    ang = positions.astype(F32)[..., None] * inv_freq
    cos, sin = jnp.cos(ang), jnp.sin(ang)
    ones = jnp.ones(cos.shape[:-1] + (QK_NOPE,), F32)
    zpad = jnp.zeros(cos.shape[:-1] + (HEAD_PAD - QK_NOPE - QK_ROPE,), F32)
    cos128 = jnp.concatenate([ones, cos, cos, zpad], axis=-1)
    sin128 = jnp.concatenate([0.0 * ones, -sin, sin, zpad], axis=-1)
    T = positions.shape[0] * positions.shape[1]
    return cos128.reshape(T, HEAD_PAD), sin128.reshape(T, HEAD_PAD)


def _swap_halves(w):
    half = w.shape[-1] // 2
    return jnp.concatenate([w[..., half:], w[..., :half]], axis=-1)


def _layer_weights(l, p):
    D = p["w_in"].shape[1]
    w_in = p["w_in"][l]
    o = 0
    seg = {}
    for name, width in (("cv", 3 * D_CONV), ("ssm", D_SSM), ("cq", Q_LORA), ("ckv", KV_LORA),
                        ("kr", QK_ROPE), ("g", 3 * D)):
        seg[name] = w_in[:, o:o + width]
        o += width
    pad_r = HEAD_PAD - QK_NOPE - QK_ROPE
    zl = jnp.zeros((D, QK_NOPE), F32)
    zr = jnp.zeros((D, pad_r), F32)
    w = {
        "w_cv": seg["cv"].astype(BF16), "w_ssm": seg["ssm"].astype(BF16),
        "w_cq": seg["cq"].astype(BF16), "w_ckv": seg["ckv"].astype(BF16),
        "w_kra": jnp.concatenate([zl, seg["kr"], zr], axis=1).astype(BF16),
        "w_krb": jnp.concatenate([zl, _swap_halves(seg["kr"]), zr], axis=1).astype(BF16),
        "w_g": seg["g"].astype(BF16),
        "convw": p["conv_w"][l],
        "qn": p["q_norm"][l].reshape(1, Q_LORA), "kvn": p["kv_norm"][l].reshape(1, KV_LORA),
        "w_upc": p["w_up_conv"][l].astype(BF16),
    }
    scale = (QK_NOPE + QK_ROPE) ** -0.5
    wq = p["w_uq"][l].reshape(Q_LORA, MLA_HEADS, QK_NOPE + QK_ROPE) * scale
    zq = jnp.zeros((Q_LORA, MLA_HEADS, pad_r), F32)
    zn = jnp.zeros((Q_LORA, MLA_HEADS, QK_NOPE), F32)
    w["wq_a"] = jnp.concatenate([wq, zq], axis=-1).reshape(Q_LORA, -1).astype(BF16)
    w["wq_b"] = jnp.concatenate([zn, _swap_halves(wq[..., QK_NOPE:]), zq],
                                axis=-1).reshape(Q_LORA, -1).astype(BF16)
    wk = p["w_uk"][l].reshape(KV_LORA, MLA_HEADS, QK_NOPE)
    zk = jnp.zeros((KV_LORA, MLA_HEADS, HEAD_PAD - QK_NOPE), F32)
    w["wk"] = jnp.concatenate([wk, zk], axis=-1).reshape(KV_LORA, -1).astype(BF16)
    w["wv"] = p["w_uv"][l].astype(BF16)

    ar, ai = p["ssm_a_re"][l], p["ssm_a_im"][l]
    dt = jnp.exp(p["ssm_log_dt"][l])[:, None]
    mag = jnp.exp(ar * dt)
    lb_re, lb_im = mag * jnp.cos(ai * dt), mag * jnp.sin(ai * dt)
    den = ar * ar + ai * ai
    nr, ni = lb_re - 1.0, lb_im
    f_re, f_im = (nr * ar + ni * ai) / den, (ni * ar - nr * ai) / den
    br, bi = p["ssm_b_re"][l], p["ssm_b_im"][l]
    bb_re = f_re[..., None] * br - f_im[..., None] * bi
    bb_im = f_re[..., None] * bi + f_im[..., None] * br
    gpc = LANE // SSM_GROUP_DIM
    nch = SSM_GROUPS // gpc
    eye = jnp.eye(gpc, dtype=F32)

    def bmat(bb):
        return jnp.einsum("cgnp,gh->cgphn", bb.reshape(nch, gpc, SSM_STATE, SSM_GROUP_DIM),
                          eye).reshape(nch, LANE, gpc * SSM_STATE).astype(BF16)

    def cmat(cc):
        return jnp.einsum("cgpn,gh->cgnhp", cc.reshape(nch, gpc, SSM_GROUP_DIM, SSM_STATE),
                          eye).reshape(nch, gpc * SSM_STATE, LANE).astype(BF16)

    w["ssm_bre"], w["ssm_bim"] = bmat(bb_re), bmat(bb_im)
    w["ssm_cre"], w["ssm_cim"] = cmat(p["ssm_c_re"][l]), cmat(-p["ssm_c_im"][l])
    w["ssm_lre"] = lb_re.reshape(1, -1)
    w["ssm_lim"] = lb_im.reshape(1, -1)
    w["ssm_d"] = p["ssm_d"][l].reshape(1, D_SSM)
    w["ssm_wglu"] = p["ssm_w_glu"][l].astype(BF16)

    w["w_ups"] = p["w_up_ssm"][l].astype(BF16)
    w["w_upa"] = p["w_up_attn"][l].astype(BF16)
    w["w_o"] = p["w_o"][l].astype(BF16)
    w["lng0"], w["lnb0"] = p["ln_g"][l, 0].reshape(1, D), p["ln_b"][l, 0].reshape(1, D)
    w["lng1"], w["lnb1"] = p["ln_g"][l, 1].reshape(1, D), p["ln_b"][l, 1].reshape(1, D)
    rwg = p["router_w"].reshape(D, N_EXPERT_GROUPS, EXPERTS_PER_GROUP)
    rbg = p["router_bias"].reshape(1, N_EXPERT_GROUPS, EXPERTS_PER_GROUP)

    def lanes(a):
        a = jnp.swapaxes(a, 1, 2)
        a = jnp.pad(a, ((0, 0), (0, 0), (0, LANE // EXPERTS_PER_GROUP - N_EXPERT_GROUPS)))
        return a.reshape(a.shape[0], LANE)

    w["rw"], w["rb"] = lanes(rwg), lanes(rbg)
    w["ws1"] = p["shared_w1"][l].astype(BF16)
    w["ws3"] = p["shared_w3"][l].astype(BF16)
    w["ws2"] = p["shared_w2"][l].astype(BF16)
    w["e1"] = p["exp_w1"][l].astype(BF16)
    w["e3"] = p["exp_w3"][l].astype(BF16)
    w["e2"] = p["exp_w2"][l].astype(BF16)
    return w


def kernel(x, c, positions, w_ada, b_ada, w_in, conv_w, ssm_a_re, ssm_a_im, ssm_b_re, ssm_b_im,
           ssm_c_re, ssm_c_im, ssm_d, ssm_log_dt, ssm_w_glu, q_norm, w_uq, kv_norm, w_uk, w_uv,
           w_up_conv, w_up_ssm, w_up_attn, w_o, ln_g, ln_b, router_w, router_bias,
           exp_w1, exp_w3, exp_w2, shared_w1, shared_w3, shared_w2):
    B, S, D = x.shape
    L = w_in.shape[0]
    T = B * S
    assert B == 8 and S % max(TM_PROJ, T_CHUNK, TQ, 512) == 0 and T % MOE_ROWS == 0
    p = dict(w_in=w_in, conv_w=conv_w, ssm_a_re=ssm_a_re, ssm_a_im=ssm_a_im, ssm_b_re=ssm_b_re,
             ssm_b_im=ssm_b_im, ssm_c_re=ssm_c_re, ssm_c_im=ssm_c_im, ssm_d=ssm_d,
             ssm_log_dt=ssm_log_dt, ssm_w_glu=ssm_w_glu, q_norm=q_norm, w_uq=w_uq, kv_norm=kv_norm,
             w_uk=w_uk, w_uv=w_uv, w_up_conv=w_up_conv, w_up_ssm=w_up_ssm, w_up_attn=w_up_attn,
             w_o=w_o, ln_g=ln_g, ln_b=ln_b, router_w=router_w, router_bias=router_bias,
             exp_w1=exp_w1, exp_w3=exp_w3, exp_w2=exp_w2, shared_w1=shared_w1,
             shared_w3=shared_w3, shared_w2=shared_w2)
    alpha = (2 * L) ** 0.25
    cos128, sin128 = _rope_tables(positions)
    ada = _ada(c, w_ada, b_ada).reshape(L, B, 6, D)
    x2d = x.reshape(T, D)
    for l in range(L):
        wts = _layer_weights(l, p)
        pc, sgs, sga, u_tm, q, k, v = _inproj(x2d, ada[l], cos128, sin128, wts, B, S)
        ys = _ssm(u_tm.reshape(S * B, D_SSM), wts, B, S).reshape(S, B * D_SSM)
        ya = _attention(q, k, v, B, S)
        base, hr = _merge(x2d, ada[l], pc, sgs, sga, ys, ya, wts, B, S, alpha)
        gid = hr[:, D + EXPERTS_PER_GROUP].astype(jnp.int32)
        block_g, row_tok = _route_plan(gid, T)
        yr = _moe(hr, block_g, row_tok, wts["e1"], wts["e3"], wts["e2"])
        x2d = _final(base, yr, ada[l], wts["lng1"], wts["lnb1"], B, S)
    return x2d.reshape(B, S, D)
```

```python
import functools
import math

import jax
import jax.numpy as jnp
from jax import lax
from jax.experimental import pallas as pl
from jax.experimental.pallas import tpu as pltpu

F32 = jnp.float32
BF16 = jnp.bfloat16

D_CONV = 512
D_SSM = 512
SSM_GROUP_DIM = 16
SSM_GROUPS = 32
SSM_STATE = 64
MLA_HEADS = 8
QK_NOPE = 64
QK_ROPE = 32
V_HEAD = 64
Q_LORA = 256
KV_LORA = 128
ROPE_THETA = 10000.0
N_EXPERTS = 32
N_EXPERT_GROUPS = 8
EXPERTS_PER_GROUP = 4
D_EXPERT = 256
LN_EPS = 1e-5
RMS_EPS = 1e-6

LANE = 128
HEAD_PAD = 128
VMEM_LIMIT = 56 * 1024 * 1024

TM_PROJ = 256
T_CHUNK = 64
TQ = 512
MOE_ROWS = 256
ROUTE_W = 128


def _cparams(n_axes):
    return pltpu.CompilerParams(dimension_semantics=("arbitrary",) * n_axes,
                                vmem_limit_bytes=VMEM_LIMIT)


def _ln_plain(x):
    mu = jnp.mean(x, axis=-1, keepdims=True)
    xc = x - mu
    var = jnp.mean(xc * xc, axis=-1, keepdims=True)
    return xc * lax.rsqrt(var + LN_EPS)


def _rms(x, g):
    return x * lax.rsqrt(jnp.mean(x * x, axis=-1, keepdims=True) + RMS_EPS) * g


def _bdot(a, b):
    return jnp.dot(a.astype(BF16), b, preferred_element_type=F32)


def _ada_kernel(c_ref, w_ref, b_ref, o_ref):
    c = c_ref[...]
    cond = c * jax.nn.sigmoid(c)
    o_ref[...] = jnp.dot(cond, w_ref[...], precision=lax.Precision.HIGHEST,
                         preferred_element_type=F32) + b_ref[...]


def _ada(c, w_ada, b_ada):
    L, D, D6 = w_ada.shape
    B = c.shape[0]
    nj = D6 // D
    return pl.pallas_call(
        _ada_kernel,
        out_shape=jax.ShapeDtypeStruct((L, B, D6), F32),
        grid=(L, nj),
        in_specs=[pl.BlockSpec((B, D), lambda l, j: (0, 0)),
                  pl.BlockSpec((None, D, D), lambda l, j: (l, 0, j)),
                  pl.BlockSpec((None, 1, D), lambda l, j: (l, 0, j))],
        out_specs=pl.BlockSpec((None, B, D), lambda l, j: (l, 0, j)),
        compiler_params=_cparams(2),
        name="ada",
    )(c, w_ada, b_ada.reshape(L, 1, D6))


def _inproj_kernel(x_ref, ada_ref, cos_ref, sin_ref,
                   w_cv, w_ssm, w_cq, w_ckv, w_kra, w_krb, w_g,
                   convw, qn, wq_a, wq_b, kvn, wk, wv, w_upc,
                   pc_ref, sgs_ref, sga_ref, u_ref, q_ref, k_ref, v_ref,
                   carry_ref):
    si = pl.program_id(1)
    D = x_ref.shape[1]
    tm = x_ref.shape[0]

    @pl.when(si == 0)
    def _():
        carry_ref[...] = jnp.zeros_like(carry_ref)

    ada = ada_ref[...]
    h = _ln_plain(x_ref[...]) * (1.0 + ada[1:2, :]) + ada[0:1, :]
    hb = h.astype(BF16)

    pcv = jnp.dot(hb, w_cv[...], preferred_element_type=F32)
    u = pcv[:, D_CONV:2 * D_CONV] * pcv[:, :D_CONV]
    gb = pcv[:, 2 * D_CONV:]
    tail = carry_ref[...]
    row8 = lax.broadcasted_iota(jnp.int32, (8, D_CONV), 0)
    r1 = pltpu.roll(u, 1, 0)
    r2 = pltpu.roll(u, 2, 0)
    t1 = pltpu.roll(tail, 1, 0)
    t2 = pltpu.roll(tail, 2, 0)
    u1 = jnp.concatenate([jnp.where(row8 < 1, t1, r1[:8]), r1[8:]], axis=0)
    u2 = jnp.concatenate([jnp.where(row8 < 2, t2, r2[:8]), r2[8:]], axis=0)
    cw = convw[...]
    yc = gb * (cw[0:1, :] * u2 + cw[1:2, :] * u1 + cw[2:3, :] * u)
    carry_ref[...] = u[tm - 8:, :]

    gates = jnp.dot(hb, w_g[...], preferred_element_type=F32)
    pc_ref[...] = (jax.nn.sigmoid(gates[:, :D]) * _bdot(yc, w_upc[...])).astype(BF16)
    sgs_ref[...] = jax.nn.sigmoid(gates[:, D:2 * D]).astype(BF16)
    sga_ref[...] = jax.nn.sigmoid(gates[:, 2 * D:]).astype(BF16)

    u_ref[...] = jnp.dot(hb, w_ssm[...], preferred_element_type=F32).astype(BF16)

    cos = cos_ref[...]
    sin = sin_ref[...]
    cq = _rms(jnp.dot(hb, w_cq[...], preferred_element_type=F32), qn[...]).astype(BF16)
    qa = jnp.dot(cq, wq_a[...], preferred_element_type=F32)
    qb = jnp.dot(cq, wq_b[...], preferred_element_type=F32)
    ckv = _rms(jnp.dot(hb, w_ckv[...], preferred_element_type=F32), kvn[...]).astype(BF16)
    kn = jnp.dot(ckv, wk[...], preferred_element_type=F32)
    kr = (jnp.dot(hb, w_kra[...], preferred_element_type=F32) * cos
          + jnp.dot(hb, w_krb[...], preferred_element_type=F32) * sin)
    for hd in range(MLA_HEADS):
        sl = slice(hd * HEAD_PAD, (hd + 1) * HEAD_PAD)
        q_ref[:, sl] = (qa[:, sl] * cos + qb[:, sl] * sin).astype(BF16)
        k_ref[:, sl] = (kn[:, sl] + kr).astype(BF16)
    v_ref[...] = jnp.dot(ckv, wv[...], preferred_element_type=F32).astype(BF16)


def _inproj(x2d, ada_l, cos128, sin128, wts, B, S):
    T, D = x2d.shape
    tm = TM_PROJ
    ns = S // tm
    row = lambda b, s: (b * ns + s, 0)
    full = lambda a: pl.BlockSpec(a.shape, lambda b, s: (0,) * a.ndim)
    names = ["w_cv", "w_ssm", "w_cq", "w_ckv", "w_kra", "w_krb", "w_g", "convw", "qn",
             "wq_a", "wq_b", "kvn", "wk", "wv", "w_upc"]
    ws = [wts[n] for n in names]
    out_shape = (
        jax.ShapeDtypeStruct((T, D), BF16),
        jax.ShapeDtypeStruct((T, D), BF16),
        jax.ShapeDtypeStruct((T, D), BF16),
        jax.ShapeDtypeStruct((S, B * D_SSM), BF16),
        jax.ShapeDtypeStruct((T, MLA_HEADS * HEAD_PAD), BF16),
        jax.ShapeDtypeStruct((T, MLA_HEADS * HEAD_PAD), BF16),
        jax.ShapeDtypeStruct((T, MLA_HEADS * V_HEAD), BF16),
    )
    out_specs = (
        pl.BlockSpec((tm, D), row), pl.BlockSpec((tm, D), row), pl.BlockSpec((tm, D), row),
        pl.BlockSpec((tm, D_SSM), lambda b, s: (s, b)),
        pl.BlockSpec((tm, MLA_HEADS * HEAD_PAD), row),
        pl.BlockSpec((tm, MLA_HEADS * HEAD_PAD), row),
        pl.BlockSpec((tm, MLA_HEADS * V_HEAD), row),
    )
    return pl.pallas_call(
        _inproj_kernel,
        out_shape=out_shape,
        grid=(B, ns),
        in_specs=[pl.BlockSpec((tm, D), row),
                  pl.BlockSpec((None, 6, D), lambda b, s: (b, 0, 0)),
                  pl.BlockSpec((tm, LANE), row), pl.BlockSpec((tm, LANE), row)]
                 + [full(a) for a in ws],
        out_specs=out_specs,
        scratch_shapes=[pltpu.VMEM((8, D_CONV), F32)],
        compiler_params=_cparams(2),
        name="inproj",
    )(x2d, ada_l, cos128, sin128, *ws)


def _ssm_kernel(u_ref, bre_ref, bim_ref, lre_ref, lim_ref, cre_ref, cim_ref, d_ref, wglu_ref,
                o_ref, sre, sim, st_re, st_im, *, batch):
    i = pl.program_id(0)
    rows = u_ref.shape[0]
    nsteps = rows // batch
    nchunk = bre_ref.shape[0]
    cw = bre_ref.shape[2]

    @pl.when(i == 0)
    def _():
        st_re[...] = jnp.zeros_like(st_re)
        st_im[...] = jnp.zeros_like(st_im)

    ub = u_ref[...]
    for c in range(nchunk):
        uc = ub[:, c * LANE:(c + 1) * LANE]
        sre[:, c * cw:(c + 1) * cw] = jnp.dot(uc, bre_ref[c], preferred_element_type=F32)
        sim[:, c * cw:(c + 1) * cw] = jnp.dot(uc, bim_ref[c], preferred_element_type=F32)

    lre = lre_ref[...]
    lim = lim_ref[...]

    def step(t, carry):
        pr, pi = carry
        r = pl.multiple_of(t * batch, batch)
        nr = lre * pr - lim * pi + sre[pl.ds(r, batch), :]
        ni = lre * pi + lim * pr + sim[pl.ds(r, batch), :]
        sre[pl.ds(r, batch), :] = nr
        sim[pl.ds(r, batch), :] = ni
        return nr, ni

    fr, fi = lax.fori_loop(0, nsteps, step, (st_re[...], st_im[...]))
    st_re[...] = fr
    st_im[...] = fi

    ys = []
    for c in range(nchunk):
        ys.append(_bdot(sre[:, c * cw:(c + 1) * cw], cre_ref[c])
                  + _bdot(sim[:, c * cw:(c + 1) * cw], cim_ref[c]))
    y = jnp.concatenate(ys, axis=1) + d_ref[...] * ub.astype(F32)
    y = 0.5 * y * (1.0 + jnp.tanh(math.sqrt(2.0 / math.pi) * (y + 0.044715 * (y * y * y))))
    z = _bdot(y, wglu_ref[...])
    o_ref[...] = (z[:, :D_SSM] * jax.nn.sigmoid(z[:, D_SSM:])).astype(BF16)


def _ssm(u_tm, wts, B, S):
    rows = T_CHUNK * B
    n = S // T_CHUNK
    nstate = SSM_GROUPS * SSM_STATE
    names = ["ssm_bre", "ssm_bim", "ssm_lre", "ssm_lim", "ssm_cre", "ssm_cim", "ssm_d", "ssm_wglu"]
    ws = [wts[k] for k in names]
    full = lambda a: pl.BlockSpec(a.shape, lambda i: (0,) * a.ndim)
    return pl.pallas_call(
        functools.partial(_ssm_kernel, batch=B),
        out_shape=jax.ShapeDtypeStruct((S * B, D_SSM), BF16),
        grid=(n,),
        in_specs=[pl.BlockSpec((rows, D_SSM), lambda i: (i, 0))] + [full(a) for a in ws],
        out_specs=pl.BlockSpec((rows, D_SSM), lambda i: (i, 0)),
        scratch_shapes=[pltpu.VMEM((rows, nstate), F32), pltpu.VMEM((rows, nstate), F32),
                        pltpu.VMEM((B, nstate), F32), pltpu.VMEM((B, nstate), F32)],
        compiler_params=_cparams(1),
        name="ssm",
    )(u_tm, *ws)


def _attn_kernel(q_ref, k_ref, v_ref, o_ref, m_sc, l_sc, acc_sc):
    qi = pl.program_id(2)
    tq = q_ref.shape[0]
    neg = -1e30
    nrep = tq // LANE
    m_sc[...] = jnp.full(m_sc.shape, neg, F32)
    l_sc[...] = jnp.zeros(l_sc.shape, F32)
    acc_sc[...] = jnp.zeros(acc_sc.shape, F32)

    def block(j, masked):
        off = pl.multiple_of(j * tq, tq)
        vb = v_ref[pl.ds(off, tq), :]
        for hh in range(2):
            q = q_ref[:, hh * HEAD_PAD:(hh + 1) * HEAD_PAD]
            kb = k_ref[pl.ds(off, tq), hh * HEAD_PAD:(hh + 1) * HEAD_PAD]
            s = lax.dot_general(q, kb, (((1,), (1,)), ((), ())), preferred_element_type=F32)
            if masked:
                rowi = lax.broadcasted_iota(jnp.int32, (tq, tq), 0)
                coli = lax.broadcasted_iota(jnp.int32, (tq, tq), 1)
                s = jnp.where(coli <= rowi, s, neg)
            m_prev = m_sc[hh]
            m_new = jnp.maximum(m_prev, jnp.max(s, axis=-1, keepdims=True))
            p = jnp.exp2(s - jnp.concatenate([m_new] * nrep, axis=1))
            a = jnp.exp2(m_prev - m_new)
            l_sc[hh] = a * l_sc[hh] + jnp.sum(p, axis=-1, keepdims=True)
            acc_sc[hh] = a * acc_sc[hh] + jnp.dot(p.astype(BF16), vb, preferred_element_type=F32)
            m_sc[hh] = m_new

    def full_block(j, c):
        block(j, False)
        return c

    lax.fori_loop(0, qi, full_block, 0)
    block(qi, True)
    lane = lax.broadcasted_iota(jnp.int32, (tq, 2 * V_HEAD), 1)
    o_ref[...] = jnp.where(lane < V_HEAD, acc_sc[0] / l_sc[0], acc_sc[1] / l_sc[1]).astype(BF16)


def _attention(q, k, v, B, S):
    nq = S // TQ
    hp = MLA_HEADS // 2
    return pl.pallas_call(
        _attn_kernel,
        out_shape=jax.ShapeDtypeStruct((B * S, MLA_HEADS * V_HEAD), BF16),
        grid=(B, hp, nq),
        in_specs=[pl.BlockSpec((TQ, 2 * HEAD_PAD), lambda b, h, i: (b * nq + i, h)),
                  pl.BlockSpec((S, 2 * HEAD_PAD), lambda b, h, i: (b, h)),
                  pl.BlockSpec((S, 2 * V_HEAD), lambda b, h, i: (b, h))],
        out_specs=pl.BlockSpec((TQ, 2 * V_HEAD), lambda b, h, i: (b * nq + i, h)),
        scratch_shapes=[pltpu.VMEM((2, TQ, LANE), F32), pltpu.VMEM((2, TQ, LANE), F32),
                        pltpu.VMEM((2, TQ, 2 * V_HEAD), F32)],
        compiler_params=_cparams(3),
        name="attention",
    )(q, k, v)


def _merge_kernel(x_ref, ada_ref, pc_ref, sgs_ref, sga_ref, ys_ref, ya_ref,
                  w_ups, w_upa, w_o, lng, lnb, rw, rb, ws1, ws3, ws2,
                  base_ref, hr_ref, *, alpha):
    D = x_ref.shape[1]
    tm = x_ref.shape[0]
    ada = ada_ref[...]
    merged = (pc_ref[...].astype(F32)
              + sgs_ref[...].astype(F32) * jnp.dot(ys_ref[...], w_ups[...], preferred_element_type=F32)
              + sga_ref[...].astype(F32) * jnp.dot(ya_ref[...], w_upa[...], preferred_element_type=F32))
    y = _bdot(merged, w_o[...])
    x1 = _ln_plain(alpha * x_ref[...] + ada[2:3, :] * y) * lng[...] + lnb[...]
    h2 = _ln_plain(x1) * (1.0 + ada[4:5, :]) + ada[3:4, :]
    hr_ref[:, :D] = h2

    logits = jnp.dot(h2, rw[...], precision=lax.Precision.HIGHEST, preferred_element_type=F32)
    sc0 = jax.nn.sigmoid(logits)
    sl0 = sc0 + rb[...]
    sc = [sc0] + [pltpu.roll(sc0, LANE - 32 * j, 1) for j in range(1, EXPERTS_PER_GROUP)]
    sl = [sl0] + [pltpu.roll(sl0, LANE - 32 * j, 1) for j in range(1, EXPERTS_PER_GROUP)]
    hi1, lo1 = jnp.maximum(sl[0], sl[1]), jnp.minimum(sl[0], sl[1])
    hi2, lo2 = jnp.maximum(sl[2], sl[3]), jnp.minimum(sl[2], sl[3])
    gscore = jnp.maximum(hi1, hi2) + jnp.maximum(jnp.minimum(hi1, hi2), jnp.maximum(lo1, lo2))
    lane = lax.broadcasted_iota(jnp.int32, (tm, LANE), 1)
    gscore = jnp.where(lane < N_EXPERT_GROUPS, gscore, -jnp.inf)
    gmax = jnp.max(gscore, axis=-1, keepdims=True)
    gidx = jnp.min(jnp.where(gscore == gmax, lane, LANE), axis=-1, keepdims=True)
    pick = lane == gidx
    vs = [jnp.sum(jnp.where(pick, a, 0.0), axis=-1, keepdims=True) for a in sl]
    ss = [jnp.sum(jnp.where(pick, a, 0.0), axis=-1, keepdims=True) for a in sc]
    chosen = []
    for j in range(EXPERTS_PER_GROUP):
        cnt = jnp.zeros((tm, 1), jnp.int32)
        for k in range(EXPERTS_PER_GROUP):
            if k == j:
                continue
            beats = (vs[k] > vs[j]) | ((vs[k] == vs[j]) & (k < j))
            cnt = cnt + beats.astype(jnp.int32)
        chosen.append(cnt < 2)
    wsel = [jnp.where(chosen[j], ss[j], 0.0) for j in range(EXPERTS_PER_GROUP)]
    wsum = wsel[0] + wsel[1] + wsel[2] + wsel[3]
    route = jnp.where(lane == EXPERTS_PER_GROUP, gidx.astype(F32), 0.0)
    for j in range(EXPERTS_PER_GROUP):
        route = jnp.where(lane == j, wsel[j] / wsum, route)
    hr_ref[:, D:] = route

    h2b = h2.astype(BF16)
    a1 = jnp.dot(h2b, ws1[...], preferred_element_type=F32)
    a3 = jnp.dot(h2b, ws3[...], preferred_element_type=F32)
    ysh = _bdot(a1 * jax.nn.sigmoid(a1) * a3, ws2[...])
    base_ref[...] = alpha * x1 + ada[5:6, :] * ysh


def _merge(x2d, ada_l, pc, sgs, sga, ys_tm, ya, wts, B, S, alpha):
    T, D = x2d.shape
    tm = TM_PROJ
    ns = S // tm
    row = lambda b, s: (b * ns + s, 0)
    names = ["w_ups", "w_upa", "w_o", "lng0", "lnb0", "rw", "rb", "ws1", "ws3", "ws2"]
    ws = [wts[n] for n in names]
    full = lambda a: pl.BlockSpec(a.shape, lambda b, s: (0,) * a.ndim)
    return pl.pallas_call(
        functools.partial(_merge_kernel, alpha=alpha),
        out_shape=(jax.ShapeDtypeStruct((T, D), F32),
                   jax.ShapeDtypeStruct((T, D + ROUTE_W), F32)),
        grid=(B, ns),
        in_specs=[pl.BlockSpec((tm, D), row),
                  pl.BlockSpec((None, 6, D), lambda b, s: (b, 0, 0)),
                  pl.BlockSpec((tm, D), row), pl.BlockSpec((tm, D), row), pl.BlockSpec((tm, D), row),
                  pl.BlockSpec((tm, D_SSM), lambda b, s: (s, b)),
                  pl.BlockSpec((tm, MLA_HEADS * V_HEAD), row)]
                 + [full(a) for a in ws],
        out_specs=(pl.BlockSpec((tm, D), row), pl.BlockSpec((tm, D + ROUTE_W), row)),
        compiler_params=_cparams(2),
        name="merge",
    )(x2d, ada_l, pc, sgs, sga, ys_tm, ya, *ws)


def _moe_kernel(bg_ref, tok_ref, hr_hbm, w1_ref, w3_ref, w2_ref, y_hbm, xbuf, ybuf, gsem, ssem,
                *, n_tok):
    i = pl.program_id(0)
    nb = pl.num_programs(0)
    R = xbuf.shape[1]
    D = ybuf.shape[2]
    slot = i % 2

    def valid(b):
        return tok_ref[b * R] >= 0

    def start_gather(b, s):
        def body(r, c):
            t = jnp.maximum(tok_ref[b * R + r], 0)
            pltpu.make_async_copy(hr_hbm.at[pl.ds(t, 1)], xbuf.at[s, pl.ds(r, 1)], gsem.at[s]).start()
            return c

        lax.fori_loop(0, R, body, 0, unroll=8)

    def wait_gather(s):
        pltpu.make_async_copy(hr_hbm.at[pl.ds(0, R)], xbuf.at[s], gsem.at[s]).wait()

    def start_scatter(b, s):
        def body(r, c):
            t = tok_ref[b * R + r]
            dst = jnp.where(t >= 0, t, n_tok + s * R + r)
            pltpu.make_async_copy(ybuf.at[s, pl.ds(r, 1)], y_hbm.at[pl.ds(dst, 1)], ssem.at[s]).start()
            return c

        lax.fori_loop(0, R, body, 0, unroll=8)

    def wait_scatter(s):
        pltpu.make_async_copy(ybuf.at[s], y_hbm.at[pl.ds(0, R)], ssem.at[s]).wait()

    @pl.when(i == 0)
    def _():
        ybuf[...] = jnp.zeros(ybuf.shape, F32)
        for s in range(2):
            spare = pltpu.make_async_copy(ybuf.at[s], y_hbm.at[pl.ds(n_tok + s * R, R)], ssem.at[s])
            spare.start()
            spare.wait()

    @pl.when((i == 0) & valid(0))
    def _():
        start_gather(0, 0)

    nxt = jnp.minimum(i + 1, nb - 1)

    @pl.when((i + 1 < nb) & valid(nxt))
    def _():
        start_gather(nxt, 1 - slot)

    @pl.when((i >= 2) & valid(jnp.maximum(i - 2, 0)))
    def _():
        wait_scatter(slot)

    @pl.when(valid(i))
    def _():
        wait_gather(slot)
        xb = xbuf[slot, :, :D].astype(BF16)
        acc = jnp.zeros((R, D), F32)
        for j in range(EXPERTS_PER_GROUP):
            a1 = jnp.dot(xb, w1_ref[j], preferred_element_type=F32)
            a3 = jnp.dot(xb, w3_ref[j], preferred_element_type=F32)
            hj = a1 * jax.nn.sigmoid(a1) * a3 * xbuf[slot, :, D + j:D + j + 1]
            acc = acc + _bdot(hj, w2_ref[j])
        ybuf[slot] = acc
        start_scatter(i, slot)

    @pl.when(i == nb - 1)
    def _():
        @pl.when(valid(jnp.maximum(i - 1, 0)) & (i >= 1))
        def _():
            wait_scatter(1 - slot)

        @pl.when(valid(i))
        def _():
            wait_scatter(slot)


def _moe(hr, block_g, row_tok, w1, w3, w2):
    T = hr.shape[0]
    D = w1.shape[1]
    R = MOE_ROWS
    nb = row_tok.shape[0] // R
    wspec_in = pl.BlockSpec((EXPERTS_PER_GROUP, D, D_EXPERT), lambda i, bg, tok: (bg[i], 0, 0))
    wspec_out = pl.BlockSpec((EXPERTS_PER_GROUP, D_EXPERT, D), lambda i, bg, tok: (bg[i], 0, 0))
    return pl.pallas_call(
        functools.partial(_moe_kernel, n_tok=T),
        out_shape=jax.ShapeDtypeStruct((T + 2 * R, D), F32),
        grid_spec=pltpu.PrefetchScalarGridSpec(
            num_scalar_prefetch=2,
            grid=(nb,),
            in_specs=[pl.BlockSpec(memory_space=pl.ANY), wspec_in, wspec_in, wspec_out],
            out_specs=pl.BlockSpec(memory_space=pl.ANY),
            scratch_shapes=[pltpu.VMEM((2, R, D + ROUTE_W), F32), pltpu.VMEM((2, R, D), F32),
                            pltpu.SemaphoreType.DMA((2,)), pltpu.SemaphoreType.DMA((2,))]),
        compiler_params=_cparams(1),
        name="moe",
    )(block_g, row_tok, hr, w1, w3, w2)


def _route_plan(gid, T):
    R = MOE_ROWS
    G = N_EXPERT_GROUPS
    onehot = (gid[:, None] == jnp.arange(G, dtype=jnp.int32)[None, :]).astype(jnp.int32)
    csum = jnp.cumsum(onehot, axis=0)
    counts = csum[-1]
    rank = jnp.sum(onehot * csum, axis=1) - 1
    padded = (counts + R - 1) // R * R
    pad_end = jnp.cumsum(padded)
    pad_start = pad_end - padded
    dest = pad_start[gid] + rank
    nb = T // R + G
    row_tok = jnp.full((nb * R,), -1, jnp.int32).at[dest].set(jnp.arange(T, dtype=jnp.int32))
    starts = jnp.arange(nb, dtype=jnp.int32) * R
    block_g = jnp.minimum(jnp.sum((starts[:, None] >= pad_end[None, :]).astype(jnp.int32), axis=1), G - 1)
    return block_g, row_tok


def _final_kernel(base_ref, y_ref, ada_ref, lng, lnb, o_ref):
    ada = ada_ref[...]
    o_ref[...] = _ln_plain(base_ref[...] + ada[5:6, :] * y_ref[...]) * lng[...] + lnb[...]


def _final(base, yr, ada_l, lng, lnb, B, S):
    T, D = base.shape
    tm = 512
    ns = S // tm
    row = lambda b, s: (b * ns + s, 0)
    return pl.pallas_call(
        _final_kernel,
        out_shape=jax.ShapeDtypeStruct((T, D), F32),
        grid=(B, ns),
        in_specs=[pl.BlockSpec((tm, D), row), pl.BlockSpec((tm, D), row),
                  pl.BlockSpec((None, 6, D), lambda b, s: (b, 0, 0)),
                  pl.BlockSpec((1, D), lambda b, s: (0, 0)), pl.BlockSpec((1, D), lambda b, s: (0, 0))],
        out_specs=pl.BlockSpec((tm, D), row),
        compiler_params=_cparams(2),
        name="final_ln",
    )(base, yr, ada_l, lng, lnb)


def _rope_tables(positions):
    inv_freq = ROPE_THETA ** (-jnp.arange(0, QK_ROPE, 2, dtype=F32) / QK_ROPE)
    ang = positions.astype(F32)[..., None] * inv_freq
    cos, sin = jnp.cos(ang), jnp.sin(ang)
    ones = jnp.ones(cos.shape[:-1] + (QK_NOPE,), F32)
    zpad = jnp.zeros(cos.shape[:-1] + (HEAD_PAD - QK_NOPE - QK_ROPE,), F32)
    cos128 = jnp.concatenate([ones, cos, cos, zpad], axis=-1)
    sin128 = jnp.concatenate([0.0 * ones, -sin, sin, zpad], axis=-1)
    T = positions.shape[0] * positions.shape[1]
    return cos128.reshape(T, HEAD_PAD), sin128.reshape(T, HEAD_PAD)


def _swap_halves(w):
    half = w.shape[-1] // 2
    return jnp.concatenate([w[..., half:], w[..., :half]], axis=-1)


def _layer_weights(l, p):
    D = p["w_in"].shape[1]
    w_in = p["w_in"][l]
    o = 0
    seg = {}
    for name, width in (("cv", 3 * D_CONV), ("ssm", D_SSM), ("cq", Q_LORA), ("ckv", KV_LORA),
                        ("kr", QK_ROPE), ("g", 3 * D)):
        seg[name] = w_in[:, o:o + width]
        o += width
    pad_r = HEAD_PAD - QK_NOPE - QK_ROPE
    zl = jnp.zeros((D, QK_NOPE), F32)
    zr = jnp.zeros((D, pad_r), F32)
    w = {
        "w_cv": seg["cv"].astype(BF16), "w_ssm": seg["ssm"].astype(BF16),
        "w_cq": seg["cq"].astype(BF16), "w_ckv": seg["ckv"].astype(BF16),
        "w_kra": jnp.concatenate([zl, seg["kr"], zr], axis=1).astype(BF16),
        "w_krb": jnp.concatenate([zl, _swap_halves(seg["kr"]), zr], axis=1).astype(BF16),
        "w_g": seg["g"].astype(BF16),
        "convw": p["conv_w"][l],
        "qn": p["q_norm"][l].reshape(1, Q_LORA), "kvn": p["kv_norm"][l].reshape(1, KV_LORA),
        "w_upc": p["w_up_conv"][l].astype(BF16),
    }
    scale = (QK_NOPE + QK_ROPE) ** -0.5 * math.log2(math.e)
    wq = p["w_uq"][l].reshape(Q_LORA, MLA_HEADS, QK_NOPE + QK_ROPE) * scale
    zq = jnp.zeros((Q_LORA, MLA_HEADS, pad_r), F32)
    zn = jnp.zeros((Q_LORA, MLA_HEADS, QK_NOPE), F32)
    w["wq_a"] = jnp.concatenate([wq, zq], axis=-1).reshape(Q_LORA, -1).astype(BF16)
    w["wq_b"] = jnp.concatenate([zn, _swap_halves(wq[..., QK_NOPE:]), zq],
                                axis=-1).reshape(Q_LORA, -1).astype(BF16)
    wk = p["w_uk"][l].reshape(KV_LORA, MLA_HEADS, QK_NOPE)
    zk = jnp.zeros((KV_LORA, MLA_HEADS, HEAD_PAD - QK_NOPE), F32)
    w["wk"] = jnp.concatenate([wk, zk], axis=-1).reshape(KV_LORA, -1).astype(BF16)
    w["wv"] = p["w_uv"][l].astype(BF16)

    ar, ai = p["ssm_a_re"][l], p["ssm_a_im"][l]
    dt = jnp.exp(p["ssm_log_dt"][l])[:, None]
    mag = jnp.exp(ar * dt)
    lb_re, lb_im = mag * jnp.cos(ai * dt), mag * jnp.sin(ai * dt)
    den = ar * ar + ai * ai
    nr, ni = lb_re - 1.0, lb_im
    f_re, f_im = (nr * ar + ni * ai) / den, (ni * ar - nr * ai) / den
    br, bi = p["ssm_b_re"][l], p["ssm_b_im"][l]
    bb_re = f_re[..., None] * br - f_im[..., None] * bi
    bb_im = f_re[..., None] * bi + f_im[..., None] * br
    gpc = LANE // SSM_GROUP_DIM
    nch = SSM_GROUPS // gpc
    eye = jnp.eye(gpc, dtype=F32)

    def bmat(bb):
        return jnp.einsum("cgnp,gh->cgphn", bb.reshape(nch, gpc, SSM_STATE, SSM_GROUP_DIM),
                          eye).reshape(nch, LANE, gpc * SSM_STATE).astype(BF16)

    def cmat(cc):
        return jnp.einsum("cgpn,gh->cgnhp", cc.reshape(nch, gpc, SSM_GROUP_DIM, SSM_STATE),
                          eye).reshape(nch, gpc * SSM_STATE, LANE).astype(BF16)

    w["ssm_bre"], w["ssm_bim"] = bmat(bb_re), bmat(bb_im)
    w["ssm_cre"], w["ssm_cim"] = cmat(p["ssm_c_re"][l]), cmat(-p["ssm_c_im"][l])
    w["ssm_lre"] = lb_re.reshape(1, -1)
    w["ssm_lim"] = lb_im.reshape(1, -1)
    w["ssm_d"] = p["ssm_d"][l].reshape(1, D_SSM)
    w["ssm_wglu"] = p["ssm_w_glu"][l].astype(BF16)

    w["w_ups"] = p["w_up_ssm"][l].astype(BF16)
    w["w_upa"] = p["w_up_attn"][l].astype(BF16)
    w["w_o"] = p["w_o"][l].astype(BF16)
    w["lng0"], w["lnb0"] = p["ln_g"][l, 0].reshape(1, D), p["ln_b"][l, 0].reshape(1, D)
    w["lng1"], w["lnb1"] = p["ln_g"][l, 1].reshape(1, D), p["ln_b"][l, 1].reshape(1, D)
    rwg = p["router_w"].reshape(D, N_EXPERT_GROUPS, EXPERTS_PER_GROUP)
    rbg = p["router_bias"].reshape(1, N_EXPERT_GROUPS, EXPERTS_PER_GROUP)

    def lanes(a):
        a = jnp.swapaxes(a, 1, 2)
        a = jnp.pad(a, ((0, 0), (0, 0), (0, LANE // EXPERTS_PER_GROUP - N_EXPERT_GROUPS)))
        return a.reshape(a.shape[0], LANE)

    w["rw"], w["rb"] = lanes(rwg), lanes(rbg)
    w["ws1"] = p["shared_w1"][l].astype(BF16)
    w["ws3"] = p["shared_w3"][l].astype(BF16)
    w["ws2"] = p["shared_w2"][l].astype(BF16)
    w["e1"] = p["exp_w1"][l].astype(BF16)
    w["e3"] = p["exp_w3"][l].astype(BF16)
    w["e2"] = p["exp_w2"][l].astype(BF16)
    return w


def kernel(x, c, positions, w_ada, b_ada, w_in, conv_w, ssm_a_re, ssm_a_im, ssm_b_re, ssm_b_im,
           ssm_c_re, ssm_c_im, ssm_d, ssm_log_dt, ssm_w_glu, q_norm, w_uq, kv_norm, w_uk, w_uv,
           w_up_conv, w_up_ssm, w_up_attn, w_o, ln_g, ln_b, router_w, router_bias,
           exp_w1, exp_w3, exp_w2, shared_w1, shared_w3, shared_w2):
    B, S, D = x.shape
    L = w_in.shape[0]
    T = B * S
    assert B == 8 and S % max(TM_PROJ, T_CHUNK, TQ, 512) == 0 and T % MOE_ROWS == 0
    p = dict(w_in=w_in, conv_w=conv_w, ssm_a_re=ssm_a_re, ssm_a_im=ssm_a_im, ssm_b_re=ssm_b_re,
             ssm_b_im=ssm_b_im, ssm_c_re=ssm_c_re, ssm_c_im=ssm_c_im, ssm_d=ssm_d,
             ssm_log_dt=ssm_log_dt, ssm_w_glu=ssm_w_glu, q_norm=q_norm, w_uq=w_uq, kv_norm=kv_norm,
             w_uk=w_uk, w_uv=w_uv, w_up_conv=w_up_conv, w_up_ssm=w_up_ssm, w_up_attn=w_up_attn,
             w_o=w_o, ln_g=ln_g, ln_b=ln_b, router_w=router_w, router_bias=router_bias,
             exp_w1=exp_w1, exp_w3=exp_w3, exp_w2=exp_w2, shared_w1=shared_w1,
             shared_w3=shared_w3, shared_w2=shared_w2)
    alpha = (2 * L) ** 0.25
    cos128, sin128 = _rope_tables(positions)
    ada = _ada(c, w_ada, b_ada).reshape(L, B, 6, D)
    x2d = x.reshape(T, D)
    for l in range(L):
        wts = _layer_weights(l, p)
        pc, sgs, sga, u_tm, q, k, v = _inproj(x2d, ada[l], cos128, sin128, wts, B, S)
        ys = _ssm(u_tm.reshape(S * B, D_SSM), wts, B, S).reshape(S, B * D_SSM)
        ya = _attention(q, k, v, B, S)
        base, hr = _merge(x2d, ada[l], pc, sgs, sga, ys, ya, wts, B, S, alpha)
        gid = hr[:, D + EXPERTS_PER_GROUP].astype(jnp.int32)
        block_g, row_tok = _route_plan(gid, T)
        yr = _moe(hr, block_g, row_tok, wts["e1"], wts["e3"], wts["e2"])
        x2d = _final(base, yr, ada[l], wts["lng1"], wts["lnb1"], B, S)
    return x2d.reshape(B, S, D)
```

```python
import functools
import math

import jax
import jax.numpy as jnp
from jax import lax
from jax.experimental import pallas as pl
from jax.experimental.pallas import tpu as pltpu

F32 = jnp.float32
BF16 = jnp.bfloat16

D_CONV = 512
D_SSM = 512
SSM_GROUP_DIM = 16
SSM_GROUPS = 32
SSM_STATE = 64
MLA_HEADS = 8
QK_NOPE = 64
QK_ROPE = 32
V_HEAD = 64
Q_LORA = 256
KV_LORA = 128
ROPE_THETA = 10000.0
N_EXPERTS = 32
N_EXPERT_GROUPS = 8
EXPERTS_PER_GROUP = 4
D_EXPERT = 256
LN_EPS = 1e-5
RMS_EPS = 1e-6

LANE = 128
HEAD_PAD = 128
VMEM_LIMIT = 56 * 1024 * 1024

TM_PROJ = 512
T_CHUNK = 64
TQ = 512
ATTN_HEADS = 4
MOE_ROWS = 256
PERM_ROWS = 1024
FINAL_ROWS = 512
ROUTE_W = 128


def _cparams(n_axes):
    return pltpu.CompilerParams(dimension_semantics=("arbitrary",) * n_axes,
                                vmem_limit_bytes=VMEM_LIMIT)


def _ln_plain(x):
    mu = jnp.mean(x, axis=-1, keepdims=True)
    xc = x - mu
    var = jnp.mean(xc * xc, axis=-1, keepdims=True)
    return xc * lax.rsqrt(var + LN_EPS)


def _rms(x, g):
    return x * lax.rsqrt(jnp.mean(x * x, axis=-1, keepdims=True) + RMS_EPS) * g


def _bdot(a, b):
    return jnp.dot(a.astype(BF16), b, preferred_element_type=F32)


def _ada_kernel(c_ref, w_ref, b_ref, o_ref):
    c = c_ref[...]
    cond = c * jax.nn.sigmoid(c)
    o_ref[...] = jnp.dot(cond, w_ref[...], precision=lax.Precision.HIGHEST,
                         preferred_element_type=F32) + b_ref[...]


def _ada(c, w_ada, b_ada):
    L, D, D6 = w_ada.shape
    B = c.shape[0]
    nj = D6 // D
    return pl.pallas_call(
        _ada_kernel,
        out_shape=jax.ShapeDtypeStruct((L, B, D6), F32),
        grid=(L, nj),
        in_specs=[pl.BlockSpec((B, D), lambda l, j: (0, 0)),
                  pl.BlockSpec((None, D, D), lambda l, j: (l, 0, j)),
                  pl.BlockSpec((None, 1, D), lambda l, j: (l, 0, j))],
        out_specs=pl.BlockSpec((None, B, D), lambda l, j: (l, 0, j)),
        compiler_params=_cparams(2),
        name="ada",
    )(c, w_ada, b_ada.reshape(L, 1, D6))


def _inproj_kernel(x_ref, ada_ref, cos_ref, sin_ref,
                   w_cv, w_ssm, w_cq, w_ckv, w_kra, w_krb, w_g,
                   convw, qn, wq_a, wq_b, kvn, wk, wv, w_upc,
                   pc_ref, sgs_ref, sga_ref, u_ref, q_ref, k_ref, v_ref,
                   carry_ref):
    si = pl.program_id(1)
    D = x_ref.shape[1]
    tm = x_ref.shape[0]

    @pl.when(si == 0)
    def _():
        carry_ref[...] = jnp.zeros_like(carry_ref)

    ada = ada_ref[...]
    h = _ln_plain(x_ref[...]) * (1.0 + ada[1:2, :]) + ada[0:1, :]
    hb = h.astype(BF16)

    pcv = jnp.dot(hb, w_cv[...], preferred_element_type=F32)
    u = pcv[:, D_CONV:2 * D_CONV] * pcv[:, :D_CONV]
    gb = pcv[:, 2 * D_CONV:]
    tail = carry_ref[...]
    row8 = lax.broadcasted_iota(jnp.int32, (8, D_CONV), 0)
    r1 = pltpu.roll(u, 1, 0)
    r2 = pltpu.roll(u, 2, 0)
    t1 = pltpu.roll(tail, 1, 0)
    t2 = pltpu.roll(tail, 2, 0)
    u1 = jnp.concatenate([jnp.where(row8 < 1, t1, r1[:8]), r1[8:]], axis=0)
    u2 = jnp.concatenate([jnp.where(row8 < 2, t2, r2[:8]), r2[8:]], axis=0)
    cw = convw[...]
    yc = gb * (cw[0:1, :] * u2 + cw[1:2, :] * u1 + cw[2:3, :] * u)
    carry_ref[...] = u[tm - 8:, :]

    gates = jnp.dot(hb, w_g[...], preferred_element_type=F32)
    pc_ref[...] = (jax.nn.sigmoid(gates[:, :D]) * _bdot(yc, w_upc[...])).astype(BF16)
    sgs_ref[...] = jax.nn.sigmoid(gates[:, D:2 * D]).astype(BF16)
    sga_ref[...] = jax.nn.sigmoid(gates[:, 2 * D:]).astype(BF16)

    u_ref[...] = jnp.dot(hb, w_ssm[...], preferred_element_type=F32).astype(BF16)

    cos = cos_ref[...]
    sin = sin_ref[...]
    cq = _rms(jnp.dot(hb, w_cq[...], preferred_element_type=F32), qn[...]).astype(BF16)
    qa = jnp.dot(cq, wq_a[...], preferred_element_type=F32)
    qb = jnp.dot(cq, wq_b[...], preferred_element_type=F32)
    ckv = _rms(jnp.dot(hb, w_ckv[...], preferred_element_type=F32), kvn[...]).astype(BF16)
    kn = jnp.dot(ckv, wk[...], preferred_element_type=F32)
    kr = (jnp.dot(hb, w_kra[...], preferred_element_type=F32) * cos
          + jnp.dot(hb, w_krb[...], preferred_element_type=F32) * sin)
    for hd in range(MLA_HEADS):
        sl = slice(hd * HEAD_PAD, (hd + 1) * HEAD_PAD)
        q_ref[:, sl] = (qa[:, sl] * cos + qb[:, sl] * sin).astype(BF16)
        k_ref[:, sl] = (kn[:, sl] + kr).astype(BF16)
    v_ref[...] = jnp.dot(ckv, wv[...], preferred_element_type=F32).astype(BF16)


def _inproj(x2d, ada_l, cos128, sin128, wts, B, S):
    T, D = x2d.shape
    tm = TM_PROJ
    ns = S // tm
    row = lambda b, s: (b * ns + s, 0)
    full = lambda a: pl.BlockSpec(a.shape, lambda b, s: (0,) * a.ndim, pipeline_mode=pl.Buffered(1))
    names = ["w_cv", "w_ssm", "w_cq", "w_ckv", "w_kra", "w_krb", "w_g", "convw", "qn",
             "wq_a", "wq_b", "kvn", "wk", "wv", "w_upc"]
    ws = [wts[n] for n in names]
    out_shape = (
        jax.ShapeDtypeStruct((T, D), BF16),
        jax.ShapeDtypeStruct((T, D), BF16),
        jax.ShapeDtypeStruct((T, D), BF16),
        jax.ShapeDtypeStruct((T, D_SSM), BF16),
        jax.ShapeDtypeStruct((T, MLA_HEADS * HEAD_PAD), BF16),
        jax.ShapeDtypeStruct((T, MLA_HEADS * HEAD_PAD), BF16),
        jax.ShapeDtypeStruct((T, MLA_HEADS * V_HEAD), BF16),
    )
    out_specs = (
        pl.BlockSpec((tm, D), row), pl.BlockSpec((tm, D), row), pl.BlockSpec((tm, D), row),
        pl.BlockSpec((tm, D_SSM), row),
        pl.BlockSpec((tm, MLA_HEADS * HEAD_PAD), row),
        pl.BlockSpec((tm, MLA_HEADS * HEAD_PAD), row),
        pl.BlockSpec((tm, MLA_HEADS * V_HEAD), row),
    )
    return pl.pallas_call(
        _inproj_kernel,
        out_shape=out_shape,
        grid=(B, ns),
        in_specs=[pl.BlockSpec((tm, D), row),
                  pl.BlockSpec((None, 6, D), lambda b, s: (b, 0, 0)),
                  pl.BlockSpec((tm, LANE), row), pl.BlockSpec((tm, LANE), row)]
                 + [full(a) for a in ws],
        out_specs=out_specs,
        scratch_shapes=[pltpu.VMEM((8, D_CONV), F32)],
        compiler_params=_cparams(2),
        name="inproj",
    )(x2d, ada_l, cos128, sin128, *ws)


def _ssm_kernel(u_ref, bre_ref, bim_ref, lre_ref, lim_ref, cre_ref, cim_ref, d_ref, wglu_ref,
                o_ref, sre, sim, st_re, st_im, tmaj):
    i = pl.program_id(0)
    batch, nsteps = u_ref.shape[0], u_ref.shape[1]
    rows = batch * nsteps
    nchunk = bre_ref.shape[0]
    cw = bre_ref.shape[2]

    @pl.when(i == 0)
    def _():
        st_re[...] = jnp.zeros_like(st_re)
        st_im[...] = jnp.zeros_like(st_im)

    for b in range(batch):
        tmaj[:, b, :] = u_ref[b].astype(F32)
    uf = tmaj[...].reshape(rows, D_SSM)
    ub = uf.astype(BF16)
    for c in range(nchunk):
        uc = ub[:, c * LANE:(c + 1) * LANE]
        sre[:, c * cw:(c + 1) * cw] = jnp.dot(uc, bre_ref[c], preferred_element_type=F32)
        sim[:, c * cw:(c + 1) * cw] = jnp.dot(uc, bim_ref[c], preferred_element_type=F32)

    def step(t, carry):
        pr, pi = carry
        r = pl.multiple_of(t * batch, batch)
        lre = lre_ref[...]
        lim = lim_ref[...]
        nr = lre * pr - lim * pi + sre[pl.ds(r, batch), :]
        ni = lre * pi + lim * pr + sim[pl.ds(r, batch), :]
        sre[pl.ds(r, batch), :] = nr
        sim[pl.ds(r, batch), :] = ni
        return nr, ni

    fr, fi = lax.fori_loop(0, nsteps, step, (st_re[...], st_im[...]))
    st_re[...] = fr
    st_im[...] = fi

    ys = []
    for c in range(nchunk):
        ys.append(_bdot(sre[:, c * cw:(c + 1) * cw], cre_ref[c])
                  + _bdot(sim[:, c * cw:(c + 1) * cw], cim_ref[c]))
    y = jnp.concatenate(ys, axis=1) + d_ref[...] * uf
    y = 0.5 * y * (1.0 + jnp.tanh(math.sqrt(2.0 / math.pi) * (y + 0.044715 * (y * y * y))))
    z = _bdot(y, wglu_ref[...])
    tmaj[...] = (z[:, :D_SSM] * jax.nn.sigmoid(z[:, D_SSM:])).reshape(nsteps, batch, D_SSM)
    for b in range(batch):
        o_ref[b] = tmaj[:, b, :].astype(BF16)


def _ssm(u, wts, B, S):
    rows = T_CHUNK * B
    n = S // T_CHUNK
    nstate = SSM_GROUPS * SSM_STATE
    names = ["ssm_bre", "ssm_bim", "ssm_lre", "ssm_lim", "ssm_cre", "ssm_cim", "ssm_d", "ssm_wglu"]
    ws = [wts[k] for k in names]
    full = lambda a: pl.BlockSpec(a.shape, lambda i: (0,) * a.ndim, pipeline_mode=pl.Buffered(1))
    return pl.pallas_call(
        _ssm_kernel,
        out_shape=jax.ShapeDtypeStruct((B, S, D_SSM), BF16),
        grid=(n,),
        in_specs=[pl.BlockSpec((B, T_CHUNK, D_SSM), lambda i: (0, i, 0))] + [full(a) for a in ws],
        out_specs=pl.BlockSpec((B, T_CHUNK, D_SSM), lambda i: (0, i, 0)),
        scratch_shapes=[pltpu.VMEM((rows, nstate), F32), pltpu.VMEM((rows, nstate), F32),
                        pltpu.VMEM((B, nstate), F32), pltpu.VMEM((B, nstate), F32),
                        pltpu.VMEM((T_CHUNK, B, D_SSM), F32)],
        compiler_params=_cparams(1),
        name="ssm",
    )(u, *ws)


def _attn_kernel(q_ref, k_ref, v_ref, o_ref, m_sc, l_sc, acc_sc):
    qi = pl.program_id(2)
    tq = q_ref.shape[0]
    nh = m_sc.shape[0]
    neg = -1e30
    nrep = tq // LANE
    m_sc[...] = jnp.full(m_sc.shape, neg, F32)
    l_sc[...] = jnp.zeros(l_sc.shape, F32)
    acc_sc[...] = jnp.zeros(acc_sc.shape, F32)

    def block(j, masked):
        off = pl.multiple_of(j * tq, tq)
        vb = v_ref[pl.ds(off, tq), :]
        for hh in range(nh):
            q = q_ref[:, hh * HEAD_PAD:(hh + 1) * HEAD_PAD]
            kb = k_ref[pl.ds(off, tq), hh * HEAD_PAD:(hh + 1) * HEAD_PAD]
            s = lax.dot_general(q, kb, (((1,), (1,)), ((), ())), preferred_element_type=F32)
            if masked:
                rowi = lax.broadcasted_iota(jnp.int32, (tq, tq), 0)
                coli = lax.broadcasted_iota(jnp.int32, (tq, tq), 1)
                s = jnp.where(coli <= rowi, s, neg)
            m_prev = m_sc[hh]
            m_new = jnp.maximum(m_prev, jnp.max(s, axis=-1, keepdims=True))
            p = jnp.exp2(s - jnp.concatenate([m_new] * nrep, axis=1))
            a = jnp.exp2(m_prev - m_new)
            l_sc[hh] = a * l_sc[hh] + jnp.sum(p, axis=-1, keepdims=True)
            acc_sc[hh] = (jnp.concatenate([a] * (acc_sc.shape[2] // LANE), axis=1) * acc_sc[hh]
                          + jnp.dot(p.astype(BF16), vb, preferred_element_type=F32))
            m_sc[hh] = m_new

    def full_block(j, c):
        block(j, False)
        return c

    lax.fori_loop(0, qi, full_block, 0)
    block(qi, True)
    lane = lax.broadcasted_iota(jnp.int32, (tq, nh * V_HEAD), 1)
    out = jnp.zeros((tq, nh * V_HEAD), F32)
    for hh in range(nh):
        inv = 1.0 / l_sc[hh]
        oh = acc_sc[hh] * jnp.concatenate([inv] * (acc_sc.shape[2] // LANE), axis=1)
        out = jnp.where((lane >= hh * V_HEAD) & (lane < (hh + 1) * V_HEAD), oh, out)
    o_ref[...] = out.astype(BF16)


def _attention(q, k, v, B, S):
    nq = S // TQ
    nh = ATTN_HEADS
    hg = MLA_HEADS // nh
    return pl.pallas_call(
        _attn_kernel,
        out_shape=jax.ShapeDtypeStruct((B * S, MLA_HEADS * V_HEAD), BF16),
        grid=(B, hg, nq),
        in_specs=[pl.BlockSpec((TQ, nh * HEAD_PAD), lambda b, h, i: (b * nq + i, h)),
                  pl.BlockSpec((S, nh * HEAD_PAD), lambda b, h, i: (b, h)),
                  pl.BlockSpec((S, nh * V_HEAD), lambda b, h, i: (b, h))],
        out_specs=pl.BlockSpec((TQ, nh * V_HEAD), lambda b, h, i: (b * nq + i, h)),
        scratch_shapes=[pltpu.VMEM((nh, TQ, LANE), F32), pltpu.VMEM((nh, TQ, LANE), F32),
                        pltpu.VMEM((nh, TQ, nh * V_HEAD), F32)],
        compiler_params=_cparams(3),
        name="attention",
    )(q, k, v)


def _merge_kernel(x_ref, ada_ref, pc_ref, sgs_ref, sga_ref, ys_ref, ya_ref,
                  w_ups, w_upa, w_o, lng, lnb, rw, rb, ws1, ws3, ws2,
                  base_ref, hr_ref, *, alpha):
    D = x_ref.shape[1]
    tm = x_ref.shape[0]
    ada = ada_ref[...]
    merged = (pc_ref[...].astype(F32)
              + sgs_ref[...].astype(F32) * jnp.dot(ys_ref[...], w_ups[...], preferred_element_type=F32)
              + sga_ref[...].astype(F32) * jnp.dot(ya_ref[...], w_upa[...], preferred_element_type=F32))
    y = _bdot(merged, w_o[...])
    x1 = _ln_plain(alpha * x_ref[...] + ada[2:3, :] * y) * lng[...] + lnb[...]
    h2 = _ln_plain(x1) * (1.0 + ada[4:5, :]) + ada[3:4, :]
    hr_ref[:, :D] = h2

    logits = jnp.dot(h2, rw[...], precision=lax.Precision.HIGHEST, preferred_element_type=F32)
    sc0 = jax.nn.sigmoid(logits)
    sl0 = sc0 + rb[...]
    sc = [sc0] + [pltpu.roll(sc0, LANE - 32 * j, 1) for j in range(1, EXPERTS_PER_GROUP)]
    sl = [sl0] + [pltpu.roll(sl0, LANE - 32 * j, 1) for j in range(1, EXPERTS_PER_GROUP)]
    hi1, lo1 = jnp.maximum(sl[0], sl[1]), jnp.minimum(sl[0], sl[1])
    hi2, lo2 = jnp.maximum(sl[2], sl[3]), jnp.minimum(sl[2], sl[3])
    gscore = jnp.maximum(hi1, hi2) + jnp.maximum(jnp.minimum(hi1, hi2), jnp.maximum(lo1, lo2))
    lane = lax.broadcasted_iota(jnp.int32, (tm, LANE), 1)
    gscore = jnp.where(lane < N_EXPERT_GROUPS, gscore, -jnp.inf)
    gmax = jnp.max(gscore, axis=-1, keepdims=True)
    gidx = jnp.min(jnp.where(gscore == gmax, lane, LANE), axis=-1, keepdims=True)
    pick = lane == gidx
    vs = [jnp.sum(jnp.where(pick, a, 0.0), axis=-1, keepdims=True) for a in sl]
    ss = [jnp.sum(jnp.where(pick, a, 0.0), axis=-1, keepdims=True) for a in sc]
    chosen = []
    for j in range(EXPERTS_PER_GROUP):
        cnt = jnp.zeros((tm, 1), jnp.int32)
        for k in range(EXPERTS_PER_GROUP):
            if k == j:
                continue
            beats = (vs[k] > vs[j]) | ((vs[k] == vs[j]) & (k < j))
            cnt = cnt + beats.astype(jnp.int32)
        chosen.append(cnt < 2)
    wsel = [jnp.where(chosen[j], ss[j], 0.0) for j in range(EXPERTS_PER_GROUP)]
    wsum = wsel[0] + wsel[1] + wsel[2] + wsel[3]
    route = jnp.where(lane == EXPERTS_PER_GROUP, gidx.astype(F32), 0.0)
    for j in range(EXPERTS_PER_GROUP):
        route = jnp.where(lane == j, wsel[j] / wsum, route)
    hr_ref[:, D:] = route

    h2b = h2.astype(BF16)
    a1 = jnp.dot(h2b, ws1[...], preferred_element_type=F32)
    a3 = jnp.dot(h2b, ws3[...], preferred_element_type=F32)
    ysh = _bdot(a1 * jax.nn.sigmoid(a1) * a3, ws2[...])
    base_ref[...] = alpha * x1 + ada[5:6, :] * ysh


def _merge(x2d, ada_l, pc, sgs, sga, ys, ya, wts, B, S, alpha):
    T, D = x2d.shape
    tm = TM_PROJ
    ns = S // tm
    row = lambda b, s: (b * ns + s, 0)
    names = ["w_ups", "w_upa", "w_o", "lng0", "lnb0", "rw", "rb", "ws1", "ws3", "ws2"]
    ws = [wts[n] for n in names]
    full = lambda a: pl.BlockSpec(a.shape, lambda b, s: (0,) * a.ndim, pipeline_mode=pl.Buffered(1))
    return pl.pallas_call(
        functools.partial(_merge_kernel, alpha=alpha),
        out_shape=(jax.ShapeDtypeStruct((T, D), F32),
                   jax.ShapeDtypeStruct((T, D + ROUTE_W), F32)),
        grid=(B, ns),
        in_specs=[pl.BlockSpec((tm, D), row),
                  pl.BlockSpec((None, 6, D), lambda b, s: (b, 0, 0)),
                  pl.BlockSpec((tm, D), row), pl.BlockSpec((tm, D), row), pl.BlockSpec((tm, D), row),
                  pl.BlockSpec((tm, D_SSM), row),
                  pl.BlockSpec((tm, MLA_HEADS * V_HEAD), row)]
                 + [full(a) for a in ws],
        out_specs=(pl.BlockSpec((tm, D), row), pl.BlockSpec((tm, D + ROUTE_W), row)),
        compiler_params=_cparams(2),
        name="merge",
    )(x2d, ada_l, pc, sgs, sga, ys, ya, *ws)


def _route_plan(gid, T):
    R = MOE_ROWS
    G = N_EXPERT_GROUPS
    onehot = (gid[:, None] == jnp.arange(G, dtype=jnp.int32)[None, :]).astype(jnp.int32)
    csum = jnp.cumsum(onehot, axis=0)
    counts = csum[-1]
    rank = jnp.sum(onehot * csum, axis=1) - 1
    padded = (counts + R - 1) // R * R
    pad_end = jnp.cumsum(padded)
    pad_start = pad_end - padded
    dest = jnp.sum(onehot * pad_start[None, :], axis=1) + rank
    nb = T // R + G
    starts = jnp.arange(nb, dtype=jnp.int32) * R
    block_g = jnp.minimum(jnp.sum((starts[:, None] >= pad_end[None, :]).astype(jnp.int32), axis=1), G - 1)
    gsel = (block_g[:, None] == jnp.arange(G, dtype=jnp.int32)[None, :]).astype(jnp.int32)
    group_end = jnp.sum(gsel * (pad_start + counts)[None, :], axis=1)
    nvalid = jnp.clip(group_end - starts, 0, R)
    return dest.astype(jnp.int32), block_g.astype(jnp.int32), nvalid.astype(jnp.int32)


def _permute_kernel(dest_ref, nvalid_ref, hr_ref, hs_hbm, zbuf, sem, zsem):
    i = pl.program_id(0)
    tp = hr_ref.shape[0]
    R = zbuf.shape[0]
    nb = nvalid_ref.shape[0]

    @pl.when(i == 0)
    def _():
        zbuf[...] = jnp.zeros(zbuf.shape, F32)

        def fill(b, n):
            partial = nvalid_ref[b] < R

            @pl.when(partial)
            def _():
                pltpu.make_async_copy(zbuf, hs_hbm.at[pl.ds(b * R, R)], zsem).start()

            return n + partial.astype(jnp.int32)

        nfill = lax.fori_loop(0, nb, fill, 0)

        def drain(k, c):
            pltpu.make_async_copy(zbuf, hs_hbm.at[pl.ds(0, R)], zsem).wait()
            return c

        lax.fori_loop(0, nfill, drain, 0)

    def body(r, c):
        d = dest_ref[i * tp + r]
        pltpu.make_async_copy(hr_ref.at[pl.ds(r, 1)], hs_hbm.at[pl.ds(d, 1)], sem).start()
        return c

    lax.fori_loop(0, tp, body, 0, unroll=8)
    pltpu.make_async_copy(hr_ref, hs_hbm.at[pl.ds(0, tp)], sem).wait()


def _permute(hr, dest, nvalid):
    T, W = hr.shape
    R = MOE_ROWS
    tp = PERM_ROWS
    nb = nvalid.shape[0]
    return pl.pallas_call(
        _permute_kernel,
        out_shape=jax.ShapeDtypeStruct((nb * R, W), F32),
        grid_spec=pltpu.PrefetchScalarGridSpec(
            num_scalar_prefetch=2,
            grid=(T // tp,),
            in_specs=[pl.BlockSpec((tp, W), lambda i, d, n: (i, 0))],
            out_specs=pl.BlockSpec(memory_space=pl.ANY),
            scratch_shapes=[pltpu.VMEM((R, W), F32), pltpu.SemaphoreType.DMA, pltpu.SemaphoreType.DMA]),
        compiler_params=_cparams(1),
        name="permute",
    )(dest, nvalid, hr)


def _moe_kernel(bg_ref, nvalid_ref, hs_ref, w1_ref, w3_ref, w2_ref, y_ref):
    i = pl.program_id(0)
    R, D = y_ref.shape

    @pl.when(nvalid_ref[i] > 0)
    def _():
        xb = hs_ref[:, :D].astype(BF16)
        acc = jnp.zeros((R, D), F32)
        for j in range(EXPERTS_PER_GROUP):
            a1 = jnp.dot(xb, w1_ref[j], preferred_element_type=F32)
            a3 = jnp.dot(xb, w3_ref[j], preferred_element_type=F32)
            hj = a1 * jax.nn.sigmoid(a1) * a3 * hs_ref[:, D + j:D + j + 1]
            acc = acc + _bdot(hj, w2_ref[j])
        y_ref[...] = acc

    @pl.when(nvalid_ref[i] == 0)
    def _():
        y_ref[...] = jnp.zeros((R, D), F32)


def _moe(hs, block_g, nvalid, w1, w3, w2):
    D = w1.shape[1]
    W = hs.shape[1]
    R = MOE_ROWS
    nb = nvalid.shape[0]
    wspec_in = pl.BlockSpec((EXPERTS_PER_GROUP, D, D_EXPERT), lambda i, bg, nv: (bg[i], 0, 0))
    wspec_out = pl.BlockSpec((EXPERTS_PER_GROUP, D_EXPERT, D), lambda i, bg, nv: (bg[i], 0, 0))
    return pl.pallas_call(
        _moe_kernel,
        out_shape=jax.ShapeDtypeStruct((nb * R, D), F32),
        grid_spec=pltpu.PrefetchScalarGridSpec(
            num_scalar_prefetch=2,
            grid=(nb,),
            in_specs=[pl.BlockSpec((R, W), lambda i, bg, nv: (i, 0)), wspec_in, wspec_in, wspec_out],
            out_specs=pl.BlockSpec((R, D), lambda i, bg, nv: (i, 0))),
        compiler_params=_cparams(1),
        name="moe",
    )(block_g, nvalid, hs, w1, w3, w2)


def _final_kernel(dest_ref, base_ref, ys_hbm, ada_ref, lng, lnb, o_ref, ybuf, sem):
    i = pl.program_id(0)
    n = pl.num_programs(0)
    tf = base_ref.shape[0]
    slot = i % 2

    def start_gather(step, s):
        def body(r, c):
            d = dest_ref[step * tf + r]
            pltpu.make_async_copy(ys_hbm.at[pl.ds(d, 1)], ybuf.at[s, pl.ds(r, 1)], sem.at[s]).start()
            return c

        lax.fori_loop(0, tf, body, 0, unroll=8)

    @pl.when(i == 0)
    def _():
        start_gather(0, 0)

    @pl.when(i + 1 < n)
    def _():
        start_gather(i + 1, 1 - slot)

    pltpu.make_async_copy(ys_hbm.at[pl.ds(0, tf)], ybuf.at[slot], sem.at[slot]).wait()
    ada = ada_ref[...]
    o_ref[...] = _ln_plain(base_ref[...] + ada[5:6, :] * ybuf[slot]) * lng[...] + lnb[...]


def _final(base, ys, dest, ada_l, lng, lnb, B, S):
    T, D = base.shape
    tf = FINAL_ROWS
    per_batch = S // tf
    return pl.pallas_call(
        _final_kernel,
        out_shape=jax.ShapeDtypeStruct((T, D), F32),
        grid_spec=pltpu.PrefetchScalarGridSpec(
            num_scalar_prefetch=1,
            grid=(T // tf,),
            in_specs=[pl.BlockSpec((tf, D), lambda i, d: (i, 0)),
                      pl.BlockSpec(memory_space=pl.ANY),
                      pl.BlockSpec((None, 6, D), lambda i, d: (i // per_batch, 0, 0)),
                      pl.BlockSpec((1, D), lambda i, d: (0, 0)), pl.BlockSpec((1, D), lambda i, d: (0, 0))],
            out_specs=pl.BlockSpec((tf, D), lambda i, d: (i, 0)),
            scratch_shapes=[pltpu.VMEM((2, tf, D), F32), pltpu.SemaphoreType.DMA((2,))]),
        compiler_params=_cparams(1),
        name="final_ln",
    )(dest, base, ys, ada_l, lng, lnb)


def _rope_tables(positions):
    inv_freq = ROPE_THETA ** (-jnp.arange(0, QK_ROPE, 2, dtype=F32) / QK_ROPE)
    ang = positions.astype(F32)[..., None] * inv_freq
    cos, sin = jnp.cos(ang), jnp.sin(ang)
    ones = jnp.ones(cos.shape[:-1] + (QK_NOPE,), F32)
    zpad = jnp.zeros(cos.shape[:-1] + (HEAD_PAD - QK_NOPE - QK_ROPE,), F32)
    cos128 = jnp.concatenate([ones, cos, cos, zpad], axis=-1)
    sin128 = jnp.concatenate([0.0 * ones, -sin, sin, zpad], axis=-1)
    T = positions.shape[0] * positions.shape[1]
    return cos128.reshape(T, HEAD_PAD), sin128.reshape(T, HEAD_PAD)


def _swap_halves(w):
    half = w.shape[-1] // 2
    return jnp.concatenate([w[..., half:], w[..., :half]], axis=-1)


def _layer_weights(l, p):
    D = p["w_in"].shape[1]
    w_in = p["w_in"][l]
    o = 0
    seg = {}
    for name, width in (("cv", 3 * D_CONV), ("ssm", D_SSM), ("cq", Q_LORA), ("ckv", KV_LORA),
                        ("kr", QK_ROPE), ("g", 3 * D)):
        seg[name] = w_in[:, o:o + width]
        o += width
    pad_r = HEAD_PAD - QK_NOPE - QK_ROPE
    zl = jnp.zeros((D, QK_NOPE), F32)
    zr = jnp.zeros((D, pad_r), F32)
    w = {
        "w_cv": seg["cv"].astype(BF16), "w_ssm": seg["ssm"].astype(BF16),
        "w_cq": seg["cq"].astype(BF16), "w_ckv": seg["ckv"].astype(BF16),
        "w_kra": jnp.concatenate([zl, seg["kr"], zr], axis=1).astype(BF16),
        "w_krb": jnp.concatenate([zl, _swap_halves(seg["kr"]), zr], axis=1).astype(BF16),
        "w_g": seg["g"].astype(BF16),
        "convw": p["conv_w"][l],
        "qn": p["q_norm"][l].reshape(1, Q_LORA), "kvn": p["kv_norm"][l].reshape(1, KV_LORA),
        "w_upc": p["w_up_conv"][l].astype(BF16),
    }
    scale = (QK_NOPE + QK_ROPE) ** -0.5 * math.log2(math.e)
    wq = p["w_uq"][l].reshape(Q_LORA, MLA_HEADS, QK_NOPE + QK_ROPE) * scale
    zq = jnp.zeros((Q_LORA, MLA_HEADS, pad_r), F32)
    zn = jnp.zeros((Q_LORA, MLA_HEADS, QK_NOPE), F32)
    w["wq_a"] = jnp.concatenate([wq, zq], axis=-1).reshape(Q_LORA, -1).astype(BF16)
    w["wq_b"] = jnp.concatenate([zn, _swap_halves(wq[..., QK_NOPE:]), zq],
                                axis=-1).reshape(Q_LORA, -1).astype(BF16)
    wk = p["w_uk"][l].reshape(KV_LORA, MLA_HEADS, QK_NOPE)
    zk = jnp.zeros((KV_LORA, MLA_HEADS, HEAD_PAD - QK_NOPE), F32)
    w["wk"] = jnp.concatenate([wk, zk], axis=-1).reshape(KV_LORA, -1).astype(BF16)
    w["wv"] = p["w_uv"][l].astype(BF16)

    ar, ai = p["ssm_a_re"][l], p["ssm_a_im"][l]
    dt = jnp.exp(p["ssm_log_dt"][l])[:, None]
    mag = jnp.exp(ar * dt)
    lb_re, lb_im = mag * jnp.cos(ai * dt), mag * jnp.sin(ai * dt)
    den = ar * ar + ai * ai
    nr, ni = lb_re - 1.0, lb_im
    f_re, f_im = (nr * ar + ni * ai) / den, (ni * ar - nr * ai) / den
    br, bi = p["ssm_b_re"][l], p["ssm_b_im"][l]
    bb_re = f_re[..., None] * br - f_im[..., None] * bi
    bb_im = f_re[..., None] * bi + f_im[..., None] * br
    gpc = LANE // SSM_GROUP_DIM
    nch = SSM_GROUPS // gpc
    eye = jnp.eye(gpc, dtype=F32)

    def bmat(bb):
        return jnp.einsum("cgnp,gh->cgphn", bb.reshape(nch, gpc, SSM_STATE, SSM_GROUP_DIM),
                          eye).reshape(nch, LANE, gpc * SSM_STATE).astype(BF16)

    def cmat(cc):
        return jnp.einsum("cgpn,gh->cgnhp", cc.reshape(nch, gpc, SSM_GROUP_DIM, SSM_STATE),
                          eye).reshape(nch, gpc * SSM_STATE, LANE).astype(BF16)

    w["ssm_bre"], w["ssm_bim"] = bmat(bb_re), bmat(bb_im)
    w["ssm_cre"], w["ssm_cim"] = cmat(p["ssm_c_re"][l]), cmat(-p["ssm_c_im"][l])
    nrow = 8
    w["ssm_lre"] = jnp.broadcast_to(lb_re.reshape(1, -1), (nrow, SSM_GROUPS * SSM_STATE))
    w["ssm_lim"] = jnp.broadcast_to(lb_im.reshape(1, -1), (nrow, SSM_GROUPS * SSM_STATE))
    w["ssm_d"] = p["ssm_d"][l].reshape(1, D_SSM)
    w["ssm_wglu"] = p["ssm_w_glu"][l].astype(BF16)

    w["w_ups"] = p["w_up_ssm"][l].astype(BF16)
    w["w_upa"] = p["w_up_attn"][l].astype(BF16)
    w["w_o"] = p["w_o"][l].astype(BF16)
    w["lng0"], w["lnb0"] = p["ln_g"][l, 0].reshape(1, D), p["ln_b"][l, 0].reshape(1, D)
    w["lng1"], w["lnb1"] = p["ln_g"][l, 1].reshape(1, D), p["ln_b"][l, 1].reshape(1, D)
    rwg = p["router_w"].reshape(D, N_EXPERT_GROUPS, EXPERTS_PER_GROUP)
    rbg = p["router_bias"].reshape(1, N_EXPERT_GROUPS, EXPERTS_PER_GROUP)

    def lanes(a):
        a = jnp.swapaxes(a, 1, 2)
        a = jnp.pad(a, ((0, 0), (0, 0), (0, LANE // EXPERTS_PER_GROUP - N_EXPERT_GROUPS)))
        return a.reshape(a.shape[0], LANE)

    w["rw"], w["rb"] = lanes(rwg), lanes(rbg)
    w["ws1"] = p["shared_w1"][l].astype(BF16)
    w["ws3"] = p["shared_w3"][l].astype(BF16)
    w["ws2"] = p["shared_w2"][l].astype(BF16)
    w["e1"] = p["exp_w1"][l].astype(BF16)
    w["e3"] = p["exp_w3"][l].astype(BF16)
    w["e2"] = p["exp_w2"][l].astype(BF16)
    return w


def kernel(x, c, positions, w_ada, b_ada, w_in, conv_w, ssm_a_re, ssm_a_im, ssm_b_re, ssm_b_im,
           ssm_c_re, ssm_c_im, ssm_d, ssm_log_dt, ssm_w_glu, q_norm, w_uq, kv_norm, w_uk, w_uv,
           w_up_conv, w_up_ssm, w_up_attn, w_o, ln_g, ln_b, router_w, router_bias,
           exp_w1, exp_w3, exp_w2, shared_w1, shared_w3, shared_w2):
    B, S, D = x.shape
    L = w_in.shape[0]
    T = B * S
    assert B == 8 and S % max(TM_PROJ, T_CHUNK, TQ, FINAL_ROWS) == 0 and T % max(MOE_ROWS, PERM_ROWS) == 0
    p = dict(w_in=w_in, conv_w=conv_w, ssm_a_re=ssm_a_re, ssm_a_im=ssm_a_im, ssm_b_re=ssm_b_re,
             ssm_b_im=ssm_b_im, ssm_c_re=ssm_c_re, ssm_c_im=ssm_c_im, ssm_d=ssm_d,
             ssm_log_dt=ssm_log_dt, ssm_w_glu=ssm_w_glu, q_norm=q_norm, w_uq=w_uq, kv_norm=kv_norm,
             w_uk=w_uk, w_uv=w_uv, w_up_conv=w_up_conv, w_up_ssm=w_up_ssm, w_up_attn=w_up_attn,
             w_o=w_o, ln_g=ln_g, ln_b=ln_b, router_w=router_w, router_bias=router_bias,
             exp_w1=exp_w1, exp_w3=exp_w3, exp_w2=exp_w2, shared_w1=shared_w1,
             shared_w3=shared_w3, shared_w2=shared_w2)
    alpha = (2 * L) ** 0.25
    cos128, sin128 = _rope_tables(positions)
    ada = _ada(c, w_ada, b_ada).reshape(L, B, 6, D)
    x2d = x.reshape(T, D)
    for l in range(L):
        wts = _layer_weights(l, p)
        pc, sgs, sga, u, q, k, v = _inproj(x2d, ada[l], cos128, sin128, wts, B, S)
        ys = _ssm(u.reshape(B, S, D_SSM), wts, B, S).reshape(T, D_SSM)
        ya = _attention(q, k, v, B, S)
        base, hr = _merge(x2d, ada[l], pc, sgs, sga, ys, ya, wts, B, S, alpha)
        gid = hr[:, D + EXPERTS_PER_GROUP].astype(jnp.int32)
        dest, block_g, nvalid = _route_plan(gid, T)
        hs = _permute(hr, dest, nvalid)
        ysort = _moe(hs, block_g, nvalid, wts["e1"], wts["e3"], wts["e2"])
        x2d = _final(base, ysort, dest, ada[l], wts["lng1"], wts["lnb1"], B, S)
    return x2d.reshape(B, S, D)
```

```python
import functools
import math

import jax
import jax.numpy as jnp
from jax import lax
from jax.experimental import pallas as pl
from jax.experimental.pallas import tpu as pltpu

F32 = jnp.float32
BF16 = jnp.bfloat16

D_CONV = 512
D_SSM = 512
SSM_GROUP_DIM = 16
SSM_GROUPS = 32
SSM_STATE = 64
MLA_HEADS = 8
QK_NOPE = 64
QK_ROPE = 32
V_HEAD = 64
Q_LORA = 256
KV_LORA = 128
ROPE_THETA = 10000.0
N_EXPERTS = 32
N_EXPERT_GROUPS = 8
EXPERTS_PER_GROUP = 4
D_EXPERT = 256
LN_EPS = 1e-5
RMS_EPS = 1e-6

LANE = 128
HEAD_PAD = 128
VMEM_LIMIT = 56 * 1024 * 1024

TM_PROJ = 512
T_CHUNK = 64
TQ = 512
ATTN_HEADS = 4
MOE_ROWS = 256
PERM_ROWS = 1024
FINAL_ROWS = 512
ROUTE_W = 128


def _cparams(n_axes):
    return pltpu.CompilerParams(dimension_semantics=("arbitrary",) * n_axes,
                                vmem_limit_bytes=VMEM_LIMIT)


def _ln_plain(x):
    mu = jnp.mean(x, axis=-1, keepdims=True)
    xc = x - mu
    var = jnp.mean(xc * xc, axis=-1, keepdims=True)
    return xc * lax.rsqrt(var + LN_EPS)


def _rms(x, g):
    return x * lax.rsqrt(jnp.mean(x * x, axis=-1, keepdims=True) + RMS_EPS) * g


def _bdot(a, b):
    return jnp.dot(a.astype(BF16), b, preferred_element_type=F32)


def _ada_kernel(c_ref, w_ref, b_ref, o_ref):
    c = c_ref[...]
    cond = c * jax.nn.sigmoid(c)
    o_ref[...] = jnp.dot(cond, w_ref[...], precision=lax.Precision.HIGHEST,
                         preferred_element_type=F32) + b_ref[...]


def _ada(c, w_ada, b_ada):
    L, D, D6 = w_ada.shape
    B = c.shape[0]
    nj = D6 // D
    return pl.pallas_call(
        _ada_kernel,
        out_shape=jax.ShapeDtypeStruct((L, B, D6), F32),
        grid=(L, nj),
        in_specs=[pl.BlockSpec((B, D), lambda l, j: (0, 0)),
                  pl.BlockSpec((None, D, D), lambda l, j: (l, 0, j)),
                  pl.BlockSpec((None, 1, D), lambda l, j: (l, 0, j))],
        out_specs=pl.BlockSpec((None, B, D), lambda l, j: (l, 0, j)),
        compiler_params=_cparams(2),
        name="ada",
    )(c, w_ada, b_ada.reshape(L, 1, D6))


def _inproj_kernel(x_ref, ada_ref, cos_ref, sin_ref,
                   w_cv, w_ssm, w_cq, w_ckv, w_kra, w_krb, w_g,
                   convw, qn, wq_a, wq_b, kvn, wk, wv, w_upc,
                   pc_ref, sgs_ref, sga_ref, u_ref, q_ref, k_ref, v_ref,
                   carry_ref):
    si = pl.program_id(1)
    D = x_ref.shape[1]
    tm = x_ref.shape[0]

    @pl.when(si == 0)
    def _():
        carry_ref[...] = jnp.zeros_like(carry_ref)

    ada = ada_ref[...]
    h = _ln_plain(x_ref[...]) * (1.0 + ada[1:2, :]) + ada[0:1, :]
    hb = h.astype(BF16)

    pcv = jnp.dot(hb, w_cv[...], preferred_element_type=F32)
    u = pcv[:, D_CONV:2 * D_CONV] * pcv[:, :D_CONV]
    gb = pcv[:, 2 * D_CONV:]
    tail = carry_ref[...]
    row8 = lax.broadcasted_iota(jnp.int32, (8, D_CONV), 0)
    r1 = pltpu.roll(u, 1, 0)
    r2 = pltpu.roll(u, 2, 0)
    t1 = pltpu.roll(tail, 1, 0)
    t2 = pltpu.roll(tail, 2, 0)
    u1 = jnp.concatenate([jnp.where(row8 < 1, t1, r1[:8]), r1[8:]], axis=0)
    u2 = jnp.concatenate([jnp.where(row8 < 2, t2, r2[:8]), r2[8:]], axis=0)
    cw = convw[...]
    yc = gb * (cw[0:1, :] * u2 + cw[1:2, :] * u1 + cw[2:3, :] * u)
    carry_ref[...] = u[tm - 8:, :]

    gates = jnp.dot(hb, w_g[...], preferred_element_type=F32)
    pc_ref[...] = (jax.nn.sigmoid(gates[:, :D]) * _bdot(yc, w_upc[...])).astype(BF16)
    sgs_ref[...] = jax.nn.sigmoid(gates[:, D:2 * D]).astype(BF16)
    sga_ref[...] = jax.nn.sigmoid(gates[:, 2 * D:]).astype(BF16)

    u_ref[...] = jnp.dot(hb, w_ssm[...], preferred_element_type=F32).astype(BF16)

    cos = cos_ref[...]
    sin = sin_ref[...]
    cq = _rms(jnp.dot(hb, w_cq[...], preferred_element_type=F32), qn[...]).astype(BF16)
    qa = jnp.dot(cq, wq_a[...], preferred_element_type=F32)
    qb = jnp.dot(cq, wq_b[...], preferred_element_type=F32)
    ckv = _rms(jnp.dot(hb, w_ckv[...], preferred_element_type=F32), kvn[...]).astype(BF16)
    kn = jnp.dot(ckv, wk[...], preferred_element_type=F32)
    kr = (jnp.dot(hb, w_kra[...], preferred_element_type=F32) * cos
          + jnp.dot(hb, w_krb[...], preferred_element_type=F32) * sin)
    for hd in range(MLA_HEADS):
        sl = slice(hd * HEAD_PAD, (hd + 1) * HEAD_PAD)
        q_ref[:, sl] = (qa[:, sl] * cos + qb[:, sl] * sin).astype(BF16)
        k_ref[:, sl] = (kn[:, sl] + kr).astype(BF16)
    v_ref[...] = jnp.dot(ckv, wv[...], preferred_element_type=F32).astype(BF16)


def _inproj(x2d, ada_l, cos128, sin128, wts, B, S):
    T, D = x2d.shape
    tm = TM_PROJ
    ns = S // tm
    row = lambda b, s: (b * ns + s, 0)
    full = lambda a: pl.BlockSpec(a.shape, lambda b, s: (0,) * a.ndim, pipeline_mode=pl.Buffered(1))
    names = ["w_cv", "w_ssm", "w_cq", "w_ckv", "w_kra", "w_krb", "w_g", "convw", "qn",
             "wq_a", "wq_b", "kvn", "wk", "wv", "w_upc"]
    ws = [wts[n] for n in names]
    out_shape = (
        jax.ShapeDtypeStruct((T, D), BF16),
        jax.ShapeDtypeStruct((T, D), BF16),
        jax.ShapeDtypeStruct((T, D), BF16),
        jax.ShapeDtypeStruct((T, D_SSM), BF16),
        jax.ShapeDtypeStruct((T, MLA_HEADS * HEAD_PAD), BF16),
        jax.ShapeDtypeStruct((T, MLA_HEADS * HEAD_PAD), BF16),
        jax.ShapeDtypeStruct((T, MLA_HEADS * V_HEAD), BF16),
    )
    out_specs = (
        pl.BlockSpec((tm, D), row), pl.BlockSpec((tm, D), row), pl.BlockSpec((tm, D), row),
        pl.BlockSpec((tm, D_SSM), row),
        pl.BlockSpec((tm, MLA_HEADS * HEAD_PAD), row),
        pl.BlockSpec((tm, MLA_HEADS * HEAD_PAD), row),
        pl.BlockSpec((tm, MLA_HEADS * V_HEAD), row),
    )
    return pl.pallas_call(
        _inproj_kernel,
        out_shape=out_shape,
        grid=(B, ns),
        in_specs=[pl.BlockSpec((tm, D), row),
                  pl.BlockSpec((None, 6, D), lambda b, s: (b, 0, 0)),
                  pl.BlockSpec((tm, LANE), row), pl.BlockSpec((tm, LANE), row)]
                 + [full(a) for a in ws],
        out_specs=out_specs,
        scratch_shapes=[pltpu.VMEM((8, D_CONV), F32)],
        compiler_params=_cparams(2),
        name="inproj",
    )(x2d, ada_l, cos128, sin128, *ws)


def _ssm_kernel(u_ref, bre_ref, bim_ref, lre_ref, lim_ref, cre_ref, cim_ref, d_ref, wglu_ref,
                o_ref, sre, sim, st_re, st_im, tmaj):
    i = pl.program_id(0)
    batch, nsteps = u_ref.shape[0], u_ref.shape[1]
    rows = batch * nsteps
    nchunk = bre_ref.shape[0]
    cw = bre_ref.shape[2]

    @pl.when(i == 0)
    def _():
        st_re[...] = jnp.zeros_like(st_re)
        st_im[...] = jnp.zeros_like(st_im)

    for b in range(batch):
        tmaj[:, b, :] = u_ref[b].astype(F32)
    uf = tmaj[...].reshape(rows, D_SSM)
    ub = uf.astype(BF16)
    for c in range(nchunk):
        uc = ub[:, c * LANE:(c + 1) * LANE]
        sre[:, c * cw:(c + 1) * cw] = jnp.dot(uc, bre_ref[c], preferred_element_type=F32)
        sim[:, c * cw:(c + 1) * cw] = jnp.dot(uc, bim_ref[c], preferred_element_type=F32)

    def step(t, carry):
        pr, pi = carry
        r = pl.multiple_of(t * batch, batch)
        lre = lre_ref[...]
        lim = lim_ref[...]
        nr = lre * pr - lim * pi + sre[pl.ds(r, batch), :]
        ni = lre * pi + lim * pr + sim[pl.ds(r, batch), :]
        sre[pl.ds(r, batch), :] = nr
        sim[pl.ds(r, batch), :] = ni
        return nr, ni

    fr, fi = lax.fori_loop(0, nsteps, step, (st_re[...], st_im[...]))
    st_re[...] = fr
    st_im[...] = fi

    ys = []
    for c in range(nchunk):
        ys.append(_bdot(sre[:, c * cw:(c + 1) * cw], cre_ref[c])
                  + _bdot(sim[:, c * cw:(c + 1) * cw], cim_ref[c]))
    y = jnp.concatenate(ys, axis=1) + d_ref[...] * uf
    y = 0.5 * y * (1.0 + jnp.tanh(math.sqrt(2.0 / math.pi) * (y + 0.044715 * (y * y * y))))
    z = _bdot(y, wglu_ref[...])
    tmaj[...] = (z[:, :D_SSM] * jax.nn.sigmoid(z[:, D_SSM:])).reshape(nsteps, batch, D_SSM)
    for b in range(batch):
        o_ref[b] = tmaj[:, b, :].astype(BF16)


def _ssm(u, wts, B, S):
    rows = T_CHUNK * B
    n = S // T_CHUNK
    nstate = SSM_GROUPS * SSM_STATE
    names = ["ssm_bre", "ssm_bim", "ssm_lre", "ssm_lim", "ssm_cre", "ssm_cim", "ssm_d", "ssm_wglu"]
    ws = [wts[k] for k in names]
    full = lambda a: pl.BlockSpec(a.shape, lambda i: (0,) * a.ndim, pipeline_mode=pl.Buffered(1))
    return pl.pallas_call(
        _ssm_kernel,
        out_shape=jax.ShapeDtypeStruct((B, S, D_SSM), BF16),
        grid=(n,),
        in_specs=[pl.BlockSpec((B, T_CHUNK, D_SSM), lambda i: (0, i, 0))] + [full(a) for a in ws],
        out_specs=pl.BlockSpec((B, T_CHUNK, D_SSM), lambda i: (0, i, 0)),
        scratch_shapes=[pltpu.VMEM((rows, nstate), F32), pltpu.VMEM((rows, nstate), F32),
                        pltpu.VMEM((B, nstate), F32), pltpu.VMEM((B, nstate), F32),
                        pltpu.VMEM((T_CHUNK, B, D_SSM), F32)],
        compiler_params=_cparams(1),
        name="ssm",
    )(u, *ws)


def _attn_kernel(q_ref, k_ref, v_ref, o_ref, m_sc, l_sc, acc_sc):
    qi = pl.program_id(2)
    tq = q_ref.shape[0]
    nh = m_sc.shape[0]
    neg = -1e30
    m_sc[...] = jnp.full(m_sc.shape, neg, F32)
    l_sc[...] = jnp.zeros(l_sc.shape, F32)
    acc_sc[...] = jnp.zeros(acc_sc.shape, F32)

    def block(off, r0, nrows, nkeys, masked):
        for hh in range(nh):
            pair = slice((hh // 2) * 2 * V_HEAD, (hh // 2 + 1) * 2 * V_HEAD)
            vb = v_ref[pl.ds(off, nkeys), pair]
            q = q_ref[r0:r0 + nrows, hh * HEAD_PAD:(hh + 1) * HEAD_PAD]
            kb = k_ref[pl.ds(off, nkeys), hh * HEAD_PAD:(hh + 1) * HEAD_PAD]
            s = lax.dot_general(q, kb, (((1,), (1,)), ((), ())), preferred_element_type=F32)
            if masked:
                rowi = lax.broadcasted_iota(jnp.int32, (nrows, nkeys), 0)
                coli = lax.broadcasted_iota(jnp.int32, (nrows, nkeys), 1)
                s = jnp.where(coli <= rowi + r0, s, neg)
            m_prev = m_sc[hh, r0:r0 + nrows]
            m_new = jnp.maximum(m_prev, jnp.max(s, axis=-1, keepdims=True))
            p = jnp.exp2(s - jnp.concatenate([m_new] * (nkeys // LANE), axis=1))
            a = jnp.exp2(m_prev - m_new)
            l_sc[hh, r0:r0 + nrows] = a * l_sc[hh, r0:r0 + nrows] + jnp.sum(p, axis=-1, keepdims=True)
            acc_sc[hh, r0:r0 + nrows] = (a * acc_sc[hh, r0:r0 + nrows]
                                         + jnp.dot(p.astype(BF16), vb, preferred_element_type=F32))
            m_sc[hh, r0:r0 + nrows] = m_new

    def full_block(j, c):
        block(pl.multiple_of(j * tq, tq), 0, tq, tq, False)
        return c

    lax.fori_loop(0, qi, full_block, 0)
    block(pl.multiple_of(qi * tq, tq), 0, tq, tq, True)
    lane = lax.broadcasted_iota(jnp.int32, (tq, 2 * V_HEAD), 1)
    outs = []
    for pr in range(nh // 2):
        lo = acc_sc[2 * pr] * (1.0 / l_sc[2 * pr])
        hi = acc_sc[2 * pr + 1] * (1.0 / l_sc[2 * pr + 1])
        outs.append(jnp.where(lane < V_HEAD, lo, hi))
    o_ref[...] = jnp.concatenate(outs, axis=1).astype(BF16)


def _attention(q, k, v, B, S):
    nq = S // TQ
    nh = ATTN_HEADS
    hg = MLA_HEADS // nh
    return pl.pallas_call(
        _attn_kernel,
        out_shape=jax.ShapeDtypeStruct((B * S, MLA_HEADS * V_HEAD), BF16),
        grid=(B, hg, nq),
        in_specs=[pl.BlockSpec((TQ, nh * HEAD_PAD), lambda b, h, i: (b * nq + i, h)),
                  pl.BlockSpec((S, nh * HEAD_PAD), lambda b, h, i: (b, h)),
                  pl.BlockSpec((S, nh * V_HEAD), lambda b, h, i: (b, h))],
        out_specs=pl.BlockSpec((TQ, nh * V_HEAD), lambda b, h, i: (b * nq + i, h)),
        scratch_shapes=[pltpu.VMEM((nh, TQ, LANE), F32), pltpu.VMEM((nh, TQ, LANE), F32),
                        pltpu.VMEM((nh, TQ, 2 * V_HEAD), F32)],
        compiler_params=_cparams(3),
        name="attention",
    )(q, k, v)


def _merge_kernel(x_ref, ada_ref, pc_ref, sgs_ref, sga_ref, ys_ref, ya_ref,
                  w_ups, w_upa, w_o, lng, lnb, rw, rb, ws1, ws3, ws2,
                  base_ref, hr_ref, *, alpha):
    D = x_ref.shape[1]
    tm = x_ref.shape[0]
    ada = ada_ref[...]
    merged = (pc_ref[...].astype(F32)
              + sgs_ref[...].astype(F32) * jnp.dot(ys_ref[...], w_ups[...], preferred_element_type=F32)
              + sga_ref[...].astype(F32) * jnp.dot(ya_ref[...], w_upa[...], preferred_element_type=F32))
    y = _bdot(merged, w_o[...])
    x1 = _ln_plain(alpha * x_ref[...] + ada[2:3, :] * y) * lng[...] + lnb[...]
    h2 = _ln_plain(x1) * (1.0 + ada[4:5, :]) + ada[3:4, :]
    hr_ref[:, :D] = h2

    G = N_EXPERT_GROUPS
    logits = lax.dot_general(rw[...], h2, (((1,), (1,)), ((), ())),
                             precision=lax.Precision.HIGHEST, preferred_element_type=F32)
    sc0 = jax.nn.sigmoid(logits)
    sl0 = sc0 + rb[...]
    sc = [sc0[j * G:(j + 1) * G] for j in range(EXPERTS_PER_GROUP)]
    sl = [sl0[j * G:(j + 1) * G] for j in range(EXPERTS_PER_GROUP)]
    hi1, lo1 = jnp.maximum(sl[0], sl[1]), jnp.minimum(sl[0], sl[1])
    hi2, lo2 = jnp.maximum(sl[2], sl[3]), jnp.minimum(sl[2], sl[3])
    gscore = jnp.maximum(hi1, hi2) + jnp.maximum(jnp.minimum(hi1, hi2), jnp.maximum(lo1, lo2))
    sub = lax.broadcasted_iota(jnp.int32, (G, tm), 0)
    gmax = jnp.max(gscore, axis=0, keepdims=True)
    gidx = jnp.min(jnp.where(gscore == gmax, sub, G), axis=0, keepdims=True)
    pick = sub == gidx
    vs = [jnp.sum(jnp.where(pick, a, 0.0), axis=0, keepdims=True) for a in sl]
    ss = [jnp.sum(jnp.where(pick, a, 0.0), axis=0, keepdims=True) for a in sc]
    chosen = []
    for j in range(EXPERTS_PER_GROUP):
        cnt = jnp.zeros((1, tm), jnp.int32)
        for k in range(EXPERTS_PER_GROUP):
            if k == j:
                continue
            beats = (vs[k] > vs[j]) | ((vs[k] == vs[j]) & (k < j))
            cnt = cnt + beats.astype(jnp.int32)
        chosen.append(cnt < 2)
    wsel = [jnp.where(chosen[j], ss[j], 0.0) for j in range(EXPERTS_PER_GROUP)]
    wsum = wsel[0] + wsel[1] + wsel[2] + wsel[3]
    rows = [wsel[j] / wsum for j in range(EXPERTS_PER_GROUP)] + [gidx.astype(F32)]
    rows += [jnp.zeros((1, tm), F32)] * (8 - len(rows))
    route_t = jnp.concatenate(rows, axis=0)
    eye = (lax.broadcasted_iota(jnp.int32, (8, ROUTE_W), 0)
           == lax.broadcasted_iota(jnp.int32, (8, ROUTE_W), 1)).astype(F32)
    hr_ref[:, D:] = lax.dot_general(route_t, eye, (((0,), (0,)), ((), ())),
                                    precision=lax.Precision.HIGHEST, preferred_element_type=F32)

    h2b = h2.astype(BF16)
    a1 = jnp.dot(h2b, ws1[...], preferred_element_type=F32)
    a3 = jnp.dot(h2b, ws3[...], preferred_element_type=F32)
    ysh = _bdot(a1 * jax.nn.sigmoid(a1) * a3, ws2[...])
    base_ref[...] = alpha * x1 + ada[5:6, :] * ysh


def _merge(x2d, ada_l, pc, sgs, sga, ys, ya, wts, B, S, alpha):
    T, D = x2d.shape
    tm = TM_PROJ
    ns = S // tm
    row = lambda b, s: (b * ns + s, 0)
    names = ["w_ups", "w_upa", "w_o", "lng0", "lnb0", "rw", "rb", "ws1", "ws3", "ws2"]
    ws = [wts[n] for n in names]
    full = lambda a: pl.BlockSpec(a.shape, lambda b, s: (0,) * a.ndim, pipeline_mode=pl.Buffered(1))
    return pl.pallas_call(
        functools.partial(_merge_kernel, alpha=alpha),
        out_shape=(jax.ShapeDtypeStruct((T, D), F32),
                   jax.ShapeDtypeStruct((T, D + ROUTE_W), F32)),
        grid=(B, ns),
        in_specs=[pl.BlockSpec((tm, D), row),
                  pl.BlockSpec((None, 6, D), lambda b, s: (b, 0, 0)),
                  pl.BlockSpec((tm, D), row), pl.BlockSpec((tm, D), row), pl.BlockSpec((tm, D), row),
                  pl.BlockSpec((tm, D_SSM), row),
                  pl.BlockSpec((tm, MLA_HEADS * V_HEAD), row)]
                 + [full(a) for a in ws],
        out_specs=(pl.BlockSpec((tm, D), row), pl.BlockSpec((tm, D + ROUTE_W), row)),
        compiler_params=_cparams(2),
        name="merge",
    )(x2d, ada_l, pc, sgs, sga, ys, ya, *ws)


def _route_plan(gid, T):
    R = MOE_ROWS
    G = N_EXPERT_GROUPS
    onehot = (gid[:, None] == jnp.arange(G, dtype=jnp.int32)[None, :]).astype(jnp.int32)
    csum = jnp.cumsum(onehot, axis=0)
    counts = csum[-1]
    rank = jnp.sum(onehot * csum, axis=1) - 1
    padded = (counts + R - 1) // R * R
    pad_end = jnp.cumsum(padded)
    pad_start = pad_end - padded
    dest = jnp.sum(onehot * pad_start[None, :], axis=1) + rank
    nb = T // R + G
    starts = jnp.arange(nb, dtype=jnp.int32) * R
    block_g = jnp.minimum(jnp.sum((starts[:, None] >= pad_end[None, :]).astype(jnp.int32), axis=1), G - 1)
    gsel = (block_g[:, None] == jnp.arange(G, dtype=jnp.int32)[None, :]).astype(jnp.int32)
    group_end = jnp.sum(gsel * (pad_start + counts)[None, :], axis=1)
    nvalid = jnp.clip(group_end - starts, 0, R)
    return dest.astype(jnp.int32), block_g.astype(jnp.int32), nvalid.astype(jnp.int32)


def _permute_kernel(dest_ref, nvalid_ref, hr_ref, hs_hbm, zbuf, sem, zsem):
    i = pl.program_id(0)
    tp = hr_ref.shape[0]
    R = zbuf.shape[0]
    nb = nvalid_ref.shape[0]

    @pl.when(i == 0)
    def _():
        zbuf[...] = jnp.zeros(zbuf.shape, F32)

        def fill(b, n):
            partial = nvalid_ref[b] < R

            @pl.when(partial)
            def _():
                pltpu.make_async_copy(zbuf, hs_hbm.at[pl.ds(b * R, R)], zsem).start()

            return n + partial.astype(jnp.int32)

        nfill = lax.fori_loop(0, nb, fill, 0)

        def drain(k, c):
            pltpu.make_async_copy(zbuf, hs_hbm.at[pl.ds(0, R)], zsem).wait()
            return c

        lax.fori_loop(0, nfill, drain, 0)

    def body(r, c):
        d = dest_ref[i * tp + r]
        pltpu.make_async_copy(hr_ref.at[pl.ds(r, 1)], hs_hbm.at[pl.ds(d, 1)], sem).start()
        return c

    lax.fori_loop(0, tp, body, 0, unroll=8)
    pltpu.make_async_copy(hr_ref, hs_hbm.at[pl.ds(0, tp)], sem).wait()


def _permute(hr, dest, nvalid):
    T, W = hr.shape
    R = MOE_ROWS
    tp = PERM_ROWS
    nb = nvalid.shape[0]
    return pl.pallas_call(
        _permute_kernel,
        out_shape=jax.ShapeDtypeStruct((nb * R, W), F32),
        grid_spec=pltpu.PrefetchScalarGridSpec(
            num_scalar_prefetch=2,
            grid=(T // tp,),
            in_specs=[pl.BlockSpec((tp, W), lambda i, d, n: (i, 0))],
            out_specs=pl.BlockSpec(memory_space=pl.ANY),
            scratch_shapes=[pltpu.VMEM((R, W), F32), pltpu.SemaphoreType.DMA, pltpu.SemaphoreType.DMA]),
        compiler_params=_cparams(1),
        name="permute",
    )(dest, nvalid, hr)


def _moe_kernel(bg_ref, nvalid_ref, hs_ref, w1_ref, w3_ref, w2_ref, y_ref, w1b, w3b, w2b):
    i = pl.program_id(0)
    R, D = y_ref.shape

    @pl.when((i == 0) | (bg_ref[i] != bg_ref[jnp.maximum(i - 1, 0)]))
    def _():
        w1b[...] = w1_ref[...].astype(BF16)
        w3b[...] = w3_ref[...].astype(BF16)
        w2b[...] = w2_ref[...].astype(BF16)

    @pl.when(nvalid_ref[i] > 0)
    def _():
        xb = hs_ref[:, :D].astype(BF16)
        acc = jnp.zeros((R, D), F32)
        for j in range(EXPERTS_PER_GROUP):
            a1 = jnp.dot(xb, w1b[j], preferred_element_type=F32)
            a3 = jnp.dot(xb, w3b[j], preferred_element_type=F32)
            hj = a1 * jax.nn.sigmoid(a1) * a3 * hs_ref[:, D + j:D + j + 1]
            acc = acc + _bdot(hj, w2b[j])
        y_ref[...] = acc

    @pl.when(nvalid_ref[i] == 0)
    def _():
        y_ref[...] = jnp.zeros((R, D), F32)


def _moe(hs, block_g, nvalid, w1, w3, w2, layer):
    D = w1.shape[2]
    W = hs.shape[1]
    R = MOE_ROWS
    nb = nvalid.shape[0]
    wspec_in = pl.BlockSpec((None, EXPERTS_PER_GROUP, D, D_EXPERT), lambda i, bg, nv: (layer, bg[i], 0, 0))
    wspec_out = pl.BlockSpec((None, EXPERTS_PER_GROUP, D_EXPERT, D), lambda i, bg, nv: (layer, bg[i], 0, 0))
    return pl.pallas_call(
        _moe_kernel,
        out_shape=jax.ShapeDtypeStruct((nb * R, D), F32),
        grid_spec=pltpu.PrefetchScalarGridSpec(
            num_scalar_prefetch=2,
            grid=(nb,),
            in_specs=[pl.BlockSpec((R, W), lambda i, bg, nv: (i, 0)), wspec_in, wspec_in, wspec_out],
            out_specs=pl.BlockSpec((R, D), lambda i, bg, nv: (i, 0)),
            scratch_shapes=[pltpu.VMEM((EXPERTS_PER_GROUP, D, D_EXPERT), BF16),
                            pltpu.VMEM((EXPERTS_PER_GROUP, D, D_EXPERT), BF16),
                            pltpu.VMEM((EXPERTS_PER_GROUP, D_EXPERT, D), BF16)]),
        compiler_params=_cparams(1),
        name="moe",
    )(block_g, nvalid, hs, w1, w3, w2)


def _final_kernel(dest_ref, base_ref, ys_hbm, ada_ref, lng, lnb, o_ref, ybuf, sem):
    i = pl.program_id(0)
    n = pl.num_programs(0)
    tf = base_ref.shape[0]
    slot = i % 2

    def start_gather(step, s):
        def body(r, c):
            d = dest_ref[step * tf + r]
            pltpu.make_async_copy(ys_hbm.at[pl.ds(d, 1)], ybuf.at[s, pl.ds(r, 1)], sem.at[s]).start()
            return c

        lax.fori_loop(0, tf, body, 0, unroll=8)

    @pl.when(i == 0)
    def _():
        start_gather(0, 0)

    @pl.when(i + 1 < n)
    def _():
        start_gather(i + 1, 1 - slot)

    pltpu.make_async_copy(ys_hbm.at[pl.ds(0, tf)], ybuf.at[slot], sem.at[slot]).wait()
    ada = ada_ref[...]
    o_ref[...] = _ln_plain(base_ref[...] + ada[5:6, :] * ybuf[slot]) * lng[...] + lnb[...]


def _final(base, ys, dest, ada_l, lng, lnb, B, S):
    T, D = base.shape
    tf = FINAL_ROWS
    per_batch = S // tf
    return pl.pallas_call(
        _final_kernel,
        out_shape=jax.ShapeDtypeStruct((T, D), F32),
        grid_spec=pltpu.PrefetchScalarGridSpec(
            num_scalar_prefetch=1,
            grid=(T // tf,),
            in_specs=[pl.BlockSpec((tf, D), lambda i, d: (i, 0)),
                      pl.BlockSpec(memory_space=pl.ANY),
                      pl.BlockSpec((None, 6, D), lambda i, d: (i // per_batch, 0, 0)),
                      pl.BlockSpec((1, D), lambda i, d: (0, 0)), pl.BlockSpec((1, D), lambda i, d: (0, 0))],
            out_specs=pl.BlockSpec((tf, D), lambda i, d: (i, 0)),
            scratch_shapes=[pltpu.VMEM((2, tf, D), F32), pltpu.SemaphoreType.DMA((2,))]),
        compiler_params=_cparams(1),
        name="final_ln",
    )(dest, base, ys, ada_l, lng, lnb)


def _rope_tables(positions):
    inv_freq = ROPE_THETA ** (-jnp.arange(0, QK_ROPE, 2, dtype=F32) / QK_ROPE)
    ang = positions.astype(F32)[..., None] * inv_freq
    cos, sin = jnp.cos(ang), jnp.sin(ang)
    ones = jnp.ones(cos.shape[:-1] + (QK_NOPE,), F32)
    zpad = jnp.zeros(cos.shape[:-1] + (HEAD_PAD - QK_NOPE - QK_ROPE,), F32)
    cos128 = jnp.concatenate([ones, cos, cos, zpad], axis=-1)
    sin128 = jnp.concatenate([0.0 * ones, -sin, sin, zpad], axis=-1)
    T = positions.shape[0] * positions.shape[1]
    return cos128.reshape(T, HEAD_PAD), sin128.reshape(T, HEAD_PAD)


def _swap_halves(w):
    half = w.shape[-1] // 2
    return jnp.concatenate([w[..., half:], w[..., :half]], axis=-1)


def _layer_weights(l, p):
    D = p["w_in"].shape[1]
    w_in = p["w_in"][l]
    o = 0
    seg = {}
    for name, width in (("cv", 3 * D_CONV), ("ssm", D_SSM), ("cq", Q_LORA), ("ckv", KV_LORA),
                        ("kr", QK_ROPE), ("g", 3 * D)):
        seg[name] = w_in[:, o:o + width]
        o += width
    pad_r = HEAD_PAD - QK_NOPE - QK_ROPE
    zl = jnp.zeros((D, QK_NOPE), F32)
    zr = jnp.zeros((D, pad_r), F32)
    w = {
        "w_cv": seg["cv"].astype(BF16), "w_ssm": seg["ssm"].astype(BF16),
        "w_cq": seg["cq"].astype(BF16), "w_ckv": seg["ckv"].astype(BF16),
        "w_kra": jnp.concatenate([zl, seg["kr"], zr], axis=1).astype(BF16),
        "w_krb": jnp.concatenate([zl, _swap_halves(seg["kr"]), zr], axis=1).astype(BF16),
        "w_g": seg["g"].astype(BF16),
        "convw": p["conv_w"][l],
        "qn": p["q_norm"][l].reshape(1, Q_LORA), "kvn": p["kv_norm"][l].reshape(1, KV_LORA),
        "w_upc": p["w_up_conv"][l].astype(BF16),
    }
    scale = (QK_NOPE + QK_ROPE) ** -0.5 * math.log2(math.e)
    wq = p["w_uq"][l].reshape(Q_LORA, MLA_HEADS, QK_NOPE + QK_ROPE) * scale
    zq = jnp.zeros((Q_LORA, MLA_HEADS, pad_r), F32)
    zn = jnp.zeros((Q_LORA, MLA_HEADS, QK_NOPE), F32)
    w["wq_a"] = jnp.concatenate([wq, zq], axis=-1).reshape(Q_LORA, -1).astype(BF16)
    w["wq_b"] = jnp.concatenate([zn, _swap_halves(wq[..., QK_NOPE:]), zq],
                                axis=-1).reshape(Q_LORA, -1).astype(BF16)
    wk = p["w_uk"][l].reshape(KV_LORA, MLA_HEADS, QK_NOPE)
    zk = jnp.zeros((KV_LORA, MLA_HEADS, HEAD_PAD - QK_NOPE), F32)
    w["wk"] = jnp.concatenate([wk, zk], axis=-1).reshape(KV_LORA, -1).astype(BF16)
    w["wv"] = p["w_uv"][l].astype(BF16)

    ar, ai = p["ssm_a_re"][l], p["ssm_a_im"][l]
    dt = jnp.exp(p["ssm_log_dt"][l])[:, None]
    mag = jnp.exp(ar * dt)
    lb_re, lb_im = mag * jnp.cos(ai * dt), mag * jnp.sin(ai * dt)
    den = ar * ar + ai * ai
    nr, ni = lb_re - 1.0, lb_im
    f_re, f_im = (nr * ar + ni * ai) / den, (ni * ar - nr * ai) / den
    br, bi = p["ssm_b_re"][l], p["ssm_b_im"][l]
    bb_re = f_re[..., None] * br - f_im[..., None] * bi
    bb_im = f_re[..., None] * bi + f_im[..., None] * br
    gpc = LANE // SSM_GROUP_DIM
    nch = SSM_GROUPS // gpc
    eye = jnp.eye(gpc, dtype=F32)

    def bmat(bb):
        return jnp.einsum("cgnp,gh->cgphn", bb.reshape(nch, gpc, SSM_STATE, SSM_GROUP_DIM),
                          eye).reshape(nch, LANE, gpc * SSM_STATE).astype(BF16)

    def cmat(cc):
        return jnp.einsum("cgpn,gh->cgnhp", cc.reshape(nch, gpc, SSM_GROUP_DIM, SSM_STATE),
                          eye).reshape(nch, gpc * SSM_STATE, LANE).astype(BF16)

    w["ssm_bre"], w["ssm_bim"] = bmat(bb_re), bmat(bb_im)
    w["ssm_cre"], w["ssm_cim"] = cmat(p["ssm_c_re"][l]), cmat(-p["ssm_c_im"][l])
    nrow = 8
    w["ssm_lre"] = jnp.broadcast_to(lb_re.reshape(1, -1), (nrow, SSM_GROUPS * SSM_STATE))
    w["ssm_lim"] = jnp.broadcast_to(lb_im.reshape(1, -1), (nrow, SSM_GROUPS * SSM_STATE))
    w["ssm_d"] = p["ssm_d"][l].reshape(1, D_SSM)
    w["ssm_wglu"] = p["ssm_w_glu"][l].astype(BF16)

    w["w_ups"] = p["w_up_ssm"][l].astype(BF16)
    w["w_upa"] = p["w_up_attn"][l].astype(BF16)
    w["w_o"] = p["w_o"][l].astype(BF16)
    w["lng0"], w["lnb0"] = p["ln_g"][l, 0].reshape(1, D), p["ln_b"][l, 0].reshape(1, D)
    w["lng1"], w["lnb1"] = p["ln_g"][l, 1].reshape(1, D), p["ln_b"][l, 1].reshape(1, D)
    rwg = p["router_w"].reshape(D, N_EXPERT_GROUPS, EXPERTS_PER_GROUP)
    rbg = p["router_bias"].reshape(N_EXPERT_GROUPS, EXPERTS_PER_GROUP)
    w["rw"] = jnp.transpose(rwg, (2, 1, 0)).reshape(N_EXPERTS, D)
    w["rb"] = jnp.transpose(rbg, (1, 0)).reshape(N_EXPERTS, 1)
    w["ws1"] = p["shared_w1"][l].astype(BF16)
    w["ws3"] = p["shared_w3"][l].astype(BF16)
    w["ws2"] = p["shared_w2"][l].astype(BF16)
    return w


def kernel(x, c, positions, w_ada, b_ada, w_in, conv_w, ssm_a_re, ssm_a_im, ssm_b_re, ssm_b_im,
           ssm_c_re, ssm_c_im, ssm_d, ssm_log_dt, ssm_w_glu, q_norm, w_uq, kv_norm, w_uk, w_uv,
           w_up_conv, w_up_ssm, w_up_attn, w_o, ln_g, ln_b, router_w, router_bias,
           exp_w1, exp_w3, exp_w2, shared_w1, shared_w3, shared_w2):
    B, S, D = x.shape
    L = w_in.shape[0]
    T = B * S
    assert B == 8 and S % max(TM_PROJ, T_CHUNK, TQ, FINAL_ROWS) == 0 and T % max(MOE_ROWS, PERM_ROWS) == 0
    p = dict(w_in=w_in, conv_w=conv_w, ssm_a_re=ssm_a_re, ssm_a_im=ssm_a_im, ssm_b_re=ssm_b_re,
             ssm_b_im=ssm_b_im, ssm_c_re=ssm_c_re, ssm_c_im=ssm_c_im, ssm_d=ssm_d,
             ssm_log_dt=ssm_log_dt, ssm_w_glu=ssm_w_glu, q_norm=q_norm, w_uq=w_uq, kv_norm=kv_norm,
             w_uk=w_uk, w_uv=w_uv, w_up_conv=w_up_conv, w_up_ssm=w_up_ssm, w_up_attn=w_up_attn,
             w_o=w_o, ln_g=ln_g, ln_b=ln_b, router_w=router_w, router_bias=router_bias,
             exp_w1=exp_w1, exp_w3=exp_w3, exp_w2=exp_w2, shared_w1=shared_w1,
             shared_w3=shared_w3, shared_w2=shared_w2)
    alpha = (2 * L) ** 0.25
    cos128, sin128 = _rope_tables(positions)
    ada = _ada(c, w_ada, b_ada).reshape(L, B, 6, D)
    x2d = x.reshape(T, D)
    for l in range(L):
        wts = _layer_weights(l, p)
        pc, sgs, sga, u, q, k, v = _inproj(x2d, ada[l], cos128, sin128, wts, B, S)
        ys = _ssm(u.reshape(B, S, D_SSM), wts, B, S).reshape(T, D_SSM)
        ya = _attention(q, k, v, B, S)
        base, hr = _merge(x2d, ada[l], pc, sgs, sga, ys, ya, wts, B, S, alpha)
        gid = hr[:, D + EXPERTS_PER_GROUP].astype(jnp.int32)
        dest, block_g, nvalid = _route_plan(gid, T)
        hs = _permute(hr, dest, nvalid)
        ysort = _moe(hs, block_g, nvalid, exp_w1, exp_w3, exp_w2, l)
        x2d = _final(base, ysort, dest, ada[l], wts["lng1"], wts["lnb1"], B, S)
    return x2d.reshape(B, S, D)
```

```python
import functools
import math

import jax
import jax.numpy as jnp
from jax import lax
from jax.experimental import pallas as pl
from jax.experimental.pallas import tpu as pltpu

F32 = jnp.float32
BF16 = jnp.bfloat16

D_CONV = 512
D_SSM = 512
SSM_GROUP_DIM = 16
SSM_GROUPS = 32
SSM_STATE = 64
MLA_HEADS = 8
QK_NOPE = 64
QK_ROPE = 32
V_HEAD = 64
Q_LORA = 256
KV_LORA = 128
ROPE_THETA = 10000.0
N_EXPERTS = 32
N_EXPERT_GROUPS = 8
EXPERTS_PER_GROUP = 4
D_EXPERT = 256
LN_EPS = 1e-5
RMS_EPS = 1e-6

LANE = 128
HEAD_PAD = 128
VMEM_LIMIT = 56 * 1024 * 1024

TM_PROJ = 512
T_CHUNK = 64
TQ = 512
ATTN_HEADS = 4
DEN_ROWS = 16
MOE_ROWS = 256
PERM_ROWS = 1024
FINAL_ROWS = 512
ROUTE_W = 128


def _cparams(n_axes):
    return pltpu.CompilerParams(dimension_semantics=("arbitrary",) * n_axes,
                                vmem_limit_bytes=VMEM_LIMIT)


def _ln_plain(x):
    mu = jnp.mean(x, axis=-1, keepdims=True)
    xc = x - mu
    var = jnp.mean(xc * xc, axis=-1, keepdims=True)
    return xc * lax.rsqrt(var + LN_EPS)


def _rms(x, g):
    return x * lax.rsqrt(jnp.mean(x * x, axis=-1, keepdims=True) + RMS_EPS) * g


def _bdot(a, b):
    return jnp.dot(a.astype(BF16), b, preferred_element_type=F32)


def _ada_kernel(c_ref, w_ref, b_ref, o_ref):
    c = c_ref[...]
    cond = c * jax.nn.sigmoid(c)
    o_ref[...] = jnp.dot(cond, w_ref[...], precision=lax.Precision.HIGHEST,
                         preferred_element_type=F32) + b_ref[...]


def _ada(c, w_ada, b_ada):
    L, D, D6 = w_ada.shape
    B = c.shape[0]
    nj = D6 // D
    return pl.pallas_call(
        _ada_kernel,
        out_shape=jax.ShapeDtypeStruct((L, B, D6), F32),
        grid=(L, nj),
        in_specs=[pl.BlockSpec((B, D), lambda l, j: (0, 0)),
                  pl.BlockSpec((None, D, D), lambda l, j: (l, 0, j)),
                  pl.BlockSpec((None, 1, D), lambda l, j: (l, 0, j))],
        out_specs=pl.BlockSpec((None, B, D), lambda l, j: (l, 0, j)),
        compiler_params=_cparams(2),
        name="ada",
    )(c, w_ada, b_ada.reshape(L, 1, D6))


def _inproj_kernel(x_ref, ada_ref, cos_ref, sin_ref,
                   w_cv, w_ssm, w_cq, w_ckv, w_kra, w_krb, w_g,
                   convw, qn, wq_a, wq_b, kvn, wk, wv, w_upc,
                   pc_ref, sgs_ref, sga_ref, u_ref, q_ref, k_ref, v_ref,
                   carry_ref):
    si = pl.program_id(1)
    D = x_ref.shape[1]
    tm = x_ref.shape[0]

    @pl.when(si == 0)
    def _():
        carry_ref[...] = jnp.zeros_like(carry_ref)

    ada = ada_ref[...]
    h = _ln_plain(x_ref[...]) * (1.0 + ada[1:2, :]) + ada[0:1, :]
    hb = h.astype(BF16)

    pcv = jnp.dot(hb, w_cv[...], preferred_element_type=F32)
    u = pcv[:, D_CONV:2 * D_CONV] * pcv[:, :D_CONV]
    gb = pcv[:, 2 * D_CONV:]
    tail = carry_ref[...]
    row8 = lax.broadcasted_iota(jnp.int32, (8, D_CONV), 0)
    r1 = pltpu.roll(u, 1, 0)
    r2 = pltpu.roll(u, 2, 0)
    t1 = pltpu.roll(tail, 1, 0)
    t2 = pltpu.roll(tail, 2, 0)
    u1 = jnp.concatenate([jnp.where(row8 < 1, t1, r1[:8]), r1[8:]], axis=0)
    u2 = jnp.concatenate([jnp.where(row8 < 2, t2, r2[:8]), r2[8:]], axis=0)
    cw = convw[...]
    yc = gb * (cw[0:1, :] * u2 + cw[1:2, :] * u1 + cw[2:3, :] * u)
    carry_ref[...] = u[tm - 8:, :]

    gates = jnp.dot(hb, w_g[...], preferred_element_type=F32)
    pc_ref[...] = (jax.nn.sigmoid(gates[:, :D]) * _bdot(yc, w_upc[...])).astype(BF16)
    sgs_ref[...] = jax.nn.sigmoid(gates[:, D:2 * D]).astype(BF16)
    sga_ref[...] = jax.nn.sigmoid(gates[:, 2 * D:]).astype(BF16)

    u_ref[...] = jnp.dot(hb, w_ssm[...], preferred_element_type=F32).astype(BF16)

    cos = cos_ref[...]
    sin = sin_ref[...]
    cq = _rms(jnp.dot(hb, w_cq[...], preferred_element_type=F32), qn[...]).astype(BF16)
    qa = jnp.dot(cq, wq_a[...], preferred_element_type=F32)
    qb = jnp.dot(cq, wq_b[...], preferred_element_type=F32)
    ckv = _rms(jnp.dot(hb, w_ckv[...], preferred_element_type=F32), kvn[...]).astype(BF16)
    kn = jnp.dot(ckv, wk[...], preferred_element_type=F32)
    kr = (jnp.dot(hb, w_kra[...], preferred_element_type=F32) * cos
          + jnp.dot(hb, w_krb[...], preferred_element_type=F32) * sin)
    for hd in range(MLA_HEADS):
        sl = slice(hd * HEAD_PAD, (hd + 1) * HEAD_PAD)
        q_ref[:, sl] = (qa[:, sl] * cos + qb[:, sl] * sin).astype(BF16)
        k_ref[:, sl] = (kn[:, sl] + kr).astype(BF16)
    v_ref[...] = lax.dot_general(wv[...], ckv, (((1,), (1,)), ((), ())),
                                 preferred_element_type=F32).astype(BF16)


def _inproj(x2d, ada_l, cos128, sin128, wts, B, S):
    T, D = x2d.shape
    tm = TM_PROJ
    ns = S // tm
    row = lambda b, s: (b * ns + s, 0)
    full = lambda a: pl.BlockSpec(a.shape, lambda b, s: (0,) * a.ndim, pipeline_mode=pl.Buffered(1))
    names = ["w_cv", "w_ssm", "w_cq", "w_ckv", "w_kra", "w_krb", "w_g", "convw", "qn",
             "wq_a", "wq_b", "kvn", "wk", "wv", "w_upc"]
    ws = [wts[n] for n in names]
    out_shape = (
        jax.ShapeDtypeStruct((T, D), BF16),
        jax.ShapeDtypeStruct((T, D), BF16),
        jax.ShapeDtypeStruct((T, D), BF16),
        jax.ShapeDtypeStruct((T, D_SSM), BF16),
        jax.ShapeDtypeStruct((T, MLA_HEADS * HEAD_PAD), BF16),
        jax.ShapeDtypeStruct((T, MLA_HEADS * HEAD_PAD), BF16),
        jax.ShapeDtypeStruct((B, ns, MLA_HEADS * V_HEAD, tm), BF16),
    )
    out_specs = (
        pl.BlockSpec((tm, D), row), pl.BlockSpec((tm, D), row), pl.BlockSpec((tm, D), row),
        pl.BlockSpec((tm, D_SSM), row),
        pl.BlockSpec((tm, MLA_HEADS * HEAD_PAD), row),
        pl.BlockSpec((tm, MLA_HEADS * HEAD_PAD), row),
        pl.BlockSpec((None, None, MLA_HEADS * V_HEAD, tm), lambda b, s: (b, s, 0, 0)),
    )
    return pl.pallas_call(
        _inproj_kernel,
        out_shape=out_shape,
        grid=(B, ns),
        in_specs=[pl.BlockSpec((tm, D), row),
                  pl.BlockSpec((None, 6, D), lambda b, s: (b, 0, 0)),
                  pl.BlockSpec((tm, LANE), row), pl.BlockSpec((tm, LANE), row)]
                 + [full(a) for a in ws],
        out_specs=out_specs,
        scratch_shapes=[pltpu.VMEM((8, D_CONV), F32)],
        compiler_params=_cparams(2),
        name="inproj",
    )(x2d, ada_l, cos128, sin128, *ws)


def _ssm_kernel(u_ref, bre_ref, bim_ref, lre_ref, lim_ref, cre_ref, cim_ref, d_ref, wglu_ref,
                o_ref, sre, sim, st_re, st_im, tmaj):
    i = pl.program_id(0)
    batch, nsteps = u_ref.shape[0], u_ref.shape[1]
    rows = batch * nsteps
    nchunk = bre_ref.shape[0]
    cw = bre_ref.shape[2]

    @pl.when(i == 0)
    def _():
        st_re[...] = jnp.zeros_like(st_re)
        st_im[...] = jnp.zeros_like(st_im)

    for b in range(batch):
        tmaj[:, b, :] = u_ref[b].astype(F32)
    uf = tmaj[...].reshape(rows, D_SSM)
    ub = uf.astype(BF16)
    for c in range(nchunk):
        uc = ub[:, c * LANE:(c + 1) * LANE]
        sre[:, c * cw:(c + 1) * cw] = jnp.dot(uc, bre_ref[c], preferred_element_type=F32)
        sim[:, c * cw:(c + 1) * cw] = jnp.dot(uc, bim_ref[c], preferred_element_type=F32)

    def step(t, carry):
        pr, pi = carry
        r = pl.multiple_of(t * batch, batch)
        lre = lre_ref[...]
        lim = lim_ref[...]
        nr = lre * pr - lim * pi + sre[pl.ds(r, batch), :]
        ni = lre * pi + lim * pr + sim[pl.ds(r, batch), :]
        sre[pl.ds(r, batch), :] = nr
        sim[pl.ds(r, batch), :] = ni
        return nr, ni

    fr, fi = lax.fori_loop(0, nsteps, step, (st_re[...], st_im[...]))
    st_re[...] = fr
    st_im[...] = fi

    ys = []
    for c in range(nchunk):
        ys.append(_bdot(sre[:, c * cw:(c + 1) * cw], cre_ref[c])
                  + _bdot(sim[:, c * cw:(c + 1) * cw], cim_ref[c]))
    y = jnp.concatenate(ys, axis=1) + d_ref[...] * uf
    y = 0.5 * y * (1.0 + jnp.tanh(math.sqrt(2.0 / math.pi) * (y + 0.044715 * (y * y * y))))
    z = _bdot(y, wglu_ref[...])
    tmaj[...] = (z[:, :D_SSM] * jax.nn.sigmoid(z[:, D_SSM:])).reshape(nsteps, batch, D_SSM)
    for b in range(batch):
        o_ref[b] = tmaj[:, b, :].astype(BF16)


def _ssm(u, wts, B, S):
    rows = T_CHUNK * B
    n = S // T_CHUNK
    nstate = SSM_GROUPS * SSM_STATE
    names = ["ssm_bre", "ssm_bim", "ssm_lre", "ssm_lim", "ssm_cre", "ssm_cim", "ssm_d", "ssm_wglu"]
    ws = [wts[k] for k in names]
    full = lambda a: pl.BlockSpec(a.shape, lambda i: (0,) * a.ndim, pipeline_mode=pl.Buffered(1))
    return pl.pallas_call(
        _ssm_kernel,
        out_shape=jax.ShapeDtypeStruct((B, S, D_SSM), BF16),
        grid=(n,),
        in_specs=[pl.BlockSpec((B, T_CHUNK, D_SSM), lambda i: (0, i, 0))] + [full(a) for a in ws],
        out_specs=pl.BlockSpec((B, T_CHUNK, D_SSM), lambda i: (0, i, 0)),
        scratch_shapes=[pltpu.VMEM((rows, nstate), F32), pltpu.VMEM((rows, nstate), F32),
                        pltpu.VMEM((B, nstate), F32), pltpu.VMEM((B, nstate), F32),
                        pltpu.VMEM((T_CHUNK, B, D_SSM), F32)],
        compiler_params=_cparams(1),
        name="ssm",
    )(u, *ws)


def _attn_kernel(q_ref, k_ref, vt_ref, o_ref, m_sc, acc_sc, s_sc, p_sc):
    qi = pl.program_id(2)
    tq = q_ref.shape[0]
    nh = m_sc.shape[0]
    neg = -1e30
    m_sc[...] = jnp.full(m_sc.shape, neg, F32)
    acc_sc[...] = jnp.zeros(acc_sc.shape, F32)
    ones = jnp.ones((DEN_ROWS, tq), BF16)

    def block(j, masked):
        off = pl.multiple_of(j * tq, tq)
        for hh in range(nh):
            q = q_ref[:, hh * HEAD_PAD:(hh + 1) * HEAD_PAD]
            kb = k_ref[pl.ds(off, tq), hh * HEAD_PAD:(hh + 1) * HEAD_PAD]
            s_sc[hh] = lax.dot_general(kb, q, (((1,), (1,)), ((), ())),
                                       preferred_element_type=F32)
        scale = []
        for hh in range(nh):
            if masked:
                keyi = lax.broadcasted_iota(jnp.int32, (tq, tq), 0)
                qryi = lax.broadcasted_iota(jnp.int32, (tq, tq), 1)
                s_sc[hh] = jnp.where(keyi <= qryi, s_sc[hh], neg)
            m_prev = m_sc[hh]
            m_new = jnp.maximum(m_prev, jnp.max(s_sc[hh], axis=0, keepdims=True))
            m_sc[hh] = m_new
            p_sc[hh] = jnp.exp2(s_sc[hh] - m_new).astype(BF16)
            scale.append(jnp.exp2(m_prev - m_new))
        for hh in range(nh):
            vte = jnp.concatenate([vt_ref[j, hh * V_HEAD:(hh + 1) * V_HEAD, :], ones], axis=0)
            acc_sc[hh] = scale[hh] * acc_sc[hh] + jnp.dot(vte, p_sc[hh], preferred_element_type=F32)

    def full_block(j, c):
        block(j, False)
        return c

    lax.fori_loop(0, qi, full_block, 0)
    block(qi, True)
    outs = []
    for pr in range(nh // 2):
        halves = []
        for hh in (2 * pr, 2 * pr + 1):
            acc = acc_sc[hh]
            halves.append(acc[:V_HEAD] * (1.0 / acc[V_HEAD:V_HEAD + 1]))
        outs.append(jnp.concatenate(halves, axis=0).T)
    o_ref[...] = jnp.concatenate(outs, axis=1).astype(BF16)


def _attention(q, k, vt, B, S):
    nq = S // TQ
    nh = ATTN_HEADS
    hg = MLA_HEADS // nh
    return pl.pallas_call(
        _attn_kernel,
        out_shape=jax.ShapeDtypeStruct((B * S, MLA_HEADS * V_HEAD), BF16),
        grid=(B, hg, nq),
        in_specs=[pl.BlockSpec((TQ, nh * HEAD_PAD), lambda b, h, i: (b * nq + i, h)),
                  pl.BlockSpec((S, nh * HEAD_PAD), lambda b, h, i: (b, h)),
                  pl.BlockSpec((None, nq, nh * V_HEAD, TQ), lambda b, h, i: (b, 0, h, 0))],
        out_specs=pl.BlockSpec((TQ, nh * V_HEAD), lambda b, h, i: (b * nq + i, h)),
        scratch_shapes=[pltpu.VMEM((nh, 1, TQ), F32),
                        pltpu.VMEM((nh, V_HEAD + DEN_ROWS, TQ), F32),
                        pltpu.VMEM((nh, TQ, TQ), F32), pltpu.VMEM((nh, TQ, TQ), BF16)],
        compiler_params=_cparams(3),
        name="attention",
    )(q, k, vt)


def _merge_kernel(x_ref, ada_ref, pc_ref, sgs_ref, sga_ref, ys_ref, ya_ref,
                  w_ups, w_upa, w_o, lng, lnb, rw, rb, ws1, ws3, ws2,
                  base_ref, hr_ref, *, alpha):
    D = x_ref.shape[1]
    tm = x_ref.shape[0]
    ada = ada_ref[...]
    merged = (pc_ref[...].astype(F32)
              + sgs_ref[...].astype(F32) * jnp.dot(ys_ref[...], w_ups[...], preferred_element_type=F32)
              + sga_ref[...].astype(F32) * jnp.dot(ya_ref[...], w_upa[...], preferred_element_type=F32))
    y = _bdot(merged, w_o[...])
    x1 = _ln_plain(alpha * x_ref[...] + ada[2:3, :] * y) * lng[...] + lnb[...]
    h2 = _ln_plain(x1) * (1.0 + ada[4:5, :]) + ada[3:4, :]
    hr_ref[:, :D] = h2

    G = N_EXPERT_GROUPS
    logits = lax.dot_general(rw[...], h2, (((1,), (1,)), ((), ())),
                             precision=lax.Precision.HIGHEST, preferred_element_type=F32)
    sc0 = jax.nn.sigmoid(logits)
    sl0 = sc0 + rb[...]
    sc = [sc0[j * G:(j + 1) * G] for j in range(EXPERTS_PER_GROUP)]
    sl = [sl0[j * G:(j + 1) * G] for j in range(EXPERTS_PER_GROUP)]
    hi1, lo1 = jnp.maximum(sl[0], sl[1]), jnp.minimum(sl[0], sl[1])
    hi2, lo2 = jnp.maximum(sl[2], sl[3]), jnp.minimum(sl[2], sl[3])
    gscore = jnp.maximum(hi1, hi2) + jnp.maximum(jnp.minimum(hi1, hi2), jnp.maximum(lo1, lo2))
    sub = lax.broadcasted_iota(jnp.int32, (G, tm), 0)
    gmax = jnp.max(gscore, axis=0, keepdims=True)
    gidx = jnp.min(jnp.where(gscore == gmax, sub, G), axis=0, keepdims=True)
    pick = sub == gidx
    vs = [jnp.sum(jnp.where(pick, a, 0.0), axis=0, keepdims=True) for a in sl]
    ss = [jnp.sum(jnp.where(pick, a, 0.0), axis=0, keepdims=True) for a in sc]
    chosen = []
    for j in range(EXPERTS_PER_GROUP):
        cnt = jnp.zeros((1, tm), jnp.int32)
        for k in range(EXPERTS_PER_GROUP):
            if k == j:
                continue
            beats = (vs[k] > vs[j]) | ((vs[k] == vs[j]) & (k < j))
            cnt = cnt + beats.astype(jnp.int32)
        chosen.append(cnt < 2)
    wsel = [jnp.where(chosen[j], ss[j], 0.0) for j in range(EXPERTS_PER_GROUP)]
    wsum = wsel[0] + wsel[1] + wsel[2] + wsel[3]
    rows = [wsel[j] / wsum for j in range(EXPERTS_PER_GROUP)] + [gidx.astype(F32)]
    rows += [jnp.zeros((1, tm), F32)] * (8 - len(rows))
    route_t = jnp.concatenate(rows, axis=0)
    eye = (lax.broadcasted_iota(jnp.int32, (8, ROUTE_W), 0)
           == lax.broadcasted_iota(jnp.int32, (8, ROUTE_W), 1)).astype(F32)
    hr_ref[:, D:] = lax.dot_general(route_t, eye, (((0,), (0,)), ((), ())),
                                    precision=lax.Precision.HIGHEST, preferred_element_type=F32)

    h2b = h2.astype(BF16)
    a1 = jnp.dot(h2b, ws1[...], preferred_element_type=F32)
    a3 = jnp.dot(h2b, ws3[...], preferred_element_type=F32)
    ysh = _bdot(a1 * jax.nn.sigmoid(a1) * a3, ws2[...])
    base_ref[...] = alpha * x1 + ada[5:6, :] * ysh


def _merge(x2d, ada_l, pc, sgs, sga, ys, ya, wts, B, S, alpha):
    T, D = x2d.shape
    tm = TM_PROJ
    ns = S // tm
    row = lambda b, s: (b * ns + s, 0)
    names = ["w_ups", "w_upa", "w_o", "lng0", "lnb0", "rw", "rb", "ws1", "ws3", "ws2"]
    ws = [wts[n] for n in names]
    full = lambda a: pl.BlockSpec(a.shape, lambda b, s: (0,) * a.ndim, pipeline_mode=pl.Buffered(1))
    return pl.pallas_call(
        functools.partial(_merge_kernel, alpha=alpha),
        out_shape=(jax.ShapeDtypeStruct((T, D), F32),
                   jax.ShapeDtypeStruct((T, D + ROUTE_W), F32)),
        grid=(B, ns),
        in_specs=[pl.BlockSpec((tm, D), row),
                  pl.BlockSpec((None, 6, D), lambda b, s: (b, 0, 0)),
                  pl.BlockSpec((tm, D), row), pl.BlockSpec((tm, D), row), pl.BlockSpec((tm, D), row),
                  pl.BlockSpec((tm, D_SSM), row),
                  pl.BlockSpec((tm, MLA_HEADS * V_HEAD), row)]
                 + [full(a) for a in ws],
        out_specs=(pl.BlockSpec((tm, D), row), pl.BlockSpec((tm, D + ROUTE_W), row)),
        compiler_params=_cparams(2),
        name="merge",
    )(x2d, ada_l, pc, sgs, sga, ys, ya, *ws)


def _route_plan(gid, T):
    R = MOE_ROWS
    G = N_EXPERT_GROUPS
    onehot = (gid[:, None] == jnp.arange(G, dtype=jnp.int32)[None, :]).astype(jnp.int32)
    csum = jnp.cumsum(onehot, axis=0)
    counts = csum[-1]
    rank = jnp.sum(onehot * csum, axis=1) - 1
    padded = (counts + R - 1) // R * R
    pad_end = jnp.cumsum(padded)
    pad_start = pad_end - padded
    dest = jnp.sum(onehot * pad_start[None, :], axis=1) + rank
    nb = T // R + G
    starts = jnp.arange(nb, dtype=jnp.int32) * R
    block_g = jnp.minimum(jnp.sum((starts[:, None] >= pad_end[None, :]).astype(jnp.int32), axis=1), G - 1)
    gsel = (block_g[:, None] == jnp.arange(G, dtype=jnp.int32)[None, :]).astype(jnp.int32)
    group_end = jnp.sum(gsel * (pad_start + counts)[None, :], axis=1)
    nvalid = jnp.clip(group_end - starts, 0, R)
    return dest.astype(jnp.int32), block_g.astype(jnp.int32), nvalid.astype(jnp.int32)


def _permute_kernel(dest_ref, nvalid_ref, hr_ref, hs_hbm, zbuf, sem, zsem):
    i = pl.program_id(0)
    tp = hr_ref.shape[0]
    R = zbuf.shape[0]
    nb = nvalid_ref.shape[0]

    @pl.when(i == 0)
    def _():
        zbuf[...] = jnp.zeros(zbuf.shape, F32)

        def fill(b, n):
            partial = nvalid_ref[b] < R

            @pl.when(partial)
            def _():
                pltpu.make_async_copy(zbuf, hs_hbm.at[pl.ds(b * R, R)], zsem).start()

            return n + partial.astype(jnp.int32)

        nfill = lax.fori_loop(0, nb, fill, 0)

        def drain(k, c):
            pltpu.make_async_copy(zbuf, hs_hbm.at[pl.ds(0, R)], zsem).wait()
            return c

        lax.fori_loop(0, nfill, drain, 0)

    def body(r, c):
        d = dest_ref[i * tp + r]
        pltpu.make_async_copy(hr_ref.at[pl.ds(r, 1)], hs_hbm.at[pl.ds(d, 1)], sem).start()
        return c

    lax.fori_loop(0, tp, body, 0, unroll=8)
    pltpu.make_async_copy(hr_ref, hs_hbm.at[pl.ds(0, tp)], sem).wait()


def _permute(hr, dest, nvalid):
    T, W = hr.shape
    R = MOE_ROWS
    tp = PERM_ROWS
    nb = nvalid.shape[0]
    return pl.pallas_call(
        _permute_kernel,
        out_shape=jax.ShapeDtypeStruct((nb * R, W), F32),
        grid_spec=pltpu.PrefetchScalarGridSpec(
            num_scalar_prefetch=2,
            grid=(T // tp,),
            in_specs=[pl.BlockSpec((tp, W), lambda i, d, n: (i, 0))],
            out_specs=pl.BlockSpec(memory_space=pl.ANY),
            scratch_shapes=[pltpu.VMEM((R, W), F32), pltpu.SemaphoreType.DMA, pltpu.SemaphoreType.DMA]),
        compiler_params=_cparams(1),
        name="permute",
    )(dest, nvalid, hr)


def _moe_kernel(bg_ref, nvalid_ref, hs_ref, w1_ref, w3_ref, w2_ref, y_ref, w1b, w3b, w2b):
    i = pl.program_id(0)
    R, D = y_ref.shape

    @pl.when((i == 0) | (bg_ref[i] != bg_ref[jnp.maximum(i - 1, 0)]))
    def _():
        w1b[...] = w1_ref[...].astype(BF16)
        w3b[...] = w3_ref[...].astype(BF16)
        w2b[...] = w2_ref[...].astype(BF16)

    @pl.when(nvalid_ref[i] > 0)
    def _():
        xb = hs_ref[:, :D].astype(BF16)
        acc = jnp.zeros((R, D), F32)
        for j in range(EXPERTS_PER_GROUP):
            a1 = jnp.dot(xb, w1b[j], preferred_element_type=F32)
            a3 = jnp.dot(xb, w3b[j], preferred_element_type=F32)
            hj = a1 * jax.nn.sigmoid(a1) * a3 * hs_ref[:, D + j:D + j + 1]
            acc = acc + _bdot(hj, w2b[j])
        y_ref[...] = acc

    @pl.when(nvalid_ref[i] == 0)
    def _():
        y_ref[...] = jnp.zeros((R, D), F32)


def _moe(hs, block_g, nvalid, w1, w3, w2, layer):
    D = w1.shape[2]
    W = hs.shape[1]
    R = MOE_ROWS
    nb = nvalid.shape[0]
    wspec_in = pl.BlockSpec((None, EXPERTS_PER_GROUP, D, D_EXPERT), lambda i, bg, nv: (layer, bg[i], 0, 0))
    wspec_out = pl.BlockSpec((None, EXPERTS_PER_GROUP, D_EXPERT, D), lambda i, bg, nv: (layer, bg[i], 0, 0))
    return pl.pallas_call(
        _moe_kernel,
        out_shape=jax.ShapeDtypeStruct((nb * R, D), F32),
        grid_spec=pltpu.PrefetchScalarGridSpec(
            num_scalar_prefetch=2,
            grid=(nb,),
            in_specs=[pl.BlockSpec((R, W), lambda i, bg, nv: (i, 0)), wspec_in, wspec_in, wspec_out],
            out_specs=pl.BlockSpec((R, D), lambda i, bg, nv: (i, 0)),
            scratch_shapes=[pltpu.VMEM((EXPERTS_PER_GROUP, D, D_EXPERT), BF16),
                            pltpu.VMEM((EXPERTS_PER_GROUP, D, D_EXPERT), BF16),
                            pltpu.VMEM((EXPERTS_PER_GROUP, D_EXPERT, D), BF16)]),
        compiler_params=_cparams(1),
        name="moe",
    )(block_g, nvalid, hs, w1, w3, w2)


def _final_kernel(dest_ref, base_ref, ys_hbm, ada_ref, lng, lnb, o_ref, ybuf, sem):
    i = pl.program_id(0)
    n = pl.num_programs(0)
    tf = base_ref.shape[0]
    slot = i % 2

    def start_gather(step, s):
        def body(r, c):
            d = dest_ref[step * tf + r]
            pltpu.make_async_copy(ys_hbm.at[pl.ds(d, 1)], ybuf.at[s, pl.ds(r, 1)], sem.at[s]).start()
            return c

        lax.fori_loop(0, tf, body, 0, unroll=8)

    @pl.when(i == 0)
    def _():
        start_gather(0, 0)

    @pl.when(i + 1 < n)
    def _():
        start_gather(i + 1, 1 - slot)

    pltpu.make_async_copy(ys_hbm.at[pl.ds(0, tf)], ybuf.at[slot], sem.at[slot]).wait()
    ada = ada_ref[...]
    o_ref[...] = _ln_plain(base_ref[...] + ada[5:6, :] * ybuf[slot]) * lng[...] + lnb[...]


def _final(base, ys, dest, ada_l, lng, lnb, B, S):
    T, D = base.shape
    tf = FINAL_ROWS
    per_batch = S // tf
    return pl.pallas_call(
        _final_kernel,
        out_shape=jax.ShapeDtypeStruct((T, D), F32),
        grid_spec=pltpu.PrefetchScalarGridSpec(
            num_scalar_prefetch=1,
            grid=(T // tf,),
            in_specs=[pl.BlockSpec((tf, D), lambda i, d: (i, 0)),
                      pl.BlockSpec(memory_space=pl.ANY),
                      pl.BlockSpec((None, 6, D), lambda i, d: (i // per_batch, 0, 0)),
                      pl.BlockSpec((1, D), lambda i, d: (0, 0)), pl.BlockSpec((1, D), lambda i, d: (0, 0))],
            out_specs=pl.BlockSpec((tf, D), lambda i, d: (i, 0)),
            scratch_shapes=[pltpu.VMEM((2, tf, D), F32), pltpu.SemaphoreType.DMA((2,))]),
        compiler_params=_cparams(1),
        name="final_ln",
    )(dest, base, ys, ada_l, lng, lnb)


def _rope_tables(positions):
    inv_freq = ROPE_THETA ** (-jnp.arange(0, QK_ROPE, 2, dtype=F32) / QK_ROPE)
    ang = positions.astype(F32)[..., None] * inv_freq
    cos, sin = jnp.cos(ang), jnp.sin(ang)
    ones = jnp.ones(cos.shape[:-1] + (QK_NOPE,), F32)
    zpad = jnp.zeros(cos.shape[:-1] + (HEAD_PAD - QK_NOPE - QK_ROPE,), F32)
    cos128 = jnp.concatenate([ones, cos, cos, zpad], axis=-1)
    sin128 = jnp.concatenate([0.0 * ones, -sin, sin, zpad], axis=-1)
    T = positions.shape[0] * positions.shape[1]
    return cos128.reshape(T, HEAD_PAD), sin128.reshape(T, HEAD_PAD)


def _swap_halves(w):
    half = w.shape[-1] // 2
    return jnp.concatenate([w[..., half:], w[..., :half]], axis=-1)


def _layer_weights(l, p):
    D = p["w_in"].shape[1]
    w_in = p["w_in"][l]
    o = 0
    seg = {}
    for name, width in (("cv", 3 * D_CONV), ("ssm", D_SSM), ("cq", Q_LORA), ("ckv", KV_LORA),
                        ("kr", QK_ROPE), ("g", 3 * D)):
        seg[name] = w_in[:, o:o + width]
        o += width
    pad_r = HEAD_PAD - QK_NOPE - QK_ROPE
    zl = jnp.zeros((D, QK_NOPE), F32)
    zr = jnp.zeros((D, pad_r), F32)
    w = {
        "w_cv": seg["cv"].astype(BF16), "w_ssm": seg["ssm"].astype(BF16),
        "w_cq": seg["cq"].astype(BF16), "w_ckv": seg["ckv"].astype(BF16),
        "w_kra": jnp.concatenate([zl, seg["kr"], zr], axis=1).astype(BF16),
        "w_krb": jnp.concatenate([zl, _swap_halves(seg["kr"]), zr], axis=1).astype(BF16),
        "w_g": seg["g"].astype(BF16),
        "convw": p["conv_w"][l],
        "qn": p["q_norm"][l].reshape(1, Q_LORA), "kvn": p["kv_norm"][l].reshape(1, KV_LORA),
        "w_upc": p["w_up_conv"][l].astype(BF16),
    }
    scale = (QK_NOPE + QK_ROPE) ** -0.5 * math.log2(math.e)
    wq = p["w_uq"][l].reshape(Q_LORA, MLA_HEADS, QK_NOPE + QK_ROPE) * scale
    zq = jnp.zeros((Q_LORA, MLA_HEADS, pad_r), F32)
    zn = jnp.zeros((Q_LORA, MLA_HEADS, QK_NOPE), F32)
    w["wq_a"] = jnp.concatenate([wq, zq], axis=-1).reshape(Q_LORA, -1).astype(BF16)
    w["wq_b"] = jnp.concatenate([zn, _swap_halves(wq[..., QK_NOPE:]), zq],
                                axis=-1).reshape(Q_LORA, -1).astype(BF16)
    wk = p["w_uk"][l].reshape(KV_LORA, MLA_HEADS, QK_NOPE)
    zk = jnp.zeros((KV_LORA, MLA_HEADS, HEAD_PAD - QK_NOPE), F32)
    w["wk"] = jnp.concatenate([wk, zk], axis=-1).reshape(KV_LORA, -1).astype(BF16)
    w["wv"] = p["w_uv"][l].T.astype(BF16)

    ar, ai = p["ssm_a_re"][l], p["ssm_a_im"][l]
    dt = jnp.exp(p["ssm_log_dt"][l])[:, None]
    mag = jnp.exp(ar * dt)
    lb_re, lb_im = mag * jnp.cos(ai * dt), mag * jnp.sin(ai * dt)
    den = ar * ar + ai * ai
    nr, ni = lb_re - 1.0, lb_im
    f_re, f_im = (nr * ar + ni * ai) / den, (ni * ar - nr * ai) / den
    br, bi = p["ssm_b_re"][l], p["ssm_b_im"][l]
    bb_re = f_re[..., None] * br - f_im[..., None] * bi
    bb_im = f_re[..., None] * bi + f_im[..., None] * br
    gpc = LANE // SSM_GROUP_DIM
    nch = SSM_GROUPS // gpc
    eye = jnp.eye(gpc, dtype=F32)

    def bmat(bb):
        return jnp.einsum("cgnp,gh->cgphn", bb.reshape(nch, gpc, SSM_STATE, SSM_GROUP_DIM),
                          eye).reshape(nch, LANE, gpc * SSM_STATE).astype(BF16)

    def cmat(cc):
        return jnp.einsum("cgpn,gh->cgnhp", cc.reshape(nch, gpc, SSM_GROUP_DIM, SSM_STATE),
                          eye).reshape(nch, gpc * SSM_STATE, LANE).astype(BF16)

    w["ssm_bre"], w["ssm_bim"] = bmat(bb_re), bmat(bb_im)
    w["ssm_cre"], w["ssm_cim"] = cmat(p["ssm_c_re"][l]), cmat(-p["ssm_c_im"][l])
    nrow = 8
    w["ssm_lre"] = jnp.broadcast_to(lb_re.reshape(1, -1), (nrow, SSM_GROUPS * SSM_STATE))
    w["ssm_lim"] = jnp.broadcast_to(lb_im.reshape(1, -1), (nrow, SSM_GROUPS * SSM_STATE))
    w["ssm_d"] = p["ssm_d"][l].reshape(1, D_SSM)
    w["ssm_wglu"] = p["ssm_w_glu"][l].astype(BF16)

    w["w_ups"] = p["w_up_ssm"][l].astype(BF16)
    w["w_upa"] = p["w_up_attn"][l].astype(BF16)
    w["w_o"] = p["w_o"][l].astype(BF16)
    w["lng0"], w["lnb0"] = p["ln_g"][l, 0].reshape(1, D), p["ln_b"][l, 0].reshape(1, D)
    w["lng1"], w["lnb1"] = p["ln_g"][l, 1].reshape(1, D), p["ln_b"][l, 1].reshape(1, D)
    rwg = p["router_w"].reshape(D, N_EXPERT_GROUPS, EXPERTS_PER_GROUP)
    rbg = p["router_bias"].reshape(N_EXPERT_GROUPS, EXPERTS_PER_GROUP)
    w["rw"] = jnp.transpose(rwg, (2, 1, 0)).reshape(N_EXPERTS, D)
    w["rb"] = jnp.transpose(rbg, (1, 0)).reshape(N_EXPERTS, 1)
    w["ws1"] = p["shared_w1"][l].astype(BF16)
    w["ws3"] = p["shared_w3"][l].astype(BF16)
    w["ws2"] = p["shared_w2"][l].astype(BF16)
    return w


def kernel(x, c, positions, w_ada, b_ada, w_in, conv_w, ssm_a_re, ssm_a_im, ssm_b_re, ssm_b_im,
           ssm_c_re, ssm_c_im, ssm_d, ssm_log_dt, ssm_w_glu, q_norm, w_uq, kv_norm, w_uk, w_uv,
           w_up_conv, w_up_ssm, w_up_attn, w_o, ln_g, ln_b, router_w, router_bias,
           exp_w1, exp_w3, exp_w2, shared_w1, shared_w3, shared_w2):
    B, S, D = x.shape
    L = w_in.shape[0]
    T = B * S
    assert B == 8 and S % max(TM_PROJ, T_CHUNK, TQ, FINAL_ROWS) == 0 and T % max(MOE_ROWS, PERM_ROWS) == 0
    assert TM_PROJ == TQ
    p = dict(w_in=w_in, conv_w=conv_w, ssm_a_re=ssm_a_re, ssm_a_im=ssm_a_im, ssm_b_re=ssm_b_re,
             ssm_b_im=ssm_b_im, ssm_c_re=ssm_c_re, ssm_c_im=ssm_c_im, ssm_d=ssm_d,
             ssm_log_dt=ssm_log_dt, ssm_w_glu=ssm_w_glu, q_norm=q_norm, w_uq=w_uq, kv_norm=kv_norm,
             w_uk=w_uk, w_uv=w_uv, w_up_conv=w_up_conv, w_up_ssm=w_up_ssm, w_up_attn=w_up_attn,
             w_o=w_o, ln_g=ln_g, ln_b=ln_b, router_w=router_w, router_bias=router_bias,
             exp_w1=exp_w1, exp_w3=exp_w3, exp_w2=exp_w2, shared_w1=shared_w1,
             shared_w3=shared_w3, shared_w2=shared_w2)
    alpha = (2 * L) ** 0.25
    cos128, sin128 = _rope_tables(positions)
    ada = _ada(c, w_ada, b_ada).reshape(L, B, 6, D)
    x2d = x.reshape(T, D)
    for l in range(L):
        wts = _layer_weights(l, p)
        pc, sgs, sga, u, q, k, v = _inproj(x2d, ada[l], cos128, sin128, wts, B, S)
        ys = _ssm(u.reshape(B, S, D_SSM), wts, B, S).reshape(T, D_SSM)
        ya = _attention(q, k, v, B, S)
        base, hr = _merge(x2d, ada[l], pc, sgs, sga, ys, ya, wts, B, S, alpha)
        gid = hr[:, D + EXPERTS_PER_GROUP].astype(jnp.int32)
        dest, block_g, nvalid = _route_plan(gid, T)
        hs = _permute(hr, dest, nvalid)
        ysort = _moe(hs, block_g, nvalid, exp_w1, exp_w3, exp_w2, l)
        x2d = _final(base, ysort, dest, ada[l], wts["lng1"], wts["lnb1"], B, S)
    return x2d.reshape(B, S, D)
```

```python
import functools
import math

import jax
import jax.numpy as jnp
from jax import lax
from jax.experimental import pallas as pl
from jax.experimental.pallas import tpu as pltpu

F32 = jnp.float32
BF16 = jnp.bfloat16

D_CONV = 512
D_SSM = 512
SSM_GROUP_DIM = 16
SSM_GROUPS = 32
SSM_STATE = 64
MLA_HEADS = 8
QK_NOPE = 64
QK_ROPE = 32
V_HEAD = 64
Q_LORA = 256
KV_LORA = 128
ROPE_THETA = 10000.0
N_EXPERTS = 32
N_EXPERT_GROUPS = 8
EXPERTS_PER_GROUP = 4
D_EXPERT = 256
LN_EPS = 1e-5
RMS_EPS = 1e-6

LANE = 128
HEAD_PAD = 128
VMEM_LIMIT = 56 * 1024 * 1024

TM_PROJ = 512
T_CHUNK = 64
TQ = 512
ATTN_HEADS = 4
DEN_ROWS = 16
MOE_ROWS = 256
PERM_ROWS = 1024
FINAL_ROWS = 512
ROUTE_W = 128


def _cparams(n_axes):
    return pltpu.CompilerParams(dimension_semantics=("arbitrary",) * n_axes,
                                vmem_limit_bytes=VMEM_LIMIT)


def _ln_plain(x):
    mu = jnp.mean(x, axis=-1, keepdims=True)
    xc = x - mu
    var = jnp.mean(xc * xc, axis=-1, keepdims=True)
    return xc * lax.rsqrt(var + LN_EPS)


def _rms(x, g):
    return x * lax.rsqrt(jnp.mean(x * x, axis=-1, keepdims=True) + RMS_EPS) * g


def _bdot(a, b):
    return jnp.dot(a.astype(BF16), b, preferred_element_type=F32)


def _ada_kernel(c_ref, w_ref, b_ref, o_ref):
    c = c_ref[...]
    cond = c * jax.nn.sigmoid(c)
    o_ref[...] = jnp.dot(cond, w_ref[...], precision=lax.Precision.HIGHEST,
                         preferred_element_type=F32) + b_ref[...]


def _ada(c, w_ada, b_ada):
    L, D, D6 = w_ada.shape
    B = c.shape[0]
    nj = D6 // D
    return pl.pallas_call(
        _ada_kernel,
        out_shape=jax.ShapeDtypeStruct((L, B, D6), F32),
        grid=(L, nj),
        in_specs=[pl.BlockSpec((B, D), lambda l, j: (0, 0)),
                  pl.BlockSpec((None, D, D), lambda l, j: (l, 0, j)),
                  pl.BlockSpec((None, 1, D), lambda l, j: (l, 0, j))],
        out_specs=pl.BlockSpec((None, B, D), lambda l, j: (l, 0, j)),
        compiler_params=_cparams(2),
        name="ada",
    )(c, w_ada, b_ada.reshape(L, 1, D6))


def _inproj_kernel(x_ref, ada_ref, cos_ref, sin_ref,
                   w_cv, w_ssm, w_lat, w_g,
                   convw, qn, wq_a, wq_b, kvn, wk, wv, w_upc,
                   pc_ref, sgs_ref, sga_ref, u_ref, q_ref, k_ref, v_ref,
                   carry_ref):
    si = pl.program_id(1)
    D = x_ref.shape[1]
    tm = x_ref.shape[0]

    @pl.when(si == 0)
    def _():
        carry_ref[...] = jnp.zeros_like(carry_ref)

    ada = ada_ref[...]
    h = _ln_plain(x_ref[...]) * (1.0 + ada[1:2, :]) + ada[0:1, :]
    hb = h.astype(BF16)

    pcv = jnp.dot(hb, w_cv[...], preferred_element_type=F32)
    u = pcv[:, D_CONV:2 * D_CONV] * pcv[:, :D_CONV]
    gb = pcv[:, 2 * D_CONV:]
    tail = carry_ref[...]
    row8 = lax.broadcasted_iota(jnp.int32, (8, D_CONV), 0)
    r1 = pltpu.roll(u, 1, 0)
    r2 = pltpu.roll(u, 2, 0)
    t1 = pltpu.roll(tail, 1, 0)
    t2 = pltpu.roll(tail, 2, 0)
    u1 = jnp.concatenate([jnp.where(row8 < 1, t1, r1[:8]), r1[8:]], axis=0)
    u2 = jnp.concatenate([jnp.where(row8 < 2, t2, r2[:8]), r2[8:]], axis=0)
    cw = convw[...]
    yc = gb * (cw[0:1, :] * u2 + cw[1:2, :] * u1 + cw[2:3, :] * u)
    carry_ref[...] = u[tm - 8:, :]

    gates = jnp.dot(hb, w_g[...], preferred_element_type=F32)
    pc_ref[...] = (jax.nn.sigmoid(gates[:, :D]) * _bdot(yc, w_upc[...])).astype(BF16)
    sgs_ref[...] = jax.nn.sigmoid(gates[:, D:2 * D]).astype(BF16)
    sga_ref[...] = jax.nn.sigmoid(gates[:, 2 * D:]).astype(BF16)

    u_ref[...] = jnp.dot(hb, w_ssm[...], preferred_element_type=F32).astype(BF16)

    cos = cos_ref[...]
    sin = sin_ref[...]
    lat = jnp.dot(hb, w_lat[...], preferred_element_type=F32)
    o_kv, o_kr = Q_LORA, Q_LORA + KV_LORA
    cq = _rms(lat[:, :o_kv], qn[...]).astype(BF16)
    qa = jnp.dot(cq, wq_a[...], preferred_element_type=F32)
    qb = jnp.dot(cq, wq_b[...], preferred_element_type=F32)
    ckv = _rms(lat[:, o_kv:o_kr], kvn[...]).astype(BF16)
    kn = jnp.dot(ckv, wk[...], preferred_element_type=F32)
    kr = lat[:, o_kr:o_kr + HEAD_PAD] * cos + lat[:, o_kr + HEAD_PAD:] * sin
    for hd in range(MLA_HEADS):
        sl = slice(hd * HEAD_PAD, (hd + 1) * HEAD_PAD)
        q_ref[:, sl] = (qa[:, sl] * cos + qb[:, sl] * sin).astype(BF16)
        k_ref[:, sl] = (kn[:, sl] + kr).astype(BF16)
    v_ref[...] = lax.dot_general(wv[...], ckv, (((1,), (1,)), ((), ())),
                                 preferred_element_type=F32).astype(BF16)


def _inproj(x2d, ada_l, cos128, sin128, wts, B, S):
    T, D = x2d.shape
    tm = TM_PROJ
    ns = S // tm
    row = lambda b, s: (b * ns + s, 0)
    full = lambda a: pl.BlockSpec(a.shape, lambda b, s: (0,) * a.ndim, pipeline_mode=pl.Buffered(1))
    names = ["w_cv", "w_ssm", "w_lat", "w_g", "convw", "qn",
             "wq_a", "wq_b", "kvn", "wk", "wv", "w_upc"]
    ws = [wts[n] for n in names]
    out_shape = (
        jax.ShapeDtypeStruct((T, D), BF16),
        jax.ShapeDtypeStruct((T, D), BF16),
        jax.ShapeDtypeStruct((T, D), BF16),
        jax.ShapeDtypeStruct((T, D_SSM), BF16),
        jax.ShapeDtypeStruct((T, MLA_HEADS * HEAD_PAD), BF16),
        jax.ShapeDtypeStruct((T, MLA_HEADS * HEAD_PAD), BF16),
        jax.ShapeDtypeStruct((B, ns, MLA_HEADS * V_HEAD, tm), BF16),
    )
    out_specs = (
        pl.BlockSpec((tm, D), row), pl.BlockSpec((tm, D), row), pl.BlockSpec((tm, D), row),
        pl.BlockSpec((tm, D_SSM), row),
        pl.BlockSpec((tm, MLA_HEADS * HEAD_PAD), row),
        pl.BlockSpec((tm, MLA_HEADS * HEAD_PAD), row),
        pl.BlockSpec((None, None, MLA_HEADS * V_HEAD, tm), lambda b, s: (b, s, 0, 0)),
    )
    return pl.pallas_call(
        _inproj_kernel,
        out_shape=out_shape,
        grid=(B, ns),
        in_specs=[pl.BlockSpec((tm, D), row),
                  pl.BlockSpec((None, 6, D), lambda b, s: (b, 0, 0)),
                  pl.BlockSpec((tm, LANE), row), pl.BlockSpec((tm, LANE), row)]
                 + [full(a) for a in ws],
        out_specs=out_specs,
        scratch_shapes=[pltpu.VMEM((8, D_CONV), F32)],
        compiler_params=_cparams(2),
        name="inproj",
    )(x2d, ada_l, cos128, sin128, *ws)


def _ssm_kernel(u_ref, bre_ref, bim_ref, lre_ref, lim_ref, cre_ref, cim_ref, d_ref, wglu_ref, perm_ref,
                o_ref, sre, sim, st_re, st_im, tmaj):
    i = pl.program_id(0)
    batch, nsteps = u_ref.shape[0], u_ref.shape[1]
    rows = batch * nsteps
    nchunk = bre_ref.shape[0]
    cw = bre_ref.shape[2]

    @pl.when(i == 0)
    def _():
        st_re[...] = jnp.zeros_like(st_re)
        st_im[...] = jnp.zeros_like(st_im)

    for b in range(batch):
        tmaj[:, b, :] = u_ref[b].astype(F32)
    uf = tmaj[...].reshape(rows, D_SSM)
    ub = uf.astype(BF16)
    for c in range(nchunk):
        uc = ub[:, c * LANE:(c + 1) * LANE]
        sre[:, c * cw:(c + 1) * cw] = jnp.dot(uc, bre_ref[c], preferred_element_type=F32)
        sim[:, c * cw:(c + 1) * cw] = jnp.dot(uc, bim_ref[c], preferred_element_type=F32)

    def step(t, carry):
        pr, pi = carry
        r = pl.multiple_of(t * batch, batch)
        lre = lre_ref[...]
        lim = lim_ref[...]
        nr = lre * pr - lim * pi + sre[pl.ds(r, batch), :]
        ni = lre * pi + lim * pr + sim[pl.ds(r, batch), :]
        sre[pl.ds(r, batch), :] = nr
        sim[pl.ds(r, batch), :] = ni
        return nr, ni

    fr, fi = lax.fori_loop(0, nsteps, step, (st_re[...], st_im[...]))
    st_re[...] = fr
    st_im[...] = fi

    ys = []
    for c in range(nchunk):
        ys.append(_bdot(sre[:, c * cw:(c + 1) * cw], cre_ref[c])
                  + _bdot(sim[:, c * cw:(c + 1) * cw], cim_ref[c]))
    y = jnp.concatenate(ys, axis=1) + d_ref[...] * uf
    y = 0.5 * y * (1.0 + jnp.tanh(math.sqrt(2.0 / math.pi) * (y + 0.044715 * (y * y * y))))
    z = _bdot(y, wglu_ref[...])
    g = (z[:, :D_SSM] * jax.nn.sigmoid(z[:, D_SSM:])).astype(BF16)
    o_ref[...] = jnp.dot(perm_ref[...], g, preferred_element_type=F32).astype(BF16).reshape(
        batch, nsteps, D_SSM)


def _ssm(u, wts, B, S):
    rows = T_CHUNK * B
    n = S // T_CHUNK
    nstate = SSM_GROUPS * SSM_STATE
    names = ["ssm_bre", "ssm_bim", "ssm_lre", "ssm_lim", "ssm_cre", "ssm_cim", "ssm_d", "ssm_wglu"]
    r = jnp.arange(rows, dtype=jnp.int32)
    src = (r % T_CHUNK) * B + r // T_CHUNK
    perm = (src[:, None] == jnp.arange(rows, dtype=jnp.int32)[None, :]).astype(BF16)
    ws = [wts[k] for k in names] + [perm]
    full = lambda a: pl.BlockSpec(a.shape, lambda i: (0,) * a.ndim, pipeline_mode=pl.Buffered(1))
    return pl.pallas_call(
        _ssm_kernel,
        out_shape=jax.ShapeDtypeStruct((B, S, D_SSM), BF16),
        grid=(n,),
        in_specs=[pl.BlockSpec((B, T_CHUNK, D_SSM), lambda i: (0, i, 0))] + [full(a) for a in ws],
        out_specs=pl.BlockSpec((B, T_CHUNK, D_SSM), lambda i: (0, i, 0)),
        scratch_shapes=[pltpu.VMEM((rows, nstate), F32), pltpu.VMEM((rows, nstate), F32),
                        pltpu.VMEM((B, nstate), F32), pltpu.VMEM((B, nstate), F32),
                        pltpu.VMEM((T_CHUNK, B, D_SSM), F32)],
        compiler_params=_cparams(1),
        name="ssm",
    )(u, *ws)


def _attn_kernel(q_ref, k_ref, vt_ref, o_ref, m_sc, acc_sc, s_sc, p_sc):
    qi = pl.program_id(2)
    tq = q_ref.shape[0]
    nh = m_sc.shape[0]
    neg = -1e30
    m_sc[...] = jnp.full(m_sc.shape, neg, F32)
    acc_sc[...] = jnp.zeros(acc_sc.shape, F32)
    ones = jnp.ones((DEN_ROWS, tq), BF16)

    def block(j, masked):
        off = pl.multiple_of(j * tq, tq)
        for hh in range(nh):
            q = q_ref[:, hh * HEAD_PAD:(hh + 1) * HEAD_PAD]
            kb = k_ref[pl.ds(off, tq), hh * HEAD_PAD:(hh + 1) * HEAD_PAD]
            s_sc[hh] = lax.dot_general(kb, q, (((1,), (1,)), ((), ())),
                                       preferred_element_type=F32)
        scale = []
        for hh in range(nh):
            if masked:
                keyi = lax.broadcasted_iota(jnp.int32, (tq, tq), 0)
                qryi = lax.broadcasted_iota(jnp.int32, (tq, tq), 1)
                s_sc[hh] = jnp.where(keyi <= qryi, s_sc[hh], neg)
            m_prev = m_sc[hh]
            m_new = jnp.maximum(m_prev, jnp.max(s_sc[hh], axis=0, keepdims=True))
            m_sc[hh] = m_new
            p_sc[hh] = jnp.exp2(s_sc[hh] - m_new).astype(BF16)
            scale.append(jnp.exp2(m_prev - m_new))
        for hh in range(nh):
            vte = jnp.concatenate([vt_ref[j, hh * V_HEAD:(hh + 1) * V_HEAD, :], ones], axis=0)
            acc_sc[hh] = scale[hh] * acc_sc[hh] + jnp.dot(vte, p_sc[hh], preferred_element_type=F32)

    def full_block(j, c):
        block(j, False)
        return c

    lax.fori_loop(0, qi, full_block, 0)
    block(qi, True)
    outs = []
    for pr in range(nh // 2):
        halves = []
        for hh in (2 * pr, 2 * pr + 1):
            acc = acc_sc[hh]
            halves.append(acc[:V_HEAD] * (1.0 / acc[V_HEAD:V_HEAD + 1]))
        outs.append(jnp.concatenate(halves, axis=0).T)
    o_ref[...] = jnp.concatenate(outs, axis=1).astype(BF16)


def _attention(q, k, vt, B, S):
    nq = S // TQ
    nh = ATTN_HEADS
    hg = MLA_HEADS // nh
    return pl.pallas_call(
        _attn_kernel,
        out_shape=jax.ShapeDtypeStruct((B * S, MLA_HEADS * V_HEAD), BF16),
        grid=(B, hg, nq),
        in_specs=[pl.BlockSpec((TQ, nh * HEAD_PAD), lambda b, h, i: (b * nq + i, h)),
                  pl.BlockSpec((S, nh * HEAD_PAD), lambda b, h, i: (b, h)),
                  pl.BlockSpec((None, nq, nh * V_HEAD, TQ), lambda b, h, i: (b, 0, h, 0))],
        out_specs=pl.BlockSpec((TQ, nh * V_HEAD), lambda b, h, i: (b * nq + i, h)),
        scratch_shapes=[pltpu.VMEM((nh, 1, TQ), F32),
                        pltpu.VMEM((nh, V_HEAD + DEN_ROWS, TQ), F32),
                        pltpu.VMEM((nh, TQ, TQ), F32), pltpu.VMEM((nh, TQ, TQ), BF16)],
        compiler_params=_cparams(3),
        name="attention",
    )(q, k, vt)


def _merge_kernel(x_ref, ada_ref, pc_ref, sgs_ref, sga_ref, ys_ref, ya_ref,
                  w_ups, w_upa, w_o, lng, lnb, rw, rb, ws1, ws3, ws2,
                  base_ref, hr_ref, rt_ref, *, alpha):
    D = x_ref.shape[1]
    tm = x_ref.shape[0]
    ada = ada_ref[...]
    merged = (pc_ref[...].astype(F32)
              + sgs_ref[...].astype(F32) * jnp.dot(ys_ref[...], w_ups[...], preferred_element_type=F32)
              + sga_ref[...].astype(F32) * jnp.dot(ya_ref[...], w_upa[...], preferred_element_type=F32))
    y = _bdot(merged, w_o[...])
    x1 = _ln_plain(alpha * x_ref[...] + ada[2:3, :] * y) * lng[...] + lnb[...]
    h2 = _ln_plain(x1) * (1.0 + ada[4:5, :]) + ada[3:4, :]
    hr_ref[:, :D] = h2

    G = N_EXPERT_GROUPS
    logits = lax.dot_general(rw[...], h2, (((1,), (1,)), ((), ())),
                             precision=lax.Precision.HIGHEST, preferred_element_type=F32)
    sc0 = jax.nn.sigmoid(logits)
    sl0 = sc0 + rb[...]
    sc = [sc0[j * G:(j + 1) * G] for j in range(EXPERTS_PER_GROUP)]
    sl = [sl0[j * G:(j + 1) * G] for j in range(EXPERTS_PER_GROUP)]
    hi1, lo1 = jnp.maximum(sl[0], sl[1]), jnp.minimum(sl[0], sl[1])
    hi2, lo2 = jnp.maximum(sl[2], sl[3]), jnp.minimum(sl[2], sl[3])
    gscore = jnp.maximum(hi1, hi2) + jnp.maximum(jnp.minimum(hi1, hi2), jnp.maximum(lo1, lo2))
    sub = lax.broadcasted_iota(jnp.int32, (G, tm), 0)
    gmax = jnp.max(gscore, axis=0, keepdims=True)
    gidx = jnp.min(jnp.where(gscore == gmax, sub, G), axis=0, keepdims=True)
    pick = sub == gidx
    vs = [jnp.sum(jnp.where(pick, a, 0.0), axis=0, keepdims=True) for a in sl]
    ss = [jnp.sum(jnp.where(pick, a, 0.0), axis=0, keepdims=True) for a in sc]
    chosen = []
    for j in range(EXPERTS_PER_GROUP):
        cnt = jnp.zeros((1, tm), jnp.int32)
        for k in range(EXPERTS_PER_GROUP):
            if k == j:
                continue
            beats = (vs[k] > vs[j]) | ((vs[k] == vs[j]) & (k < j))
            cnt = cnt + beats.astype(jnp.int32)
        chosen.append(cnt < 2)
    wsel = [jnp.where(chosen[j], ss[j], 0.0) for j in range(EXPERTS_PER_GROUP)]
    wsum = wsel[0] + wsel[1] + wsel[2] + wsel[3]
    rows = [wsel[j] / wsum for j in range(EXPERTS_PER_GROUP)] + [gidx.astype(F32)]
    rows += [jnp.zeros((1, tm), F32)] * (8 - len(rows))
    route_t = jnp.concatenate(rows, axis=0)
    rt_ref[...] = route_t
    eye = (lax.broadcasted_iota(jnp.int32, (8, ROUTE_W), 0)
           == lax.broadcasted_iota(jnp.int32, (8, ROUTE_W), 1)).astype(F32)
    hr_ref[:, D:] = lax.dot_general(route_t, eye, (((0,), (0,)), ((), ())),
                                    precision=lax.Precision.HIGHEST, preferred_element_type=F32)

    h2b = h2.astype(BF16)
    a1 = jnp.dot(h2b, ws1[...], preferred_element_type=F32)
    a3 = jnp.dot(h2b, ws3[...], preferred_element_type=F32)
    ysh = _bdot(a1 * jax.nn.sigmoid(a1) * a3, ws2[...])
    base_ref[...] = alpha * x1 + ada[5:6, :] * ysh


def _merge(x2d, ada_l, pc, sgs, sga, ys, ya, wts, B, S, alpha):
    T, D = x2d.shape
    tm = TM_PROJ
    ns = S // tm
    row = lambda b, s: (b * ns + s, 0)
    names = ["w_ups", "w_upa", "w_o", "lng0", "lnb0", "rw", "rb", "ws1", "ws3", "ws2"]
    ws = [wts[n] for n in names]
    full = lambda a: pl.BlockSpec(a.shape, lambda b, s: (0,) * a.ndim, pipeline_mode=pl.Buffered(1))
    return pl.pallas_call(
        functools.partial(_merge_kernel, alpha=alpha),
        out_shape=(jax.ShapeDtypeStruct((T, D), F32),
                   jax.ShapeDtypeStruct((T, D + ROUTE_W), F32),
                   jax.ShapeDtypeStruct((8, T), F32)),
        grid=(B, ns),
        in_specs=[pl.BlockSpec((tm, D), row),
                  pl.BlockSpec((None, 6, D), lambda b, s: (b, 0, 0)),
                  pl.BlockSpec((tm, D), row), pl.BlockSpec((tm, D), row), pl.BlockSpec((tm, D), row),
                  pl.BlockSpec((tm, D_SSM), row),
                  pl.BlockSpec((tm, MLA_HEADS * V_HEAD), row)]
                 + [full(a) for a in ws],
        out_specs=(pl.BlockSpec((tm, D), row), pl.BlockSpec((tm, D + ROUTE_W), row),
                   pl.BlockSpec((8, tm), lambda b, s: (0, b * ns + s))),
        compiler_params=_cparams(2),
        name="merge",
    )(x2d, ada_l, pc, sgs, sga, ys, ya, *ws)


def _route_plan(gid, T):
    R = MOE_ROWS
    G = N_EXPERT_GROUPS
    onehot = (gid[:, None] == jnp.arange(G, dtype=jnp.int32)[None, :]).astype(jnp.int32)
    csum = jnp.cumsum(onehot, axis=0)
    counts = csum[-1]
    rank = jnp.sum(onehot * csum, axis=1) - 1
    padded = (counts + R - 1) // R * R
    pad_end = jnp.cumsum(padded)
    pad_start = pad_end - padded
    dest = jnp.sum(onehot * pad_start[None, :], axis=1) + rank
    nb = T // R + G
    starts = jnp.arange(nb, dtype=jnp.int32) * R
    block_g = jnp.minimum(jnp.sum((starts[:, None] >= pad_end[None, :]).astype(jnp.int32), axis=1), G - 1)
    gsel = (block_g[:, None] == jnp.arange(G, dtype=jnp.int32)[None, :]).astype(jnp.int32)
    group_end = jnp.sum(gsel * (pad_start + counts)[None, :], axis=1)
    nvalid = jnp.clip(group_end - starts, 0, R)
    return dest.astype(jnp.int32), block_g.astype(jnp.int32), nvalid.astype(jnp.int32)


def _permute_kernel(dest_ref, nvalid_ref, hr_ref, hs_hbm, zbuf, sem, zsem):
    i = pl.program_id(0)
    tp = hr_ref.shape[0]
    R = zbuf.shape[0]
    nb = nvalid_ref.shape[0]

    @pl.when(i == 0)
    def _():
        zbuf[...] = jnp.zeros(zbuf.shape, F32)

        def fill(b, n):
            partial = nvalid_ref[b] < R

            @pl.when(partial)
            def _():
                pltpu.make_async_copy(zbuf, hs_hbm.at[pl.ds(b * R, R)], zsem).start()

            return n + partial.astype(jnp.int32)

        nfill = lax.fori_loop(0, nb, fill, 0)

        def drain(k, c):
            pltpu.make_async_copy(zbuf, hs_hbm.at[pl.ds(0, R)], zsem).wait()
            return c

        lax.fori_loop(0, nfill, drain, 0)

    def body(g, c):
        r0 = pl.multiple_of(g * 8, 8)
        for k in range(8):
            d = dest_ref[i * tp + r0 + k]
            pltpu.make_async_copy(hr_ref.at[pl.ds(r0 + k, 1)], hs_hbm.at[pl.ds(d, 1)], sem).start()
        return c

    lax.fori_loop(0, tp // 8, body, 0)
    pltpu.make_async_copy(hr_ref, hs_hbm.at[pl.ds(0, tp)], sem).wait()


def _permute(hr, dest, nvalid):
    T, W = hr.shape
    R = MOE_ROWS
    tp = PERM_ROWS
    nb = nvalid.shape[0]
    return pl.pallas_call(
        _permute_kernel,
        out_shape=jax.ShapeDtypeStruct((nb * R, W), F32),
        grid_spec=pltpu.PrefetchScalarGridSpec(
            num_scalar_prefetch=2,
            grid=(T // tp,),
            in_specs=[pl.BlockSpec((tp, W), lambda i, d, n: (i, 0))],
            out_specs=pl.BlockSpec(memory_space=pl.ANY),
            scratch_shapes=[pltpu.VMEM((R, W), F32), pltpu.SemaphoreType.DMA, pltpu.SemaphoreType.DMA]),
        compiler_params=_cparams(1),
        name="permute",
    )(dest, nvalid, hr)


def _moe_kernel(bg_ref, nvalid_ref, hs_ref, w1_ref, w3_ref, w2_ref, y_ref, w1b, w3b, w2b):
    i = pl.program_id(0)
    R, D = y_ref.shape

    @pl.when((i == 0) | (bg_ref[i] != bg_ref[jnp.maximum(i - 1, 0)]))
    def _():
        w1b[...] = w1_ref[...].astype(BF16)
        w3b[...] = w3_ref[...].astype(BF16)
        w2b[...] = w2_ref[...].astype(BF16)

    @pl.when(nvalid_ref[i] > 0)
    def _():
        xb = hs_ref[:, :D].astype(BF16)
        acc = jnp.zeros((R, D), F32)
        for j in range(EXPERTS_PER_GROUP):
            a1 = jnp.dot(xb, w1b[j], preferred_element_type=F32)
            a3 = jnp.dot(xb, w3b[j], preferred_element_type=F32)
            hj = a1 * jax.nn.sigmoid(a1) * a3 * hs_ref[:, D + j:D + j + 1]
            acc = acc + _bdot(hj, w2b[j])
        y_ref[...] = acc

    @pl.when(nvalid_ref[i] == 0)
    def _():
        y_ref[...] = jnp.zeros((R, D), F32)


def _moe(hs, block_g, nvalid, w1, w3, w2, layer):
    D = w1.shape[2]
    W = hs.shape[1]
    R = MOE_ROWS
    nb = nvalid.shape[0]
    wspec_in = pl.BlockSpec((None, EXPERTS_PER_GROUP, D, D_EXPERT), lambda i, bg, nv: (layer, bg[i], 0, 0))
    wspec_out = pl.BlockSpec((None, EXPERTS_PER_GROUP, D_EXPERT, D), lambda i, bg, nv: (layer, bg[i], 0, 0))
    return pl.pallas_call(
        _moe_kernel,
        out_shape=jax.ShapeDtypeStruct((nb * R, D), F32),
        grid_spec=pltpu.PrefetchScalarGridSpec(
            num_scalar_prefetch=2,
            grid=(nb,),
            in_specs=[pl.BlockSpec((R, W), lambda i, bg, nv: (i, 0)), wspec_in, wspec_in, wspec_out],
            out_specs=pl.BlockSpec((R, D), lambda i, bg, nv: (i, 0)),
            scratch_shapes=[pltpu.VMEM((EXPERTS_PER_GROUP, D, D_EXPERT), BF16),
                            pltpu.VMEM((EXPERTS_PER_GROUP, D, D_EXPERT), BF16),
                            pltpu.VMEM((EXPERTS_PER_GROUP, D_EXPERT, D), BF16)]),
        compiler_params=_cparams(1),
        name="moe",
    )(block_g, nvalid, hs, w1, w3, w2)


def _final_kernel(dest_ref, base_ref, ys_hbm, ada_ref, lng, lnb, o_ref, ybuf, sem):
    i = pl.program_id(0)
    n = pl.num_programs(0)
    tf = base_ref.shape[0]
    slot = i % 2

    def start_gather(step, s):
        def body(r, c):
            d = dest_ref[step * tf + r]
            pltpu.make_async_copy(ys_hbm.at[pl.ds(d, 1)], ybuf.at[s, pl.ds(r, 1)], sem.at[s]).start()
            return c

        lax.fori_loop(0, tf, body, 0, unroll=8)

    @pl.when(i == 0)
    def _():
        start_gather(0, 0)

    @pl.when(i + 1 < n)
    def _():
        start_gather(i + 1, 1 - slot)

    pltpu.make_async_copy(ys_hbm.at[pl.ds(0, tf)], ybuf.at[slot], sem.at[slot]).wait()
    ada = ada_ref[...]
    o_ref[...] = _ln_plain(base_ref[...] + ada[5:6, :] * ybuf[slot]) * lng[...] + lnb[...]


def _final(base, ys, dest, ada_l, lng, lnb, B, S):
    T, D = base.shape
    tf = FINAL_ROWS
    per_batch = S // tf
    return pl.pallas_call(
        _final_kernel,
        out_shape=jax.ShapeDtypeStruct((T, D), F32),
        grid_spec=pltpu.PrefetchScalarGridSpec(
            num_scalar_prefetch=1,
            grid=(T // tf,),
            in_specs=[pl.BlockSpec((tf, D), lambda i, d: (i, 0)),
                      pl.BlockSpec(memory_space=pl.ANY),
                      pl.BlockSpec((None, 6, D), lambda i, d: (i // per_batch, 0, 0)),
                      pl.BlockSpec((1, D), lambda i, d: (0, 0)), pl.BlockSpec((1, D), lambda i, d: (0, 0))],
            out_specs=pl.BlockSpec((tf, D), lambda i, d: (i, 0)),
            scratch_shapes=[pltpu.VMEM((2, tf, D), F32), pltpu.SemaphoreType.DMA((2,))]),
        compiler_params=_cparams(1),
        name="final_ln",
    )(dest, base, ys, ada_l, lng, lnb)


def _rope_tables(positions):
    inv_freq = ROPE_THETA ** (-jnp.arange(0, QK_ROPE, 2, dtype=F32) / QK_ROPE)
    ang = positions.astype(F32)[..., None] * inv_freq
    cos, sin = jnp.cos(ang), jnp.sin(ang)
    ones = jnp.ones(cos.shape[:-1] + (QK_NOPE,), F32)
    zpad = jnp.zeros(cos.shape[:-1] + (HEAD_PAD - QK_NOPE - QK_ROPE,), F32)
    cos128 = jnp.concatenate([ones, cos, cos, zpad], axis=-1)
    sin128 = jnp.concatenate([0.0 * ones, -sin, sin, zpad], axis=-1)
    T = positions.shape[0] * positions.shape[1]
    return cos128.reshape(T, HEAD_PAD), sin128.reshape(T, HEAD_PAD)


def _swap_halves(w):
    half = w.shape[-1] // 2
    return jnp.concatenate([w[..., half:], w[..., :half]], axis=-1)


def _layer_weights(l, p):
    D = p["w_in"].shape[1]
    w_in = p["w_in"][l]
    o = 0
    seg = {}
    for name, width in (("cv", 3 * D_CONV), ("ssm", D_SSM), ("cq", Q_LORA), ("ckv", KV_LORA),
                        ("kr", QK_ROPE), ("g", 3 * D)):
        seg[name] = w_in[:, o:o + width]
        o += width
    pad_r = HEAD_PAD - QK_NOPE - QK_ROPE
    zl = jnp.zeros((D, QK_NOPE), F32)
    zr = jnp.zeros((D, pad_r), F32)
    w = {
        "w_cv": seg["cv"].astype(BF16), "w_ssm": seg["ssm"].astype(BF16),
        "w_lat": jnp.concatenate([seg["cq"], seg["ckv"], zl, seg["kr"], zr,
                                  zl, _swap_halves(seg["kr"]), zr], axis=1).astype(BF16),
        "w_g": seg["g"].astype(BF16),
        "convw": p["conv_w"][l],
        "qn": p["q_norm"][l].reshape(1, Q_LORA), "kvn": p["kv_norm"][l].reshape(1, KV_LORA),
        "w_upc": p["w_up_conv"][l].astype(BF16),
    }
    scale = (QK_NOPE + QK_ROPE) ** -0.5 * math.log2(math.e)
    wq = p["w_uq"][l].reshape(Q_LORA, MLA_HEADS, QK_NOPE + QK_ROPE) * scale
    zq = jnp.zeros((Q_LORA, MLA_HEADS, pad_r), F32)
    zn = jnp.zeros((Q_LORA, MLA_HEADS, QK_NOPE), F32)
    w["wq_a"] = jnp.concatenate([wq, zq], axis=-1).reshape(Q_LORA, -1).astype(BF16)
    w["wq_b"] = jnp.concatenate([zn, _swap_halves(wq[..., QK_NOPE:]), zq],
                                axis=-1).reshape(Q_LORA, -1).astype(BF16)
    wk = p["w_uk"][l].reshape(KV_LORA, MLA_HEADS, QK_NOPE)
    zk = jnp.zeros((KV_LORA, MLA_HEADS, HEAD_PAD - QK_NOPE), F32)
    w["wk"] = jnp.concatenate([wk, zk], axis=-1).reshape(KV_LORA, -1).astype(BF16)
    w["wv"] = p["w_uv"][l].T.astype(BF16)

    ar, ai = p["ssm_a_re"][l], p["ssm_a_im"][l]
    dt = jnp.exp(p["ssm_log_dt"][l])[:, None]
    mag = jnp.exp(ar * dt)
    lb_re, lb_im = mag * jnp.cos(ai * dt), mag * jnp.sin(ai * dt)
    den = ar * ar + ai * ai
    nr, ni = lb_re - 1.0, lb_im
    f_re, f_im = (nr * ar + ni * ai) / den, (ni * ar - nr * ai) / den
    br, bi = p["ssm_b_re"][l], p["ssm_b_im"][l]
    bb_re = f_re[..., None] * br - f_im[..., None] * bi
    bb_im = f_re[..., None] * bi + f_im[..., None] * br
    gpc = LANE // SSM_GROUP_DIM
    nch = SSM_GROUPS // gpc
    eye = jnp.eye(gpc, dtype=F32)

    def bmat(bb):
        return jnp.einsum("cgnp,gh->cgphn", bb.reshape(nch, gpc, SSM_STATE, SSM_GROUP_DIM),
                          eye).reshape(nch, LANE, gpc * SSM_STATE).astype(BF16)

    def cmat(cc):
        return jnp.einsum("cgpn,gh->cgnhp", cc.reshape(nch, gpc, SSM_GROUP_DIM, SSM_STATE),
                          eye).reshape(nch, gpc * SSM_STATE, LANE).astype(BF16)

    w["ssm_bre"], w["ssm_bim"] = bmat(bb_re), bmat(bb_im)
    w["ssm_cre"], w["ssm_cim"] = cmat(p["ssm_c_re"][l]), cmat(-p["ssm_c_im"][l])
    nrow = 8
    w["ssm_lre"] = jnp.broadcast_to(lb_re.reshape(1, -1), (nrow, SSM_GROUPS * SSM_STATE))
    w["ssm_lim"] = jnp.broadcast_to(lb_im.reshape(1, -1), (nrow, SSM_GROUPS * SSM_STATE))
    w["ssm_d"] = p["ssm_d"][l].reshape(1, D_SSM)
    w["ssm_wglu"] = p["ssm_w_glu"][l].astype(BF16)

    w["w_ups"] = p["w_up_ssm"][l].astype(BF16)
    w["w_upa"] = p["w_up_attn"][l].astype(BF16)
    w["w_o"] = p["w_o"][l].astype(BF16)
    w["lng0"], w["lnb0"] = p["ln_g"][l, 0].reshape(1, D), p["ln_b"][l, 0].reshape(1, D)
    w["lng1"], w["lnb1"] = p["ln_g"][l, 1].reshape(1, D), p["ln_b"][l, 1].reshape(1, D)
    rwg = p["router_w"].reshape(D, N_EXPERT_GROUPS, EXPERTS_PER_GROUP)
    rbg = p["router_bias"].reshape(N_EXPERT_GROUPS, EXPERTS_PER_GROUP)
    w["rw"] = jnp.transpose(rwg, (2, 1, 0)).reshape(N_EXPERTS, D)
    w["rb"] = jnp.transpose(rbg, (1, 0)).reshape(N_EXPERTS, 1)
    w["ws1"] = p["shared_w1"][l].astype(BF16)
    w["ws3"] = p["shared_w3"][l].astype(BF16)
    w["ws2"] = p["shared_w2"][l].astype(BF16)
    return w


def kernel(x, c, positions, w_ada, b_ada, w_in, conv_w, ssm_a_re, ssm_a_im, ssm_b_re, ssm_b_im,
           ssm_c_re, ssm_c_im, ssm_d, ssm_log_dt, ssm_w_glu, q_norm, w_uq, kv_norm, w_uk, w_uv,
           w_up_conv, w_up_ssm, w_up_attn, w_o, ln_g, ln_b, router_w, router_bias,
           exp_w1, exp_w3, exp_w2, shared_w1, shared_w3, shared_w2):
    B, S, D = x.shape
    L = w_in.shape[0]
    T = B * S
    assert B == 8 and S % max(TM_PROJ, T_CHUNK, TQ, FINAL_ROWS) == 0 and T % max(MOE_ROWS, PERM_ROWS) == 0
    assert TM_PROJ == TQ
    p = dict(w_in=w_in, conv_w=conv_w, ssm_a_re=ssm_a_re, ssm_a_im=ssm_a_im, ssm_b_re=ssm_b_re,
             ssm_b_im=ssm_b_im, ssm_c_re=ssm_c_re, ssm_c_im=ssm_c_im, ssm_d=ssm_d,
             ssm_log_dt=ssm_log_dt, ssm_w_glu=ssm_w_glu, q_norm=q_norm, w_uq=w_uq, kv_norm=kv_norm,
             w_uk=w_uk, w_uv=w_uv, w_up_conv=w_up_conv, w_up_ssm=w_up_ssm, w_up_attn=w_up_attn,
             w_o=w_o, ln_g=ln_g, ln_b=ln_b, router_w=router_w, router_bias=router_bias,
             exp_w1=exp_w1, exp_w3=exp_w3, exp_w2=exp_w2, shared_w1=shared_w1,
             shared_w3=shared_w3, shared_w2=shared_w2)
    alpha = (2 * L) ** 0.25
    cos128, sin128 = _rope_tables(positions)
    ada = _ada(c, w_ada, b_ada).reshape(L, B, 6, D)
    x2d = x.reshape(T, D)
    for l in range(L):
        wts = _layer_weights(l, p)
        pc, sgs, sga, u, q, k, v = _inproj(x2d, ada[l], cos128, sin128, wts, B, S)
        ys = _ssm(u.reshape(B, S, D_SSM), wts, B, S).reshape(T, D_SSM)
        ya = _attention(q, k, v, B, S)
        base, hr, rt = _merge(x2d, ada[l], pc, sgs, sga, ys, ya, wts, B, S, alpha)
        gid = rt[EXPERTS_PER_GROUP].astype(jnp.int32)
        dest, block_g, nvalid = _route_plan(gid, T)
        hs = _permute(hr, dest, nvalid)
        ysort = _moe(hs, block_g, nvalid, exp_w1, exp_w3, exp_w2, l)
        x2d = _final(base, ysort, dest, ada[l], wts["lng1"], wts["lnb1"], B, S)
    return x2d.reshape(B, S, D)
```

```python
import functools
import math

import jax
import jax.numpy as jnp
from jax import lax
from jax.experimental import pallas as pl
from jax.experimental.pallas import tpu as pltpu

F32 = jnp.float32
BF16 = jnp.bfloat16

D_CONV = 512
D_SSM = 512
SSM_GROUP_DIM = 16
SSM_GROUPS = 32
SSM_STATE = 64
MLA_HEADS = 8
QK_NOPE = 64
QK_ROPE = 32
V_HEAD = 64
Q_LORA = 256
KV_LORA = 128
ROPE_THETA = 10000.0
N_EXPERTS = 32
N_EXPERT_GROUPS = 8
EXPERTS_PER_GROUP = 4
D_EXPERT = 256
LN_EPS = 1e-5
RMS_EPS = 1e-6

LANE = 128
HEAD_PAD = 128
VMEM_LIMIT = 56 * 1024 * 1024

TM_PROJ = 512
T_CHUNK = 64
TQ = 512
ATTN_HEADS = 4
DEN_ROWS = 16
MOE_ROWS = 256
PERM_ROWS = 1024
FINAL_ROWS = 512
ROUTE_W = 128


def _cparams(n_axes):
    return pltpu.CompilerParams(dimension_semantics=("arbitrary",) * n_axes,
                                vmem_limit_bytes=VMEM_LIMIT)


def _ln_plain(x):
    mu = jnp.mean(x, axis=-1, keepdims=True)
    xc = x - mu
    var = jnp.mean(xc * xc, axis=-1, keepdims=True)
    return xc * lax.rsqrt(var + LN_EPS)


def _rms(x, g):
    return x * lax.rsqrt(jnp.mean(x * x, axis=-1, keepdims=True) + RMS_EPS) * g


def _bdot(a, b):
    return jnp.dot(a.astype(BF16), b, preferred_element_type=F32)


def _ada_kernel(c_ref, w_ref, b_ref, o_ref):
    c = c_ref[...]
    cond = c * jax.nn.sigmoid(c)
    o_ref[...] = jnp.dot(cond, w_ref[...], precision=lax.Precision.HIGHEST,
                         preferred_element_type=F32) + b_ref[...]


def _ada(c, w_ada, b_ada):
    L, D, D6 = w_ada.shape
    B = c.shape[0]
    nj = D6 // D
    return pl.pallas_call(
        _ada_kernel,
        out_shape=jax.ShapeDtypeStruct((L, B, D6), F32),
        grid=(L, nj),
        in_specs=[pl.BlockSpec((B, D), lambda l, j: (0, 0)),
                  pl.BlockSpec((None, D, D), lambda l, j: (l, 0, j)),
                  pl.BlockSpec((None, 1, D), lambda l, j: (l, 0, j))],
        out_specs=pl.BlockSpec((None, B, D), lambda l, j: (l, 0, j)),
        compiler_params=_cparams(2),
        name="ada",
    )(c, w_ada, b_ada.reshape(L, 1, D6))


def _inproj_kernel(x_ref, ada_ref, cos_ref, sin_ref,
                   w_cv, w_ssm, w_lat, w_g,
                   convw, qn, wq_a, wq_b, kvn, wk, wv, w_upc,
                   pc_ref, sgs_ref, sga_ref, u_ref, q_ref, k_ref, v_ref,
                   carry_ref):
    si = pl.program_id(1)
    D = x_ref.shape[1]
    tm = x_ref.shape[0]

    @pl.when(si == 0)
    def _():
        carry_ref[...] = jnp.zeros_like(carry_ref)

    ada = ada_ref[...]
    h = _ln_plain(x_ref[...]) * (1.0 + ada[1:2, :]) + ada[0:1, :]
    hb = h.astype(BF16)

    pcv = jnp.dot(hb, w_cv[...], preferred_element_type=F32)
    u = pcv[:, D_CONV:2 * D_CONV] * pcv[:, :D_CONV]
    gb = pcv[:, 2 * D_CONV:]
    tail = carry_ref[...]
    row8 = lax.broadcasted_iota(jnp.int32, (8, D_CONV), 0)
    r1 = pltpu.roll(u, 1, 0)
    r2 = pltpu.roll(u, 2, 0)
    t1 = pltpu.roll(tail, 1, 0)
    t2 = pltpu.roll(tail, 2, 0)
    u1 = jnp.concatenate([jnp.where(row8 < 1, t1, r1[:8]), r1[8:]], axis=0)
    u2 = jnp.concatenate([jnp.where(row8 < 2, t2, r2[:8]), r2[8:]], axis=0)
    cw = convw[...]
    yc = gb * (cw[0:1, :] * u2 + cw[1:2, :] * u1 + cw[2:3, :] * u)
    carry_ref[...] = u[tm - 8:, :]

    gates = jnp.dot(hb, w_g[...], preferred_element_type=F32)
    pc_ref[...] = (jax.nn.sigmoid(gates[:, :D]) * _bdot(yc, w_upc[...])).astype(BF16)
    sgs_ref[...] = jax.nn.sigmoid(gates[:, D:2 * D]).astype(BF16)
    sga_ref[...] = jax.nn.sigmoid(gates[:, 2 * D:]).astype(BF16)

    u_ref[...] = jnp.dot(hb, w_ssm[...], preferred_element_type=F32).astype(BF16)

    cos = cos_ref[...]
    sin = sin_ref[...]
    lat = jnp.dot(hb, w_lat[...], preferred_element_type=F32)
    o_kv, o_kr = Q_LORA, Q_LORA + KV_LORA
    cq = _rms(lat[:, :o_kv], qn[...]).astype(BF16)
    qa = jnp.dot(cq, wq_a[...], preferred_element_type=F32)
    qb = jnp.dot(cq, wq_b[...], preferred_element_type=F32)
    ckv = _rms(lat[:, o_kv:o_kr], kvn[...]).astype(BF16)
    kn = jnp.dot(ckv, wk[...], preferred_element_type=F32)
    kr = lat[:, o_kr:o_kr + HEAD_PAD] * cos + lat[:, o_kr + HEAD_PAD:] * sin
    for hd in range(MLA_HEADS):
        sl = slice(hd * HEAD_PAD, (hd + 1) * HEAD_PAD)
        q_ref[:, sl] = (qa[:, sl] * cos + qb[:, sl] * sin).astype(BF16)
        k_ref[:, sl] = (kn[:, sl] + kr).astype(BF16)
    v_ref[...] = lax.dot_general(wv[...], ckv, (((1,), (1,)), ((), ())),
                                 preferred_element_type=F32).astype(BF16)


def _inproj(x2d, ada_l, cos128, sin128, wts, B, S):
    T, D = x2d.shape
    tm = TM_PROJ
    ns = S // tm
    row = lambda b, s: (b * ns + s, 0)
    full = lambda a: pl.BlockSpec(a.shape, lambda b, s: (0,) * a.ndim, pipeline_mode=pl.Buffered(1))
    names = ["w_cv", "w_ssm", "w_lat", "w_g", "convw", "qn",
             "wq_a", "wq_b", "kvn", "wk", "wv", "w_upc"]
    ws = [wts[n] for n in names]
    out_shape = (
        jax.ShapeDtypeStruct((T, D), BF16),
        jax.ShapeDtypeStruct((T, D), BF16),
        jax.ShapeDtypeStruct((T, D), BF16),
        jax.ShapeDtypeStruct((T, D_SSM), BF16),
        jax.ShapeDtypeStruct((T, MLA_HEADS * HEAD_PAD), BF16),
        jax.ShapeDtypeStruct((T, MLA_HEADS * HEAD_PAD), BF16),
        jax.ShapeDtypeStruct((B, ns, MLA_HEADS * V_HEAD, tm), BF16),
    )
    out_specs = (
        pl.BlockSpec((tm, D), row), pl.BlockSpec((tm, D), row), pl.BlockSpec((tm, D), row),
        pl.BlockSpec((tm, D_SSM), row),
        pl.BlockSpec((tm, MLA_HEADS * HEAD_PAD), row),
        pl.BlockSpec((tm, MLA_HEADS * HEAD_PAD), row),
        pl.BlockSpec((None, None, MLA_HEADS * V_HEAD, tm), lambda b, s: (b, s, 0, 0)),
    )
    return pl.pallas_call(
        _inproj_kernel,
        out_shape=out_shape,
        grid=(B, ns),
        in_specs=[pl.BlockSpec((tm, D), row),
                  pl.BlockSpec((None, 6, D), lambda b, s: (b, 0, 0)),
                  pl.BlockSpec((tm, LANE), row), pl.BlockSpec((tm, LANE), row)]
                 + [full(a) for a in ws],
        out_specs=out_specs,
        scratch_shapes=[pltpu.VMEM((8, D_CONV), F32)],
        compiler_params=_cparams(2),
        name="inproj",
    )(x2d, ada_l, cos128, sin128, *ws)


def _ssm_kernel(u_ref, bre_ref, bim_ref, lre_ref, lim_ref, cre_ref, cim_ref, d_ref, wglu_ref, perm_ref,
                o_ref, sre, sim, st_re, st_im, tmaj):
    i = pl.program_id(0)
    batch, nsteps = u_ref.shape[0], u_ref.shape[1]
    rows = batch * nsteps
    nchunk = bre_ref.shape[0]
    cw = bre_ref.shape[2]

    @pl.when(i == 0)
    def _():
        st_re[...] = jnp.zeros_like(st_re)
        st_im[...] = jnp.zeros_like(st_im)

    for b in range(batch):
        tmaj[:, b, :] = u_ref[b].astype(F32)
    uf = tmaj[...].reshape(rows, D_SSM)
    ub = uf.astype(BF16)
    for c in range(nchunk):
        uc = ub[:, c * LANE:(c + 1) * LANE]
        sre[:, c * cw:(c + 1) * cw] = jnp.dot(uc, bre_ref[c], preferred_element_type=F32)
        sim[:, c * cw:(c + 1) * cw] = jnp.dot(uc, bim_ref[c], preferred_element_type=F32)

    def step(t, carry):
        pr, pi = carry
        r = pl.multiple_of(t * batch, batch)
        lre = lre_ref[...]
        lim = lim_ref[...]
        nr = lre * pr - lim * pi + sre[pl.ds(r, batch), :]
        ni = lre * pi + lim * pr + sim[pl.ds(r, batch), :]
        sre[pl.ds(r, batch), :] = nr
        sim[pl.ds(r, batch), :] = ni
        return nr, ni

    fr, fi = lax.fori_loop(0, nsteps, step, (st_re[...], st_im[...]))
    st_re[...] = fr
    st_im[...] = fi

    ys = []
    for c in range(nchunk):
        ys.append(_bdot(sre[:, c * cw:(c + 1) * cw], cre_ref[c])
                  + _bdot(sim[:, c * cw:(c + 1) * cw], cim_ref[c]))
    y = jnp.concatenate(ys, axis=1) + d_ref[...] * uf
    y = 0.5 * y * (1.0 + jnp.tanh(math.sqrt(2.0 / math.pi) * (y + 0.044715 * (y * y * y))))
    z = _bdot(y, wglu_ref[...])
    g = (z[:, :D_SSM] * jax.nn.sigmoid(z[:, D_SSM:])).astype(BF16)
    o_ref[...] = jnp.dot(perm_ref[...], g, preferred_element_type=F32).astype(BF16).reshape(
        batch, nsteps, D_SSM)


def _ssm(u, wts, B, S):
    rows = T_CHUNK * B
    n = S // T_CHUNK
    nstate = SSM_GROUPS * SSM_STATE
    names = ["ssm_bre", "ssm_bim", "ssm_lre", "ssm_lim", "ssm_cre", "ssm_cim", "ssm_d", "ssm_wglu"]
    r = jnp.arange(rows, dtype=jnp.int32)
    src = (r % T_CHUNK) * B + r // T_CHUNK
    perm = (src[:, None] == jnp.arange(rows, dtype=jnp.int32)[None, :]).astype(BF16)
    ws = [wts[k] for k in names] + [perm]
    full = lambda a: pl.BlockSpec(a.shape, lambda i: (0,) * a.ndim, pipeline_mode=pl.Buffered(1))
    return pl.pallas_call(
        _ssm_kernel,
        out_shape=jax.ShapeDtypeStruct((B, S, D_SSM), BF16),
        grid=(n,),
        in_specs=[pl.BlockSpec((B, T_CHUNK, D_SSM), lambda i: (0, i, 0))] + [full(a) for a in ws],
        out_specs=pl.BlockSpec((B, T_CHUNK, D_SSM), lambda i: (0, i, 0)),
        scratch_shapes=[pltpu.VMEM((rows, nstate), F32), pltpu.VMEM((rows, nstate), F32),
                        pltpu.VMEM((B, nstate), F32), pltpu.VMEM((B, nstate), F32),
                        pltpu.VMEM((T_CHUNK, B, D_SSM), F32)],
        compiler_params=_cparams(1),
        name="ssm",
    )(u, *ws)


def _attn_kernel(q_ref, k_ref, vt_ref, o_ref, m_sc, acc_sc, sa_sc, sb_sc, p_sc):
    qi = pl.program_id(2)
    tq = q_ref.shape[0]
    nh = m_sc.shape[0]
    neg = -1e30
    m_sc[...] = jnp.full(m_sc.shape, neg, F32)
    acc_sc[...] = jnp.zeros(acc_sc.shape, F32)
    ones = jnp.ones((DEN_ROWS, tq), BF16)

    def scores(j, s_sc):
        off = pl.multiple_of(j * tq, tq)
        for hh in range(nh):
            q = q_ref[:, hh * HEAD_PAD:(hh + 1) * HEAD_PAD]
            kb = k_ref[pl.ds(off, tq), hh * HEAD_PAD:(hh + 1) * HEAD_PAD]
            s_sc[hh] = lax.dot_general(kb, q, (((1,), (1,)), ((), ())),
                                       preferred_element_type=F32)

    def update(j, s_sc, masked):
        scale = []
        for hh in range(nh):
            if masked:
                keyi = lax.broadcasted_iota(jnp.int32, (tq, tq), 0)
                qryi = lax.broadcasted_iota(jnp.int32, (tq, tq), 1)
                s_sc[hh] = jnp.where(keyi <= qryi, s_sc[hh], neg)
            m_prev = m_sc[hh]
            m_new = jnp.maximum(m_prev, jnp.max(s_sc[hh], axis=0, keepdims=True))
            m_sc[hh] = m_new
            p_sc[hh] = jnp.exp2(s_sc[hh] - m_new).astype(BF16)
            scale.append(jnp.exp2(m_prev - m_new))
        for hh in range(nh):
            vte = jnp.concatenate([vt_ref[j, hh * V_HEAD:(hh + 1) * V_HEAD, :], ones], axis=0)
            acc_sc[hh] = scale[hh] * acc_sc[hh] + jnp.dot(vte, p_sc[hh], preferred_element_type=F32)

    scores(0, sa_sc)

    def pair(pp, c):
        j = 2 * pp
        scores(j + 1, sb_sc)
        update(j, sa_sc, False)
        scores(j + 2, sa_sc)
        update(j + 1, sb_sc, False)
        return c

    npair = qi // 2
    lax.fori_loop(0, npair, pair, 0)
    j0 = 2 * npair

    @pl.when(qi % 2 == 0)
    def _():
        update(j0, sa_sc, True)

    @pl.when(qi % 2 == 1)
    def _():
        scores(j0 + 1, sb_sc)
        update(j0, sa_sc, False)
        update(j0 + 1, sb_sc, True)

    outs = []
    for pr in range(nh // 2):
        halves = []
        for hh in (2 * pr, 2 * pr + 1):
            acc = acc_sc[hh]
            halves.append(acc[:V_HEAD] * (1.0 / acc[V_HEAD:V_HEAD + 1]))
        outs.append(jnp.concatenate(halves, axis=0).T)
    o_ref[...] = jnp.concatenate(outs, axis=1).astype(BF16)


def _attention(q, k, vt, B, S):
    nq = S // TQ
    nh = ATTN_HEADS
    hg = MLA_HEADS // nh
    return pl.pallas_call(
        _attn_kernel,
        out_shape=jax.ShapeDtypeStruct((B * S, MLA_HEADS * V_HEAD), BF16),
        grid=(B, hg, nq),
        in_specs=[pl.BlockSpec((TQ, nh * HEAD_PAD), lambda b, h, i: (b * nq + i, h)),
                  pl.BlockSpec((S, nh * HEAD_PAD), lambda b, h, i: (b, h)),
                  pl.BlockSpec((None, nq, nh * V_HEAD, TQ), lambda b, h, i: (b, 0, h, 0))],
        out_specs=pl.BlockSpec((TQ, nh * V_HEAD), lambda b, h, i: (b * nq + i, h)),
        scratch_shapes=[pltpu.VMEM((nh, 1, TQ), F32),
                        pltpu.VMEM((nh, V_HEAD + DEN_ROWS, TQ), F32),
                        pltpu.VMEM((nh, TQ, TQ), F32), pltpu.VMEM((nh, TQ, TQ), F32),
                        pltpu.VMEM((nh, TQ, TQ), BF16)],
        compiler_params=_cparams(3),
        name="attention",
    )(q, k, vt)


def _merge_kernel(x_ref, ada_ref, pc_ref, sgs_ref, sga_ref, ys_ref, ya_ref,
                  w_ups, w_upa, w_o, lng, lnb, rw, rb, ws1, ws3, ws2,
                  base_ref, hr_ref, rt_ref, *, alpha):
    D = x_ref.shape[1]
    tm = x_ref.shape[0]
    ada = ada_ref[...]
    merged = (pc_ref[...].astype(F32)
              + sgs_ref[...].astype(F32) * jnp.dot(ys_ref[...], w_ups[...], preferred_element_type=F32)
              + sga_ref[...].astype(F32) * jnp.dot(ya_ref[...], w_upa[...], preferred_element_type=F32))
    y = _bdot(merged, w_o[...])
    x1 = _ln_plain(alpha * x_ref[...] + ada[2:3, :] * y) * lng[...] + lnb[...]
    h2 = _ln_plain(x1) * (1.0 + ada[4:5, :]) + ada[3:4, :]
    hr_ref[:, :D] = h2

    G = N_EXPERT_GROUPS
    logits = lax.dot_general(rw[...], h2, (((1,), (1,)), ((), ())),
                             precision=lax.Precision.HIGHEST, preferred_element_type=F32)
    sc0 = jax.nn.sigmoid(logits)
    sl0 = sc0 + rb[...]
    sc = [sc0[j * G:(j + 1) * G] for j in range(EXPERTS_PER_GROUP)]
    sl = [sl0[j * G:(j + 1) * G] for j in range(EXPERTS_PER_GROUP)]
    hi1, lo1 = jnp.maximum(sl[0], sl[1]), jnp.minimum(sl[0], sl[1])
    hi2, lo2 = jnp.maximum(sl[2], sl[3]), jnp.minimum(sl[2], sl[3])
    gscore = jnp.maximum(hi1, hi2) + jnp.maximum(jnp.minimum(hi1, hi2), jnp.maximum(lo1, lo2))
    sub = lax.broadcasted_iota(jnp.int32, (G, tm), 0)
    gmax = jnp.max(gscore, axis=0, keepdims=True)
    gidx = jnp.min(jnp.where(gscore == gmax, sub, G), axis=0, keepdims=True)
    pick = sub == gidx
    vs = [jnp.sum(jnp.where(pick, a, 0.0), axis=0, keepdims=True) for a in sl]
    ss = [jnp.sum(jnp.where(pick, a, 0.0), axis=0, keepdims=True) for a in sc]
    chosen = []
    for j in range(EXPERTS_PER_GROUP):
        cnt = jnp.zeros((1, tm), jnp.int32)
        for k in range(EXPERTS_PER_GROUP):
            if k == j:
                continue
            beats = (vs[k] > vs[j]) | ((vs[k] == vs[j]) & (k < j))
            cnt = cnt + beats.astype(jnp.int32)
        chosen.append(cnt < 2)
    wsel = [jnp.where(chosen[j], ss[j], 0.0) for j in range(EXPERTS_PER_GROUP)]
    wsum = wsel[0] + wsel[1] + wsel[2] + wsel[3]
    rows = [wsel[j] / wsum for j in range(EXPERTS_PER_GROUP)] + [gidx.astype(F32)]
    rows += [jnp.zeros((1, tm), F32)] * (8 - len(rows))
    route_t = jnp.concatenate(rows, axis=0)
    rt_ref[...] = route_t
    eye = (lax.broadcasted_iota(jnp.int32, (8, ROUTE_W), 0)
           == lax.broadcasted_iota(jnp.int32, (8, ROUTE_W), 1)).astype(F32)
    hr_ref[:, D:] = lax.dot_general(route_t, eye, (((0,), (0,)), ((), ())),
                                    precision=lax.Precision.HIGHEST, preferred_element_type=F32)

    h2b = h2.astype(BF16)
    a1 = jnp.dot(h2b, ws1[...], preferred_element_type=F32)
    a3 = jnp.dot(h2b, ws3[...], preferred_element_type=F32)
    ysh = _bdot(a1 * jax.nn.sigmoid(a1) * a3, ws2[...])
    base_ref[...] = alpha * x1 + ada[5:6, :] * ysh


def _merge(x2d, ada_l, pc, sgs, sga, ys, ya, wts, B, S, alpha):
    T, D = x2d.shape
    tm = TM_PROJ
    ns = S // tm
    row = lambda b, s: (b * ns + s, 0)
    names = ["w_ups", "w_upa", "w_o", "lng0", "lnb0", "rw", "rb", "ws1", "ws3", "ws2"]
    ws = [wts[n] for n in names]
    full = lambda a: pl.BlockSpec(a.shape, lambda b, s: (0,) * a.ndim, pipeline_mode=pl.Buffered(1))
    return pl.pallas_call(
        functools.partial(_merge_kernel, alpha=alpha),
        out_shape=(jax.ShapeDtypeStruct((T, D), F32),
                   jax.ShapeDtypeStruct((T, D + ROUTE_W), F32),
                   jax.ShapeDtypeStruct((8, T), F32)),
        grid=(B, ns),
        in_specs=[pl.BlockSpec((tm, D), row),
                  pl.BlockSpec((None, 6, D), lambda b, s: (b, 0, 0)),
                  pl.BlockSpec((tm, D), row), pl.BlockSpec((tm, D), row), pl.BlockSpec((tm, D), row),
                  pl.BlockSpec((tm, D_SSM), row),
                  pl.BlockSpec((tm, MLA_HEADS * V_HEAD), row)]
                 + [full(a) for a in ws],
        out_specs=(pl.BlockSpec((tm, D), row), pl.BlockSpec((tm, D + ROUTE_W), row),
                   pl.BlockSpec((8, tm), lambda b, s: (0, b * ns + s))),
        compiler_params=_cparams(2),
        name="merge",
    )(x2d, ada_l, pc, sgs, sga, ys, ya, *ws)


def _route_plan(gid, T):
    R = MOE_ROWS
    G = N_EXPERT_GROUPS
    onehot = (gid[:, None] == jnp.arange(G, dtype=jnp.int32)[None, :]).astype(jnp.int32)
    csum = jnp.cumsum(onehot, axis=0)
    counts = csum[-1]
    rank = jnp.sum(onehot * csum, axis=1) - 1
    padded = (counts + R - 1) // R * R
    pad_end = jnp.cumsum(padded)
    pad_start = pad_end - padded
    dest = jnp.sum(onehot * pad_start[None, :], axis=1) + rank
    nb = T // R + G
    starts = jnp.arange(nb, dtype=jnp.int32) * R
    block_g = jnp.minimum(jnp.sum((starts[:, None] >= pad_end[None, :]).astype(jnp.int32), axis=1), G - 1)
    gsel = (block_g[:, None] == jnp.arange(G, dtype=jnp.int32)[None, :]).astype(jnp.int32)
    group_end = jnp.sum(gsel * (pad_start + counts)[None, :], axis=1)
    nvalid = jnp.clip(group_end - starts, 0, R)
    return dest.astype(jnp.int32), block_g.astype(jnp.int32), nvalid.astype(jnp.int32)


def _permute_kernel(dest_ref, nvalid_ref, hr_ref, hs_hbm, zbuf, sem, zsem):
    i = pl.program_id(0)
    tp = hr_ref.shape[0]
    R = zbuf.shape[0]
    nb = nvalid_ref.shape[0]

    @pl.when(i == 0)
    def _():
        zbuf[...] = jnp.zeros(zbuf.shape, F32)

        def fill(b, n):
            partial = nvalid_ref[b] < R

            @pl.when(partial)
            def _():
                pltpu.make_async_copy(zbuf, hs_hbm.at[pl.ds(b * R, R)], zsem).start()

            return n + partial.astype(jnp.int32)

        nfill = lax.fori_loop(0, nb, fill, 0)

        def drain(k, c):
            pltpu.make_async_copy(zbuf, hs_hbm.at[pl.ds(0, R)], zsem).wait()
            return c

        lax.fori_loop(0, nfill, drain, 0)

    def body(g, c):
        r0 = pl.multiple_of(g * 8, 8)
        for k in range(8):
            d = dest_ref[i * tp + r0 + k]
            pltpu.make_async_copy(hr_ref.at[pl.ds(r0 + k, 1)], hs_hbm.at[pl.ds(d, 1)], sem).start()
        return c

    lax.fori_loop(0, tp // 8, body, 0)
    pltpu.make_async_copy(hr_ref, hs_hbm.at[pl.ds(0, tp)], sem).wait()


def _permute(hr, dest, nvalid):
    T, W = hr.shape
    R = MOE_ROWS
    tp = PERM_ROWS
    nb = nvalid.shape[0]
    return pl.pallas_call(
        _permute_kernel,
        out_shape=jax.ShapeDtypeStruct((nb * R, W), F32),
        grid_spec=pltpu.PrefetchScalarGridSpec(
            num_scalar_prefetch=2,
            grid=(T // tp,),
            in_specs=[pl.BlockSpec((tp, W), lambda i, d, n: (i, 0))],
            out_specs=pl.BlockSpec(memory_space=pl.ANY),
            scratch_shapes=[pltpu.VMEM((R, W), F32), pltpu.SemaphoreType.DMA, pltpu.SemaphoreType.DMA]),
        compiler_params=_cparams(1),
        name="permute",
    )(dest, nvalid, hr)


def _moe_kernel(bg_ref, nvalid_ref, hs_ref, w1_ref, w3_ref, w2_ref, y_ref, w1b, w3b, w2b):
    i = pl.program_id(0)
    R, D = y_ref.shape

    @pl.when((i == 0) | (bg_ref[i] != bg_ref[jnp.maximum(i - 1, 0)]))
    def _():
        w1b[...] = w1_ref[...].astype(BF16)
        w3b[...] = w3_ref[...].astype(BF16)
        w2b[...] = w2_ref[...].astype(BF16)

    @pl.when(nvalid_ref[i] > 0)
    def _():
        xb = hs_ref[:, :D].astype(BF16)
        acc = jnp.zeros((R, D), F32)
        for j in range(EXPERTS_PER_GROUP):
            a1 = jnp.dot(xb, w1b[j], preferred_element_type=F32)
            a3 = jnp.dot(xb, w3b[j], preferred_element_type=F32)
            hj = a1 * jax.nn.sigmoid(a1) * a3 * hs_ref[:, D + j:D + j + 1]
            acc = acc + _bdot(hj, w2b[j])
        y_ref[...] = acc

    @pl.when(nvalid_ref[i] == 0)
    def _():
        y_ref[...] = jnp.zeros((R, D), F32)


def _moe(hs, block_g, nvalid, w1, w3, w2, layer):
    D = w1.shape[2]
    W = hs.shape[1]
    R = MOE_ROWS
    nb = nvalid.shape[0]
    wspec_in = pl.BlockSpec((None, EXPERTS_PER_GROUP, D, D_EXPERT), lambda i, bg, nv: (layer, bg[i], 0, 0))
    wspec_out = pl.BlockSpec((None, EXPERTS_PER_GROUP, D_EXPERT, D), lambda i, bg, nv: (layer, bg[i], 0, 0))
    return pl.pallas_call(
        _moe_kernel,
        out_shape=jax.ShapeDtypeStruct((nb * R, D), F32),
        grid_spec=pltpu.PrefetchScalarGridSpec(
            num_scalar_prefetch=2,
            grid=(nb,),
            in_specs=[pl.BlockSpec((R, W), lambda i, bg, nv: (i, 0)), wspec_in, wspec_in, wspec_out],
            out_specs=pl.BlockSpec((R, D), lambda i, bg, nv: (i, 0)),
            scratch_shapes=[pltpu.VMEM((EXPERTS_PER_GROUP, D, D_EXPERT), BF16),
                            pltpu.VMEM((EXPERTS_PER_GROUP, D, D_EXPERT), BF16),
                            pltpu.VMEM((EXPERTS_PER_GROUP, D_EXPERT, D), BF16)]),
        compiler_params=_cparams(1),
        name="moe",
    )(block_g, nvalid, hs, w1, w3, w2)


def _final_kernel(dest_ref, base_ref, ys_hbm, ada_ref, lng, lnb, o_ref, ybuf, sem):
    i = pl.program_id(0)
    n = pl.num_programs(0)
    tf = base_ref.shape[0]
    slot = i % 2

    def start_gather(step, s):
        def body(r, c):
            d = dest_ref[step * tf + r]
            pltpu.make_async_copy(ys_hbm.at[pl.ds(d, 1)], ybuf.at[s, pl.ds(r, 1)], sem.at[s]).start()
            return c

        lax.fori_loop(0, tf, body, 0, unroll=8)

    @pl.when(i == 0)
    def _():
        start_gather(0, 0)

    @pl.when(i + 1 < n)
    def _():
        start_gather(i + 1, 1 - slot)

    pltpu.make_async_copy(ys_hbm.at[pl.ds(0, tf)], ybuf.at[slot], sem.at[slot]).wait()
    ada = ada_ref[...]
    o_ref[...] = _ln_plain(base_ref[...] + ada[5:6, :] * ybuf[slot]) * lng[...] + lnb[...]


def _final(base, ys, dest, ada_l, lng, lnb, B, S):
    T, D = base.shape
    tf = FINAL_ROWS
    per_batch = S // tf
    return pl.pallas_call(
        _final_kernel,
        out_shape=jax.ShapeDtypeStruct((T, D), F32),
        grid_spec=pltpu.PrefetchScalarGridSpec(
            num_scalar_prefetch=1,
            grid=(T // tf,),
            in_specs=[pl.BlockSpec((tf, D), lambda i, d: (i, 0)),
                      pl.BlockSpec(memory_space=pl.ANY),
                      pl.BlockSpec((None, 6, D), lambda i, d: (i // per_batch, 0, 0)),
                      pl.BlockSpec((1, D), lambda i, d: (0, 0)), pl.BlockSpec((1, D), lambda i, d: (0, 0))],
            out_specs=pl.BlockSpec((tf, D), lambda i, d: (i, 0)),
            scratch_shapes=[pltpu.VMEM((2, tf, D), F32), pltpu.SemaphoreType.DMA((2,))]),
        compiler_params=_cparams(1),
        name="final_ln",
    )(dest, base, ys, ada_l, lng, lnb)


def _rope_tables(positions):
    inv_freq = ROPE_THETA ** (-jnp.arange(0, QK_ROPE, 2, dtype=F32) / QK_ROPE)
    ang = positions.astype(F32)[..., None] * inv_freq
    dense = ang.reshape(-1, LANE)
    cos, sin = jnp.cos(dense).reshape(ang.shape), jnp.sin(dense).reshape(ang.shape)
    ones = jnp.ones(cos.shape[:-1] + (QK_NOPE,), F32)
    zpad = jnp.zeros(cos.shape[:-1] + (HEAD_PAD - QK_NOPE - QK_ROPE,), F32)
    cos128 = jnp.concatenate([ones, cos, cos, zpad], axis=-1)
    sin128 = jnp.concatenate([0.0 * ones, -sin, sin, zpad], axis=-1)
    T = positions.shape[0] * positions.shape[1]
    return cos128.reshape(T, HEAD_PAD), sin128.reshape(T, HEAD_PAD)


def _swap_halves(w):
    half = w.shape[-1] // 2
    return jnp.concatenate([w[..., half:], w[..., :half]], axis=-1)


def _layer_weights(l, p):
    D = p["w_in"].shape[1]
    w_in = p["w_in"][l]
    o = 0
    seg = {}
    for name, width in (("cv", 3 * D_CONV), ("ssm", D_SSM), ("cq", Q_LORA), ("ckv", KV_LORA),
                        ("kr", QK_ROPE), ("g", 3 * D)):
        seg[name] = w_in[:, o:o + width]
        o += width
    pad_r = HEAD_PAD - QK_NOPE - QK_ROPE
    zl = jnp.zeros((D, QK_NOPE), F32)
    zr = jnp.zeros((D, pad_r), F32)
    w = {
        "w_cv": seg["cv"].astype(BF16), "w_ssm": seg["ssm"].astype(BF16),
        "w_lat": jnp.concatenate([seg["cq"], seg["ckv"], zl, seg["kr"], zr,
                                  zl, _swap_halves(seg["kr"]), zr], axis=1).astype(BF16),
        "w_g": seg["g"].astype(BF16),
        "convw": p["conv_w"][l],
        "qn": p["q_norm"][l].reshape(1, Q_LORA), "kvn": p["kv_norm"][l].reshape(1, KV_LORA),
        "w_upc": p["w_up_conv"][l].astype(BF16),
    }
    scale = (QK_NOPE + QK_ROPE) ** -0.5 * math.log2(math.e)
    wq = p["w_uq"][l].reshape(Q_LORA, MLA_HEADS, QK_NOPE + QK_ROPE) * scale
    zq = jnp.zeros((Q_LORA, MLA_HEADS, pad_r), F32)
    zn = jnp.zeros((Q_LORA, MLA_HEADS, QK_NOPE), F32)
    w["wq_a"] = jnp.concatenate([wq, zq], axis=-1).reshape(Q_LORA, -1).astype(BF16)
    w["wq_b"] = jnp.concatenate([zn, _swap_halves(wq[..., QK_NOPE:]), zq],
                                axis=-1).reshape(Q_LORA, -1).astype(BF16)
    wk = p["w_uk"][l].reshape(KV_LORA, MLA_HEADS, QK_NOPE)
    zk = jnp.zeros((KV_LORA, MLA_HEADS, HEAD_PAD - QK_NOPE), F32)
    w["wk"] = jnp.concatenate([wk, zk], axis=-1).reshape(KV_LORA, -1).astype(BF16)
    w["wv"] = p["w_uv"][l].T.astype(BF16)

    ar, ai = p["ssm_a_re"][l], p["ssm_a_im"][l]
    dt = jnp.exp(p["ssm_log_dt"][l])[:, None]
    mag = jnp.exp(ar * dt)
    lb_re, lb_im = mag * jnp.cos(ai * dt), mag * jnp.sin(ai * dt)
    den = ar * ar + ai * ai
    nr, ni = lb_re - 1.0, lb_im
    f_re, f_im = (nr * ar + ni * ai) / den, (ni * ar - nr * ai) / den
    br, bi = p["ssm_b_re"][l], p["ssm_b_im"][l]
    bb_re = f_re[..., None] * br - f_im[..., None] * bi
    bb_im = f_re[..., None] * bi + f_im[..., None] * br
    gpc = LANE // SSM_GROUP_DIM
    nch = SSM_GROUPS // gpc
    eye = jnp.eye(gpc, dtype=F32)

    def bmat(bb):
        return jnp.einsum("cgnp,gh->cgphn", bb.reshape(nch, gpc, SSM_STATE, SSM_GROUP_DIM),
                          eye).reshape(nch, LANE, gpc * SSM_STATE).astype(BF16)

    def cmat(cc):
        return jnp.einsum("cgpn,gh->cgnhp", cc.reshape(nch, gpc, SSM_GROUP_DIM, SSM_STATE),
                          eye).reshape(nch, gpc * SSM_STATE, LANE).astype(BF16)

    w["ssm_bre"], w["ssm_bim"] = bmat(bb_re), bmat(bb_im)
    w["ssm_cre"], w["ssm_cim"] = cmat(p["ssm_c_re"][l]), cmat(-p["ssm_c_im"][l])
    nrow = 8
    w["ssm_lre"] = jnp.broadcast_to(lb_re.reshape(1, -1), (nrow, SSM_GROUPS * SSM_STATE))
    w["ssm_lim"] = jnp.broadcast_to(lb_im.reshape(1, -1), (nrow, SSM_GROUPS * SSM_STATE))
    w["ssm_d"] = p["ssm_d"][l].reshape(1, D_SSM)
    w["ssm_wglu"] = p["ssm_w_glu"][l].astype(BF16)

    w["w_ups"] = p["w_up_ssm"][l].astype(BF16)
    w["w_upa"] = p["w_up_attn"][l].astype(BF16)
    w["w_o"] = p["w_o"][l].astype(BF16)
    w["lng0"], w["lnb0"] = p["ln_g"][l, 0].reshape(1, D), p["ln_b"][l, 0].reshape(1, D)
    w["lng1"], w["lnb1"] = p["ln_g"][l, 1].reshape(1, D), p["ln_b"][l, 1].reshape(1, D)
    rwg = p["router_w"].reshape(D, N_EXPERT_GROUPS, EXPERTS_PER_GROUP)
    rbg = p["router_bias"].reshape(N_EXPERT_GROUPS, EXPERTS_PER_GROUP)
    w["rw"] = jnp.transpose(rwg, (2, 1, 0)).reshape(N_EXPERTS, D)
    w["rb"] = jnp.transpose(rbg, (1, 0)).reshape(N_EXPERTS, 1)
    w["ws1"] = p["shared_w1"][l].astype(BF16)
    w["ws3"] = p["shared_w3"][l].astype(BF16)
    w["ws2"] = p["shared_w2"][l].astype(BF16)
    return w


def kernel(x, c, positions, w_ada, b_ada, w_in, conv_w, ssm_a_re, ssm_a_im, ssm_b_re, ssm_b_im,
           ssm_c_re, ssm_c_im, ssm_d, ssm_log_dt, ssm_w_glu, q_norm, w_uq, kv_norm, w_uk, w_uv,
           w_up_conv, w_up_ssm, w_up_attn, w_o, ln_g, ln_b, router_w, router_bias,
           exp_w1, exp_w3, exp_w2, shared_w1, shared_w3, shared_w2):
    B, S, D = x.shape
    L = w_in.shape[0]
    T = B * S
    assert B == 8 and S % max(TM_PROJ, T_CHUNK, TQ, FINAL_ROWS) == 0 and T % max(MOE_ROWS, PERM_ROWS) == 0
    assert TM_PROJ == TQ
    p = dict(w_in=w_in, conv_w=conv_w, ssm_a_re=ssm_a_re, ssm_a_im=ssm_a_im, ssm_b_re=ssm_b_re,
             ssm_b_im=ssm_b_im, ssm_c_re=ssm_c_re, ssm_c_im=ssm_c_im, ssm_d=ssm_d,
             ssm_log_dt=ssm_log_dt, ssm_w_glu=ssm_w_glu, q_norm=q_norm, w_uq=w_uq, kv_norm=kv_norm,
             w_uk=w_uk, w_uv=w_uv, w_up_conv=w_up_conv, w_up_ssm=w_up_ssm, w_up_attn=w_up_attn,
             w_o=w_o, ln_g=ln_g, ln_b=ln_b, router_w=router_w, router_bias=router_bias,
             exp_w1=exp_w1, exp_w3=exp_w3, exp_w2=exp_w2, shared_w1=shared_w1,
             shared_w3=shared_w3, shared_w2=shared_w2)
    alpha = (2 * L) ** 0.25
    cos128, sin128 = _rope_tables(positions)
    ada = _ada(c, w_ada, b_ada).reshape(L, B, 6, D)
    x2d = x.reshape(T, D)
    for l in range(L):
        wts = _layer_weights(l, p)
        pc, sgs, sga, u, q, k, v = _inproj(x2d, ada[l], cos128, sin128, wts, B, S)
        ys = _ssm(u.reshape(B, S, D_SSM), wts, B, S).reshape(T, D_SSM)
        ya = _attention(q, k, v, B, S)
        base, hr, rt = _merge(x2d, ada[l], pc, sgs, sga, ys, ya, wts, B, S, alpha)
        gid = rt[EXPERTS_PER_GROUP].astype(jnp.int32)
        dest, block_g, nvalid = _route_plan(gid, T)
        hs = _permute(hr, dest, nvalid)
        ysort = _moe(hs, block_g, nvalid, exp_w1, exp_w3, exp_w2, l)
        x2d = _final(base, ysort, dest, ada[l], wts["lng1"], wts["lnb1"], B, S)
    return x2d.reshape(B, S, D)
```

```python
import functools
import math

import jax
import jax.numpy as jnp
from jax import lax
from jax.experimental import pallas as pl
from jax.experimental.pallas import tpu as pltpu

F32 = jnp.float32
BF16 = jnp.bfloat16

D_CONV = 512
D_SSM = 512
SSM_GROUP_DIM = 16
SSM_GROUPS = 32
SSM_STATE = 64
MLA_HEADS = 8
QK_NOPE = 64
QK_ROPE = 32
V_HEAD = 64
Q_LORA = 256
KV_LORA = 128
ROPE_THETA = 10000.0
N_EXPERTS = 32
N_EXPERT_GROUPS = 8
EXPERTS_PER_GROUP = 4
D_EXPERT = 256
LN_EPS = 1e-5
RMS_EPS = 1e-6

LANE = 128
HEAD_PAD = 128
VMEM_LIMIT = 56 * 1024 * 1024

TM_PROJ = 512
T_CHUNK = 64
TQ = 512
ATTN_HEADS = 4
DEN_ROWS = 16
MOE_ROWS = 256
PERM_ROWS = 1024
FINAL_ROWS = 512
ROUTE_W = 128


def _cparams(n_axes):
    return pltpu.CompilerParams(dimension_semantics=("arbitrary",) * n_axes,
                                vmem_limit_bytes=VMEM_LIMIT)


def _ln_plain(x):
    mu = jnp.mean(x, axis=-1, keepdims=True)
    xc = x - mu
    var = jnp.mean(xc * xc, axis=-1, keepdims=True)
    return xc * lax.rsqrt(var + LN_EPS)


def _rms(x, g):
    return x * lax.rsqrt(jnp.mean(x * x, axis=-1, keepdims=True) + RMS_EPS) * g


def _bdot(a, b):
    return jnp.dot(a.astype(BF16), b, preferred_element_type=F32)


def _ada_kernel(c_ref, w_ref, b_ref, o_ref):
    c = c_ref[...]
    cond = c * jax.nn.sigmoid(c)
    o_ref[...] = jnp.dot(cond, w_ref[...], precision=lax.Precision.HIGHEST,
                         preferred_element_type=F32) + b_ref[...]


def _ada(c, w_ada, b_ada):
    L, D, D6 = w_ada.shape
    B = c.shape[0]
    nj = D6 // D
    return pl.pallas_call(
        _ada_kernel,
        out_shape=jax.ShapeDtypeStruct((L, B, D6), F32),
        grid=(L, nj),
        in_specs=[pl.BlockSpec((B, D), lambda l, j: (0, 0)),
                  pl.BlockSpec((None, D, D), lambda l, j: (l, 0, j)),
                  pl.BlockSpec((None, 1, D), lambda l, j: (l, 0, j))],
        out_specs=pl.BlockSpec((None, B, D), lambda l, j: (l, 0, j)),
        compiler_params=_cparams(2),
        name="ada",
    )(c, w_ada, b_ada.reshape(L, 1, D6))


def _inproj_kernel(x_ref, ada_ref, cos_ref, sin_ref,
                   w_cv, w_ssm, w_lat, w_g,
                   convw, qn, wq_a, wq_b, kvn, wk, wv, w_upc,
                   pc_ref, sgs_ref, sga_ref, u_ref, q_ref, k_ref, v_ref,
                   carry_ref):
    si = pl.program_id(1)
    D = x_ref.shape[1]
    tm = x_ref.shape[0]

    @pl.when(si == 0)
    def _():
        carry_ref[...] = jnp.zeros_like(carry_ref)

    ada = ada_ref[...]
    h = _ln_plain(x_ref[...]) * (1.0 + ada[1:2, :]) + ada[0:1, :]
    hb = h.astype(BF16)

    pcv = jnp.dot(hb, w_cv[...], preferred_element_type=F32)
    u = pcv[:, D_CONV:2 * D_CONV] * pcv[:, :D_CONV]
    gb = pcv[:, 2 * D_CONV:]
    tail = carry_ref[...]
    row8 = lax.broadcasted_iota(jnp.int32, (8, D_CONV), 0)
    r1 = pltpu.roll(u, 1, 0)
    r2 = pltpu.roll(u, 2, 0)
    t1 = pltpu.roll(tail, 1, 0)
    t2 = pltpu.roll(tail, 2, 0)
    u1 = jnp.concatenate([jnp.where(row8 < 1, t1, r1[:8]), r1[8:]], axis=0)
    u2 = jnp.concatenate([jnp.where(row8 < 2, t2, r2[:8]), r2[8:]], axis=0)
    cw = convw[...]
    yc = gb * (cw[0:1, :] * u2 + cw[1:2, :] * u1 + cw[2:3, :] * u)
    carry_ref[...] = u[tm - 8:, :]

    gates = jnp.dot(hb, w_g[...], preferred_element_type=F32)
    pc_ref[...] = (jax.nn.sigmoid(gates[:, :D]) * _bdot(yc, w_upc[...])).astype(BF16)
    sgs_ref[...] = jax.nn.sigmoid(gates[:, D:2 * D]).astype(BF16)
    sga_ref[...] = jax.nn.sigmoid(gates[:, 2 * D:]).astype(BF16)

    u_ref[...] = jnp.dot(hb, w_ssm[...], preferred_element_type=F32).astype(BF16)

    cos = cos_ref[...]
    sin = sin_ref[...]
    lat = jnp.dot(hb, w_lat[...], preferred_element_type=F32)
    o_kv, o_kr = Q_LORA, Q_LORA + KV_LORA
    cq = _rms(lat[:, :o_kv], qn[...]).astype(BF16)
    qa = jnp.dot(cq, wq_a[...], preferred_element_type=F32)
    qb = jnp.dot(cq, wq_b[...], preferred_element_type=F32)
    ckv = _rms(lat[:, o_kv:o_kr], kvn[...]).astype(BF16)
    kn = jnp.dot(ckv, wk[...], preferred_element_type=F32)
    kr = lat[:, o_kr:o_kr + HEAD_PAD] * cos + lat[:, o_kr + HEAD_PAD:] * sin
    for hd in range(MLA_HEADS):
        sl = slice(hd * HEAD_PAD, (hd + 1) * HEAD_PAD)
        q_ref[:, sl] = (qa[:, sl] * cos + qb[:, sl] * sin).astype(BF16)
        k_ref[:, sl] = (kn[:, sl] + kr).astype(BF16)
    v_ref[...] = lax.dot_general(wv[...], ckv, (((1,), (1,)), ((), ())),
                                 preferred_element_type=F32).astype(BF16)


def _inproj(x2d, ada_l, cos128, sin128, wts, B, S):
    T, D = x2d.shape
    tm = TM_PROJ
    ns = S // tm
    row = lambda b, s: (b * ns + s, 0)
    full = lambda a: pl.BlockSpec(a.shape, lambda b, s: (0,) * a.ndim, pipeline_mode=pl.Buffered(1))
    names = ["w_cv", "w_ssm", "w_lat", "w_g", "convw", "qn",
             "wq_a", "wq_b", "kvn", "wk", "wv", "w_upc"]
    ws = [wts[n] for n in names]
    out_shape = (
        jax.ShapeDtypeStruct((T, D), BF16),
        jax.ShapeDtypeStruct((T, D), BF16),
        jax.ShapeDtypeStruct((T, D), BF16),
        jax.ShapeDtypeStruct((T, D_SSM), BF16),
        jax.ShapeDtypeStruct((T, MLA_HEADS * HEAD_PAD), BF16),
        jax.ShapeDtypeStruct((T, MLA_HEADS * HEAD_PAD), BF16),
        jax.ShapeDtypeStruct((B, ns, MLA_HEADS * V_HEAD, tm), BF16),
    )
    out_specs = (
        pl.BlockSpec((tm, D), row), pl.BlockSpec((tm, D), row), pl.BlockSpec((tm, D), row),
        pl.BlockSpec((tm, D_SSM), row),
        pl.BlockSpec((tm, MLA_HEADS * HEAD_PAD), row),
        pl.BlockSpec((tm, MLA_HEADS * HEAD_PAD), row),
        pl.BlockSpec((None, None, MLA_HEADS * V_HEAD, tm), lambda b, s: (b, s, 0, 0)),
    )
    return pl.pallas_call(
        _inproj_kernel,
        out_shape=out_shape,
        grid=(B, ns),
        in_specs=[pl.BlockSpec((tm, D), row),
                  pl.BlockSpec((None, 6, D), lambda b, s: (b, 0, 0)),
                  pl.BlockSpec((tm, LANE), row), pl.BlockSpec((tm, LANE), row)]
                 + [full(a) for a in ws],
        out_specs=out_specs,
        scratch_shapes=[pltpu.VMEM((8, D_CONV), F32)],
        compiler_params=_cparams(2),
        name="inproj",
    )(x2d, ada_l, cos128, sin128, *ws)


def _ssm_kernel(u_ref, bre_ref, bim_ref, lre_ref, lim_ref, cre_ref, cim_ref, d_ref, wglu_ref, perm_ref,
                o_ref, are, aim, au, bre_s, bim_s, bu, st_re, st_im, tmaj):
    i = pl.program_id(0)
    batch, nsteps = u_ref.shape[0], u_ref.shape[1]
    rows = batch * nsteps
    nchunk = bre_ref.shape[0]
    cw = bre_ref.shape[2]

    @pl.when(i == 0)
    def _():
        st_re[...] = jnp.zeros_like(st_re)
        st_im[...] = jnp.zeros_like(st_im)
        bre_s[...] = jnp.zeros_like(bre_s)
        bim_s[...] = jnp.zeros_like(bim_s)
        bu[...] = jnp.zeros_like(bu)

    def body(cur_re, cur_im, cur_u, prev_re, prev_im, prev_u):
        for b in range(batch):
            tmaj[:, b, :] = u_ref[b].astype(F32)
        uf = tmaj[...].reshape(rows, D_SSM)
        cur_u[...] = uf
        ub = uf.astype(BF16)
        for c in range(nchunk):
            uc = ub[:, c * LANE:(c + 1) * LANE]
            cur_re[:, c * cw:(c + 1) * cw] = jnp.dot(uc, bre_ref[c], preferred_element_type=F32)
            cur_im[:, c * cw:(c + 1) * cw] = jnp.dot(uc, bim_ref[c], preferred_element_type=F32)

        ys = []

        def out_chunk(c):
            ys.append(_bdot(prev_re[:, c * cw:(c + 1) * cw], cre_ref[c])
                      + _bdot(prev_im[:, c * cw:(c + 1) * cw], cim_ref[c]))

        def out_finish():
            y = jnp.concatenate(ys, axis=1) + d_ref[...] * prev_u[...]
            y = 0.5 * y * (1.0 + jnp.tanh(math.sqrt(2.0 / math.pi) * (y + 0.044715 * (y * y * y))))
            z = _bdot(y, wglu_ref[...])
            g = (z[:, :D_SSM] * jax.nn.sigmoid(z[:, D_SSM:])).astype(BF16)
            o_ref[...] = jnp.dot(perm_ref[...], g, preferred_element_type=F32).astype(BF16).reshape(
                batch, nsteps, D_SSM)

        pieces = [functools.partial(out_chunk, c) for c in range(nchunk)] + [out_finish]
        every = nsteps // len(pieces)
        pr, pi = st_re[...], st_im[...]
        for t in range(nsteps):
            r = t * batch
            lre = lre_ref[...]
            lim = lim_ref[...]
            nr = lre * pr - lim * pi + cur_re[r:r + batch, :]
            ni = lre * pi + lim * pr + cur_im[r:r + batch, :]
            cur_re[r:r + batch, :] = nr
            cur_im[r:r + batch, :] = ni
            pr, pi = nr, ni
            if t % every == every - 1 and pieces:
                pieces.pop(0)()
        while pieces:
            pieces.pop(0)()
        st_re[...] = pr
        st_im[...] = pi

    @pl.when(i % 2 == 0)
    def _():
        body(are, aim, au, bre_s, bim_s, bu)

    @pl.when(i % 2 == 1)
    def _():
        body(bre_s, bim_s, bu, are, aim, au)


def _ssm(u, wts, B, S):
    rows = T_CHUNK * B
    n = S // T_CHUNK
    nstate = SSM_GROUPS * SSM_STATE
    names = ["ssm_bre", "ssm_bim", "ssm_lre", "ssm_lim", "ssm_cre", "ssm_cim", "ssm_d", "ssm_wglu"]
    r = jnp.arange(rows, dtype=jnp.int32)
    src = (r % T_CHUNK) * B + r // T_CHUNK
    perm = (src[:, None] == jnp.arange(rows, dtype=jnp.int32)[None, :]).astype(BF16)
    ws = [wts[k] for k in names] + [perm]
    full = lambda a: pl.BlockSpec(a.shape, lambda i: (0,) * a.ndim, pipeline_mode=pl.Buffered(1))
    return pl.pallas_call(
        _ssm_kernel,
        out_shape=jax.ShapeDtypeStruct((B, S, D_SSM), BF16),
        grid=(n + 1,),
        in_specs=[pl.BlockSpec((B, T_CHUNK, D_SSM), lambda i: (0, jnp.minimum(i, n - 1), 0))]
                 + [full(a) for a in ws],
        out_specs=pl.BlockSpec((B, T_CHUNK, D_SSM), lambda i: (0, jnp.maximum(i - 1, 0), 0)),
        scratch_shapes=[pltpu.VMEM((rows, nstate), F32), pltpu.VMEM((rows, nstate), F32),
                        pltpu.VMEM((rows, D_SSM), F32),
                        pltpu.VMEM((rows, nstate), F32), pltpu.VMEM((rows, nstate), F32),
                        pltpu.VMEM((rows, D_SSM), F32),
                        pltpu.VMEM((B, nstate), F32), pltpu.VMEM((B, nstate), F32),
                        pltpu.VMEM((T_CHUNK, B, D_SSM), F32)],
        compiler_params=_cparams(1),
        name="ssm",
    )(u, *ws)


def _attn_kernel(q_ref, k_ref, vt_ref, o_ref, m_sc, acc_sc, sa_sc, sb_sc, p_sc):
    qi = pl.program_id(2)
    tq = q_ref.shape[0]
    nh = m_sc.shape[0]
    neg = -1e30
    m_sc[...] = jnp.full(m_sc.shape, neg, F32)
    acc_sc[...] = jnp.zeros(acc_sc.shape, F32)
    ones = jnp.ones((DEN_ROWS, tq), BF16)

    def scores(j, s_sc):
        off = pl.multiple_of(j * tq, tq)
        for hh in range(nh):
            q = q_ref[:, hh * HEAD_PAD:(hh + 1) * HEAD_PAD]
            kb = k_ref[pl.ds(off, tq), hh * HEAD_PAD:(hh + 1) * HEAD_PAD]
            s_sc[hh] = lax.dot_general(kb, q, (((1,), (1,)), ((), ())),
                                       preferred_element_type=F32)

    def update(j, s_sc, masked):
        scale = []
        for hh in range(nh):
            if masked:
                keyi = lax.broadcasted_iota(jnp.int32, (tq, tq), 0)
                qryi = lax.broadcasted_iota(jnp.int32, (tq, tq), 1)
                s_sc[hh] = jnp.where(keyi <= qryi, s_sc[hh], neg)
            m_prev = m_sc[hh]
            m_new = jnp.maximum(m_prev, jnp.max(s_sc[hh], axis=0, keepdims=True))
            m_sc[hh] = m_new
            p_sc[hh] = jnp.exp2(s_sc[hh] - m_new).astype(BF16)
            scale.append(jnp.exp2(m_prev - m_new))
        for hh in range(nh):
            vte = jnp.concatenate([vt_ref[j, hh * V_HEAD:(hh + 1) * V_HEAD, :], ones], axis=0)
            acc_sc[hh] = scale[hh] * acc_sc[hh] + jnp.dot(vte, p_sc[hh], preferred_element_type=F32)

    scores(0, sa_sc)

    def pair(pp, c):
        j = 2 * pp
        scores(j + 1, sb_sc)
        update(j, sa_sc, False)
        scores(j + 2, sa_sc)
        update(j + 1, sb_sc, False)
        return c

    npair = qi // 2
    lax.fori_loop(0, npair, pair, 0)
    j0 = 2 * npair

    @pl.when(qi % 2 == 0)
    def _():
        update(j0, sa_sc, True)

    @pl.when(qi % 2 == 1)
    def _():
        scores(j0 + 1, sb_sc)
        update(j0, sa_sc, False)
        update(j0 + 1, sb_sc, True)

    outs = []
    for pr in range(nh // 2):
        halves = []
        for hh in (2 * pr, 2 * pr + 1):
            acc = acc_sc[hh]
            halves.append(acc[:V_HEAD] * (1.0 / acc[V_HEAD:V_HEAD + 1]))
        outs.append(jnp.concatenate(halves, axis=0).T)
    o_ref[...] = jnp.concatenate(outs, axis=1).astype(BF16)


def _attention(q, k, vt, B, S):
    nq = S // TQ
    nh = ATTN_HEADS
    hg = MLA_HEADS // nh
    return pl.pallas_call(
        _attn_kernel,
        out_shape=jax.ShapeDtypeStruct((B * S, MLA_HEADS * V_HEAD), BF16),
        grid=(B, hg, nq),
        in_specs=[pl.BlockSpec((TQ, nh * HEAD_PAD), lambda b, h, i: (b * nq + i, h)),
                  pl.BlockSpec((S, nh * HEAD_PAD), lambda b, h, i: (b, h)),
                  pl.BlockSpec((None, nq, nh * V_HEAD, TQ), lambda b, h, i: (b, 0, h, 0))],
        out_specs=pl.BlockSpec((TQ, nh * V_HEAD), lambda b, h, i: (b * nq + i, h)),
        scratch_shapes=[pltpu.VMEM((nh, 1, TQ), F32),
                        pltpu.VMEM((nh, V_HEAD + DEN_ROWS, TQ), F32),
                        pltpu.VMEM((nh, TQ, TQ), F32), pltpu.VMEM((nh, TQ, TQ), F32),
                        pltpu.VMEM((nh, TQ, TQ), BF16)],
        compiler_params=_cparams(3),
        name="attention",
    )(q, k, vt)


def _merge_kernel(x_ref, ada_ref, pc_ref, sgs_ref, sga_ref, ys_ref, ya_ref,
                  w_ups, w_upa, w_o, lng, lnb, rw, rb, ws1, ws3, ws2,
                  base_ref, hr_ref, rt_ref, *, alpha):
    D = x_ref.shape[1]
    tm = x_ref.shape[0]
    ada = ada_ref[...]
    merged = (pc_ref[...].astype(F32)
              + sgs_ref[...].astype(F32) * jnp.dot(ys_ref[...], w_ups[...], preferred_element_type=F32)
              + sga_ref[...].astype(F32) * jnp.dot(ya_ref[...], w_upa[...], preferred_element_type=F32))
    y = _bdot(merged, w_o[...])
    x1 = _ln_plain(alpha * x_ref[...] + ada[2:3, :] * y) * lng[...] + lnb[...]
    h2 = _ln_plain(x1) * (1.0 + ada[4:5, :]) + ada[3:4, :]
    hr_ref[:, :D] = h2

    G = N_EXPERT_GROUPS
    logits = lax.dot_general(rw[...], h2, (((1,), (1,)), ((), ())),
                             precision=lax.Precision.HIGHEST, preferred_element_type=F32)
    sc0 = jax.nn.sigmoid(logits)
    sl0 = sc0 + rb[...]
    sc = [sc0[j * G:(j + 1) * G] for j in range(EXPERTS_PER_GROUP)]
    sl = [sl0[j * G:(j + 1) * G] for j in range(EXPERTS_PER_GROUP)]
    hi1, lo1 = jnp.maximum(sl[0], sl[1]), jnp.minimum(sl[0], sl[1])
    hi2, lo2 = jnp.maximum(sl[2], sl[3]), jnp.minimum(sl[2], sl[3])
    gscore = jnp.maximum(hi1, hi2) + jnp.maximum(jnp.minimum(hi1, hi2), jnp.maximum(lo1, lo2))
    sub = lax.broadcasted_iota(jnp.int32, (G, tm), 0)
    gmax = jnp.max(gscore, axis=0, keepdims=True)
    gidx = jnp.min(jnp.where(gscore == gmax, sub, G), axis=0, keepdims=True)
    pick = sub == gidx
    vs = [jnp.sum(jnp.where(pick, a, 0.0), axis=0, keepdims=True) for a in sl]
    ss = [jnp.sum(jnp.where(pick, a, 0.0), axis=0, keepdims=True) for a in sc]
    chosen = []
    for j in range(EXPERTS_PER_GROUP):
        cnt = jnp.zeros((1, tm), jnp.int32)
        for k in range(EXPERTS_PER_GROUP):
            if k == j:
                continue
            beats = (vs[k] > vs[j]) | ((vs[k] == vs[j]) & (k < j))
            cnt = cnt + beats.astype(jnp.int32)
        chosen.append(cnt < 2)
    wsel = [jnp.where(chosen[j], ss[j], 0.0) for j in range(EXPERTS_PER_GROUP)]
    wsum = wsel[0] + wsel[1] + wsel[2] + wsel[3]
    rows = [wsel[j] / wsum for j in range(EXPERTS_PER_GROUP)] + [gidx.astype(F32)]
    rows += [jnp.zeros((1, tm), F32)] * (8 - len(rows))
    route_t = jnp.concatenate(rows, axis=0)
    rt_ref[...] = route_t
    eye = (lax.broadcasted_iota(jnp.int32, (8, ROUTE_W), 0)
           == lax.broadcasted_iota(jnp.int32, (8, ROUTE_W), 1)).astype(F32)
    hr_ref[:, D:] = lax.dot_general(route_t, eye, (((0,), (0,)), ((), ())),
                                    precision=lax.Precision.HIGHEST, preferred_element_type=F32)

    h2b = h2.astype(BF16)
    a1 = jnp.dot(h2b, ws1[...], preferred_element_type=F32)
    a3 = jnp.dot(h2b, ws3[...], preferred_element_type=F32)
    ysh = _bdot(a1 * jax.nn.sigmoid(a1) * a3, ws2[...])
    base_ref[...] = alpha * x1 + ada[5:6, :] * ysh


def _merge(x2d, ada_l, pc, sgs, sga, ys, ya, wts, B, S, alpha):
    T, D = x2d.shape
    tm = TM_PROJ
    ns = S // tm
    row = lambda b, s: (b * ns + s, 0)
    names = ["w_ups", "w_upa", "w_o", "lng0", "lnb0", "rw", "rb", "ws1", "ws3", "ws2"]
    ws = [wts[n] for n in names]
    full = lambda a: pl.BlockSpec(a.shape, lambda b, s: (0,) * a.ndim, pipeline_mode=pl.Buffered(1))
    return pl.pallas_call(
        functools.partial(_merge_kernel, alpha=alpha),
        out_shape=(jax.ShapeDtypeStruct((T, D), F32),
                   jax.ShapeDtypeStruct((T, D + ROUTE_W), F32),
                   jax.ShapeDtypeStruct((8, T), F32)),
        grid=(B, ns),
        in_specs=[pl.BlockSpec((tm, D), row),
                  pl.BlockSpec((None, 6, D), lambda b, s: (b, 0, 0)),
                  pl.BlockSpec((tm, D), row), pl.BlockSpec((tm, D), row), pl.BlockSpec((tm, D), row),
                  pl.BlockSpec((tm, D_SSM), row),
                  pl.BlockSpec((tm, MLA_HEADS * V_HEAD), row)]
                 + [full(a) for a in ws],
        out_specs=(pl.BlockSpec((tm, D), row), pl.BlockSpec((tm, D + ROUTE_W), row),
                   pl.BlockSpec((8, tm), lambda b, s: (0, b * ns + s))),
        compiler_params=_cparams(2),
        name="merge",
    )(x2d, ada_l, pc, sgs, sga, ys, ya, *ws)


def _route_plan(gid, T):
    R = MOE_ROWS
    G = N_EXPERT_GROUPS
    onehot = (gid[:, None] == jnp.arange(G, dtype=jnp.int32)[None, :]).astype(jnp.int32)
    csum = jnp.cumsum(onehot, axis=0)
    counts = csum[-1]
    rank = jnp.sum(onehot * csum, axis=1) - 1
    padded = (counts + R - 1) // R * R
    pad_end = jnp.cumsum(padded)
    pad_start = pad_end - padded
    dest = jnp.sum(onehot * pad_start[None, :], axis=1) + rank
    nb = T // R + G
    starts = jnp.arange(nb, dtype=jnp.int32) * R
    block_g = jnp.minimum(jnp.sum((starts[:, None] >= pad_end[None, :]).astype(jnp.int32), axis=1), G - 1)
    gsel = (block_g[:, None] == jnp.arange(G, dtype=jnp.int32)[None, :]).astype(jnp.int32)
    group_end = jnp.sum(gsel * (pad_start + counts)[None, :], axis=1)
    nvalid = jnp.clip(group_end - starts, 0, R)
    return dest.astype(jnp.int32), block_g.astype(jnp.int32), nvalid.astype(jnp.int32)


def _permute_kernel(dest_ref, nvalid_ref, hr_ref, hs_hbm, zbuf, sem, zsem):
    i = pl.program_id(0)
    tp = hr_ref.shape[0]
    R = zbuf.shape[0]
    nb = nvalid_ref.shape[0]

    @pl.when(i == 0)
    def _():
        zbuf[...] = jnp.zeros(zbuf.shape, F32)

        def fill(b, n):
            partial = nvalid_ref[b] < R

            @pl.when(partial)
            def _():
                pltpu.make_async_copy(zbuf, hs_hbm.at[pl.ds(b * R, R)], zsem).start()

            return n + partial.astype(jnp.int32)

        nfill = lax.fori_loop(0, nb, fill, 0)

        def drain(k, c):
            pltpu.make_async_copy(zbuf, hs_hbm.at[pl.ds(0, R)], zsem).wait()
            return c

        lax.fori_loop(0, nfill, drain, 0)

    def body(g, c):
        r0 = pl.multiple_of(g * 8, 8)
        for k in range(8):
            d = dest_ref[i * tp + r0 + k]
            pltpu.make_async_copy(hr_ref.at[pl.ds(r0 + k, 1)], hs_hbm.at[pl.ds(d, 1)], sem).start()
        return c

    lax.fori_loop(0, tp // 8, body, 0)
    pltpu.make_async_copy(hr_ref, hs_hbm.at[pl.ds(0, tp)], sem).wait()


def _permute(hr, dest, nvalid):
    T, W = hr.shape
    R = MOE_ROWS
    tp = PERM_ROWS
    nb = nvalid.shape[0]
    return pl.pallas_call(
        _permute_kernel,
        out_shape=jax.ShapeDtypeStruct((nb * R, W), F32),
        grid_spec=pltpu.PrefetchScalarGridSpec(
            num_scalar_prefetch=2,
            grid=(T // tp,),
            in_specs=[pl.BlockSpec((tp, W), lambda i, d, n: (i, 0))],
            out_specs=pl.BlockSpec(memory_space=pl.ANY),
            scratch_shapes=[pltpu.VMEM((R, W), F32), pltpu.SemaphoreType.DMA, pltpu.SemaphoreType.DMA]),
        compiler_params=_cparams(1),
        name="permute",
    )(dest, nvalid, hr)


def _moe_kernel(bg_ref, nvalid_ref, hs_ref, w1_ref, w3_ref, w2_ref, y_ref, w1b, w3b, w2b):
    i = pl.program_id(0)
    R, D = y_ref.shape

    @pl.when((i == 0) | (bg_ref[i] != bg_ref[jnp.maximum(i - 1, 0)]))
    def _():
        w1b[...] = w1_ref[...].astype(BF16)
        w3b[...] = w3_ref[...].astype(BF16)
        w2b[...] = w2_ref[...].astype(BF16)

    @pl.when(nvalid_ref[i] > 0)
    def _():
        xb = hs_ref[:, :D].astype(BF16)
        acc = jnp.zeros((R, D), F32)
        for j in range(EXPERTS_PER_GROUP):
            a1 = jnp.dot(xb, w1b[j], preferred_element_type=F32)
            a3 = jnp.dot(xb, w3b[j], preferred_element_type=F32)
            hj = a1 * jax.nn.sigmoid(a1) * a3 * hs_ref[:, D + j:D + j + 1]
            acc = acc + _bdot(hj, w2b[j])
        y_ref[...] = acc

    @pl.when(nvalid_ref[i] == 0)
    def _():
        y_ref[...] = jnp.zeros((R, D), F32)


def _moe(hs, block_g, nvalid, w1, w3, w2, layer):
    D = w1.shape[2]
    W = hs.shape[1]
    R = MOE_ROWS
    nb = nvalid.shape[0]
    wspec_in = pl.BlockSpec((None, EXPERTS_PER_GROUP, D, D_EXPERT), lambda i, bg, nv: (layer, bg[i], 0, 0))
    wspec_out = pl.BlockSpec((None, EXPERTS_PER_GROUP, D_EXPERT, D), lambda i, bg, nv: (layer, bg[i], 0, 0))
    return pl.pallas_call(
        _moe_kernel,
        out_shape=jax.ShapeDtypeStruct((nb * R, D), F32),
        grid_spec=pltpu.PrefetchScalarGridSpec(
            num_scalar_prefetch=2,
            grid=(nb,),
            in_specs=[pl.BlockSpec((R, W), lambda i, bg, nv: (i, 0)), wspec_in, wspec_in, wspec_out],
            out_specs=pl.BlockSpec((R, D), lambda i, bg, nv: (i, 0)),
            scratch_shapes=[pltpu.VMEM((EXPERTS_PER_GROUP, D, D_EXPERT), BF16),
                            pltpu.VMEM((EXPERTS_PER_GROUP, D, D_EXPERT), BF16),
                            pltpu.VMEM((EXPERTS_PER_GROUP, D_EXPERT, D), BF16)]),
        compiler_params=_cparams(1),
        name="moe",
    )(block_g, nvalid, hs, w1, w3, w2)


def _final_kernel(dest_ref, base_ref, ys_hbm, ada_ref, lng, lnb, o_ref, ybuf, sem):
    i = pl.program_id(0)
    n = pl.num_programs(0)
    tf = base_ref.shape[0]
    slot = i % 2

    def start_gather(step, s):
        def body(r, c):
            d = dest_ref[step * tf + r]
            pltpu.make_async_copy(ys_hbm.at[pl.ds(d, 1)], ybuf.at[s, pl.ds(r, 1)], sem.at[s]).start()
            return c

        lax.fori_loop(0, tf, body, 0, unroll=8)

    @pl.when(i == 0)
    def _():
        start_gather(0, 0)

    @pl.when(i + 1 < n)
    def _():
        start_gather(i + 1, 1 - slot)

    pltpu.make_async_copy(ys_hbm.at[pl.ds(0, tf)], ybuf.at[slot], sem.at[slot]).wait()
    ada = ada_ref[...]
    o_ref[...] = _ln_plain(base_ref[...] + ada[5:6, :] * ybuf[slot]) * lng[...] + lnb[...]


def _final(base, ys, dest, ada_l, lng, lnb, B, S):
    T, D = base.shape
    tf = FINAL_ROWS
    per_batch = S // tf
    return pl.pallas_call(
        _final_kernel,
        out_shape=jax.ShapeDtypeStruct((T, D), F32),
        grid_spec=pltpu.PrefetchScalarGridSpec(
            num_scalar_prefetch=1,
            grid=(T // tf,),
            in_specs=[pl.BlockSpec((tf, D), lambda i, d: (i, 0)),
                      pl.BlockSpec(memory_space=pl.ANY),
                      pl.BlockSpec((None, 6, D), lambda i, d: (i // per_batch, 0, 0)),
                      pl.BlockSpec((1, D), lambda i, d: (0, 0)), pl.BlockSpec((1, D), lambda i, d: (0, 0))],
            out_specs=pl.BlockSpec((tf, D), lambda i, d: (i, 0)),
            scratch_shapes=[pltpu.VMEM((2, tf, D), F32), pltpu.SemaphoreType.DMA((2,))]),
        compiler_params=_cparams(1),
        name="final_ln",
    )(dest, base, ys, ada_l, lng, lnb)


def _rope_tables(positions):
    inv_freq = ROPE_THETA ** (-jnp.arange(0, QK_ROPE, 2, dtype=F32) / QK_ROPE)
    ang = positions.astype(F32)[..., None] * inv_freq
    dense = lax.optimization_barrier(ang.reshape(-1, LANE))
    cos, sin = lax.optimization_barrier((jnp.cos(dense), jnp.sin(dense)))
    cos, sin = cos.reshape(ang.shape), sin.reshape(ang.shape)
    ones = jnp.ones(cos.shape[:-1] + (QK_NOPE,), F32)
    zpad = jnp.zeros(cos.shape[:-1] + (HEAD_PAD - QK_NOPE - QK_ROPE,), F32)
    cos128 = jnp.concatenate([ones, cos, cos, zpad], axis=-1)
    sin128 = jnp.concatenate([0.0 * ones, -sin, sin, zpad], axis=-1)
    T = positions.shape[0] * positions.shape[1]
    return cos128.reshape(T, HEAD_PAD), sin128.reshape(T, HEAD_PAD)


def _swap_halves(w):
    half = w.shape[-1] // 2
    return jnp.concatenate([w[..., half:], w[..., :half]], axis=-1)


def _layer_weights(l, p):
    D = p["w_in"].shape[1]
    w_in = p["w_in"][l]
    o = 0
    seg = {}
    for name, width in (("cv", 3 * D_CONV), ("ssm", D_SSM), ("cq", Q_LORA), ("ckv", KV_LORA),
                        ("kr", QK_ROPE), ("g", 3 * D)):
        seg[name] = w_in[:, o:o + width]
        o += width
    pad_r = HEAD_PAD - QK_NOPE - QK_ROPE
    zl = jnp.zeros((D, QK_NOPE), F32)
    zr = jnp.zeros((D, pad_r), F32)
    w = {
        "w_cv": seg["cv"].astype(BF16), "w_ssm": seg["ssm"].astype(BF16),
        "w_lat": jnp.concatenate([seg["cq"], seg["ckv"], zl, seg["kr"], zr,
                                  zl, _swap_halves(seg["kr"]), zr], axis=1).astype(BF16),
        "w_g": seg["g"].astype(BF16),
        "convw": p["conv_w"][l],
        "qn": p["q_norm"][l].reshape(1, Q_LORA), "kvn": p["kv_norm"][l].reshape(1, KV_LORA),
        "w_upc": p["w_up_conv"][l].astype(BF16),
    }
    scale = (QK_NOPE + QK_ROPE) ** -0.5 * math.log2(math.e)
    wq = p["w_uq"][l].reshape(Q_LORA, MLA_HEADS, QK_NOPE + QK_ROPE) * scale
    zq = jnp.zeros((Q_LORA, MLA_HEADS, pad_r), F32)
    zn = jnp.zeros((Q_LORA, MLA_HEADS, QK_NOPE), F32)
    w["wq_a"] = jnp.concatenate([wq, zq], axis=-1).reshape(Q_LORA, -1).astype(BF16)
    w["wq_b"] = jnp.concatenate([zn, _swap_halves(wq[..., QK_NOPE:]), zq],
                                axis=-1).reshape(Q_LORA, -1).astype(BF16)
    wk = p["w_uk"][l].reshape(KV_LORA, MLA_HEADS, QK_NOPE)
    zk = jnp.zeros((KV_LORA, MLA_HEADS, HEAD_PAD - QK_NOPE), F32)
    w["wk"] = jnp.concatenate([wk, zk], axis=-1).reshape(KV_LORA, -1).astype(BF16)
    w["wv"] = p["w_uv"][l].T.astype(BF16)

    ar, ai = p["ssm_a_re"][l], p["ssm_a_im"][l]
    dt = jnp.exp(p["ssm_log_dt"][l])[:, None]
    mag = jnp.exp(ar * dt)
    lb_re, lb_im = mag * jnp.cos(ai * dt), mag * jnp.sin(ai * dt)
    den = ar * ar + ai * ai
    nr, ni = lb_re - 1.0, lb_im
    f_re, f_im = (nr * ar + ni * ai) / den, (ni * ar - nr * ai) / den
    br, bi = p["ssm_b_re"][l], p["ssm_b_im"][l]
    bb_re = f_re[..., None] * br - f_im[..., None] * bi
    bb_im = f_re[..., None] * bi + f_im[..., None] * br
    gpc = LANE // SSM_GROUP_DIM
    nch = SSM_GROUPS // gpc
    eye = jnp.eye(gpc, dtype=F32)

    def bmat(bb):
        return jnp.einsum("cgnp,gh->cgphn", bb.reshape(nch, gpc, SSM_STATE, SSM_GROUP_DIM),
                          eye).reshape(nch, LANE, gpc * SSM_STATE).astype(BF16)

    def cmat(cc):
        return jnp.einsum("cgpn,gh->cgnhp", cc.reshape(nch, gpc, SSM_GROUP_DIM, SSM_STATE),
                          eye).reshape(nch, gpc * SSM_STATE, LANE).astype(BF16)

    w["ssm_bre"], w["ssm_bim"] = bmat(bb_re), bmat(bb_im)
    w["ssm_cre"], w["ssm_cim"] = cmat(p["ssm_c_re"][l]), cmat(-p["ssm_c_im"][l])
    nrow = 8
    w["ssm_lre"] = jnp.broadcast_to(lb_re.reshape(1, -1), (nrow, SSM_GROUPS * SSM_STATE))
    w["ssm_lim"] = jnp.broadcast_to(lb_im.reshape(1, -1), (nrow, SSM_GROUPS * SSM_STATE))
    w["ssm_d"] = p["ssm_d"][l].reshape(1, D_SSM)
    w["ssm_wglu"] = p["ssm_w_glu"][l].astype(BF16)

    w["w_ups"] = p["w_up_ssm"][l].astype(BF16)
    w["w_upa"] = p["w_up_attn"][l].astype(BF16)
    w["w_o"] = p["w_o"][l].astype(BF16)
    w["lng0"], w["lnb0"] = p["ln_g"][l, 0].reshape(1, D), p["ln_b"][l, 0].reshape(1, D)
    w["lng1"], w["lnb1"] = p["ln_g"][l, 1].reshape(1, D), p["ln_b"][l, 1].reshape(1, D)
    rwg = p["router_w"].reshape(D, N_EXPERT_GROUPS, EXPERTS_PER_GROUP)
    rbg = p["router_bias"].reshape(N_EXPERT_GROUPS, EXPERTS_PER_GROUP)
    w["rw"] = jnp.transpose(rwg, (2, 1, 0)).reshape(N_EXPERTS, D)
    w["rb"] = jnp.transpose(rbg, (1, 0)).reshape(N_EXPERTS, 1)
    w["ws1"] = p["shared_w1"][l].astype(BF16)
    w["ws3"] = p["shared_w3"][l].astype(BF16)
    w["ws2"] = p["shared_w2"][l].astype(BF16)
    return w


def kernel(x, c, positions, w_ada, b_ada, w_in, conv_w, ssm_a_re, ssm_a_im, ssm_b_re, ssm_b_im,
           ssm_c_re, ssm_c_im, ssm_d, ssm_log_dt, ssm_w_glu, q_norm, w_uq, kv_norm, w_uk, w_uv,
           w_up_conv, w_up_ssm, w_up_attn, w_o, ln_g, ln_b, router_w, router_bias,
           exp_w1, exp_w3, exp_w2, shared_w1, shared_w3, shared_w2):
    B, S, D = x.shape
    L = w_in.shape[0]
    T = B * S
    assert B == 8 and S % max(TM_PROJ, T_CHUNK, TQ, FINAL_ROWS) == 0 and T % max(MOE_ROWS, PERM_ROWS) == 0
    assert TM_PROJ == TQ
    p = dict(w_in=w_in, conv_w=conv_w, ssm_a_re=ssm_a_re, ssm_a_im=ssm_a_im, ssm_b_re=ssm_b_re,
             ssm_b_im=ssm_b_im, ssm_c_re=ssm_c_re, ssm_c_im=ssm_c_im, ssm_d=ssm_d,
             ssm_log_dt=ssm_log_dt, ssm_w_glu=ssm_w_glu, q_norm=q_norm, w_uq=w_uq, kv_norm=kv_norm,
             w_uk=w_uk, w_uv=w_uv, w_up_conv=w_up_conv, w_up_ssm=w_up_ssm, w_up_attn=w_up_attn,
             w_o=w_o, ln_g=ln_g, ln_b=ln_b, router_w=router_w, router_bias=router_bias,
             exp_w1=exp_w1, exp_w3=exp_w3, exp_w2=exp_w2, shared_w1=shared_w1,
             shared_w3=shared_w3, shared_w2=shared_w2)
    alpha = (2 * L) ** 0.25
    cos128, sin128 = _rope_tables(positions)
    ada = _ada(c, w_ada, b_ada).reshape(L, B, 6, D)
    x2d = x.reshape(T, D)
    for l in range(L):
        wts = _layer_weights(l, p)
        pc, sgs, sga, u, q, k, v = _inproj(x2d, ada[l], cos128, sin128, wts, B, S)
        ys = _ssm(u.reshape(B, S, D_SSM), wts, B, S).reshape(T, D_SSM)
        ya = _attention(q, k, v, B, S)
        base, hr, rt = _merge(x2d, ada[l], pc, sgs, sga, ys, ya, wts, B, S, alpha)
        gid = rt[EXPERTS_PER_GROUP].astype(jnp.int32)
        dest, block_g, nvalid = _route_plan(gid, T)
        hs = _permute(hr, dest, nvalid)
        ysort = _moe(hs, block_g, nvalid, exp_w1, exp_w3, exp_w2, l)
        x2d = _final(base, ysort, dest, ada[l], wts["lng1"], wts["lnb1"], B, S)
    return x2d.reshape(B, S, D)
```

```python
import functools
import math

import jax
import jax.numpy as jnp
from jax import lax
from jax.experimental import pallas as pl
from jax.experimental.pallas import tpu as pltpu

F32 = jnp.float32
BF16 = jnp.bfloat16

D_CONV = 512
D_SSM = 512
SSM_GROUP_DIM = 16
SSM_GROUPS = 32
SSM_STATE = 64
MLA_HEADS = 8
QK_NOPE = 64
QK_ROPE = 32
V_HEAD = 64
Q_LORA = 256
KV_LORA = 128
ROPE_THETA = 10000.0
N_EXPERTS = 32
N_EXPERT_GROUPS = 8
EXPERTS_PER_GROUP = 4
D_EXPERT = 256
LN_EPS = 1e-5
RMS_EPS = 1e-6

LANE = 128
HEAD_PAD = 128
VMEM_LIMIT = 56 * 1024 * 1024

TM_PROJ = 512
T_CHUNK = 64
TQ = 512
ATTN_HEADS = 4
DEN_ROWS = 16
MOE_ROWS = 512
PERM_ROWS = 1024
FINAL_ROWS = 512
ROUTE_W = 128


def _cparams(n_axes):
    return pltpu.CompilerParams(dimension_semantics=("arbitrary",) * n_axes,
                                vmem_limit_bytes=VMEM_LIMIT)


def _ln_plain(x):
    mu = jnp.mean(x, axis=-1, keepdims=True)
    xc = x - mu
    var = jnp.mean(xc * xc, axis=-1, keepdims=True)
    return xc * lax.rsqrt(var + LN_EPS)


def _rms(x, g):
    return x * lax.rsqrt(jnp.mean(x * x, axis=-1, keepdims=True) + RMS_EPS) * g


def _bdot(a, b):
    return jnp.dot(a.astype(BF16), b, preferred_element_type=F32)


def _ada_kernel(c_ref, w_ref, b_ref, o_ref):
    c = c_ref[...]
    cond = c * jax.nn.sigmoid(c)
    o_ref[...] = jnp.dot(cond, w_ref[...], precision=lax.Precision.HIGHEST,
                         preferred_element_type=F32) + b_ref[...]


def _ada(c, w_ada, b_ada):
    L, D, D6 = w_ada.shape
    B = c.shape[0]
    wn = D6 // 2
    nj = D6 // wn
    return pl.pallas_call(
        _ada_kernel,
        out_shape=jax.ShapeDtypeStruct((L, B, D6), F32),
        grid=(L, nj),
        in_specs=[pl.BlockSpec((B, D), lambda l, j: (0, 0)),
                  pl.BlockSpec((None, D, wn), lambda l, j: (l, 0, j)),
                  pl.BlockSpec((None, 1, wn), lambda l, j: (l, 0, j))],
        out_specs=pl.BlockSpec((None, B, wn), lambda l, j: (l, 0, j)),
        compiler_params=_cparams(2),
        name="ada",
    )(c, w_ada, b_ada.reshape(L, 1, D6))


def _inproj_kernel(x_ref, ada_ref, cos_ref, sin_ref,
                   w_cv, w_ssm, w_lat, w_g,
                   convw, qn, wq_a, wq_b, kvn, wk, wv, w_upc,
                   pc_ref, sgs_ref, sga_ref, u_ref, q_ref, k_ref, v_ref,
                   carry_ref):
    si = pl.program_id(1)
    D = x_ref.shape[1]
    tm = x_ref.shape[0]

    @pl.when(si == 0)
    def _():
        carry_ref[...] = jnp.zeros_like(carry_ref)

    ada = ada_ref[...]
    h = _ln_plain(x_ref[...]) * (1.0 + ada[1:2, :]) + ada[0:1, :]
    hb = h.astype(BF16)

    pcv = jnp.dot(hb, w_cv[...], preferred_element_type=F32)
    u = pcv[:, D_CONV:2 * D_CONV] * pcv[:, :D_CONV]
    gb = pcv[:, 2 * D_CONV:]
    tail = carry_ref[...]
    row8 = lax.broadcasted_iota(jnp.int32, (8, D_CONV), 0)
    r1 = pltpu.roll(u, 1, 0)
    r2 = pltpu.roll(u, 2, 0)
    t1 = pltpu.roll(tail, 1, 0)
    t2 = pltpu.roll(tail, 2, 0)
    u1 = jnp.concatenate([jnp.where(row8 < 1, t1, r1[:8]), r1[8:]], axis=0)
    u2 = jnp.concatenate([jnp.where(row8 < 2, t2, r2[:8]), r2[8:]], axis=0)
    cw = convw[...]
    yc = gb * (cw[0:1, :] * u2 + cw[1:2, :] * u1 + cw[2:3, :] * u)
    carry_ref[...] = u[tm - 8:, :]

    gates = jnp.dot(hb, w_g[...], preferred_element_type=F32)
    pc_ref[...] = (jax.nn.sigmoid(gates[:, :D]) * _bdot(yc, w_upc[...])).astype(BF16)
    sgs_ref[...] = jax.nn.sigmoid(gates[:, D:2 * D]).astype(BF16)
    sga_ref[...] = jax.nn.sigmoid(gates[:, 2 * D:]).astype(BF16)

    u_ref[...] = jnp.dot(hb, w_ssm[...], preferred_element_type=F32).astype(BF16)

    cos = cos_ref[...]
    sin = sin_ref[...]
    lat = jnp.dot(hb, w_lat[...], preferred_element_type=F32)
    o_kv, o_kr = Q_LORA, Q_LORA + KV_LORA
    cq = _rms(lat[:, :o_kv], qn[...]).astype(BF16)
    qa = jnp.dot(cq, wq_a[...], preferred_element_type=F32)
    qb = jnp.dot(cq, wq_b[...], preferred_element_type=F32)
    ckv = _rms(lat[:, o_kv:o_kr], kvn[...]).astype(BF16)
    kn = jnp.dot(ckv, wk[...], preferred_element_type=F32)
    kr = lat[:, o_kr:o_kr + HEAD_PAD] * cos + lat[:, o_kr + HEAD_PAD:] * sin
    for hd in range(MLA_HEADS):
        sl = slice(hd * HEAD_PAD, (hd + 1) * HEAD_PAD)
        q_ref[:, sl] = (qa[:, sl] * cos + qb[:, sl] * sin).astype(BF16)
        k_ref[:, sl] = (kn[:, sl] + kr).astype(BF16)
    v_ref[...] = lax.dot_general(wv[...], ckv, (((1,), (1,)), ((), ())),
                                 preferred_element_type=F32).astype(BF16)


def _inproj(x2d, ada_l, cos128, sin128, wts, B, S):
    T, D = x2d.shape
    tm = TM_PROJ
    ns = S // tm
    row = lambda b, s: (b * ns + s, 0)
    full = lambda a: pl.BlockSpec(a.shape, lambda b, s: (0,) * a.ndim, pipeline_mode=pl.Buffered(1))
    names = ["w_cv", "w_ssm", "w_lat", "w_g", "convw", "qn",
             "wq_a", "wq_b", "kvn", "wk", "wv", "w_upc"]
    ws = [wts[n] for n in names]
    out_shape = (
        jax.ShapeDtypeStruct((T, D), BF16),
        jax.ShapeDtypeStruct((T, D), BF16),
        jax.ShapeDtypeStruct((T, D), BF16),
        jax.ShapeDtypeStruct((T, D_SSM), BF16),
        jax.ShapeDtypeStruct((T, MLA_HEADS * HEAD_PAD), BF16),
        jax.ShapeDtypeStruct((T, MLA_HEADS * HEAD_PAD), BF16),
        jax.ShapeDtypeStruct((B, ns, MLA_HEADS * V_HEAD, tm), BF16),
    )
    out_specs = (
        pl.BlockSpec((tm, D), row), pl.BlockSpec((tm, D), row), pl.BlockSpec((tm, D), row),
        pl.BlockSpec((tm, D_SSM), row),
        pl.BlockSpec((tm, MLA_HEADS * HEAD_PAD), row),
        pl.BlockSpec((tm, MLA_HEADS * HEAD_PAD), row),
        pl.BlockSpec((None, None, MLA_HEADS * V_HEAD, tm), lambda b, s: (b, s, 0, 0)),
    )
    return pl.pallas_call(
        _inproj_kernel,
        out_shape=out_shape,
        grid=(B, ns),
        in_specs=[pl.BlockSpec((tm, D), row),
                  pl.BlockSpec((None, 6, D), lambda b, s: (b, 0, 0)),
                  pl.BlockSpec((tm, LANE), row), pl.BlockSpec((tm, LANE), row)]
                 + [full(a) for a in ws],
        out_specs=out_specs,
        scratch_shapes=[pltpu.VMEM((8, D_CONV), F32)],
        compiler_params=_cparams(2),
        name="inproj",
    )(x2d, ada_l, cos128, sin128, *ws)


def _ssm_kernel(u_ref, bre_ref, bim_ref, lre_ref, lim_ref, cre_ref, cim_ref, d_ref, wglu_ref, perm_ref,
                o_ref, are, aim, au, bre_s, bim_s, bu, st_re, st_im, tmaj):
    i = pl.program_id(0)
    batch, nsteps = u_ref.shape[0], u_ref.shape[1]
    rows = batch * nsteps
    nchunk = bre_ref.shape[0]
    cw = bre_ref.shape[2]

    @pl.when(i == 0)
    def _():
        st_re[...] = jnp.zeros_like(st_re)
        st_im[...] = jnp.zeros_like(st_im)
        bre_s[...] = jnp.zeros_like(bre_s)
        bim_s[...] = jnp.zeros_like(bim_s)
        bu[...] = jnp.zeros_like(bu)

    def body(cur_re, cur_im, cur_u, prev_re, prev_im, prev_u):
        for b in range(batch):
            tmaj[:, b, :] = u_ref[b].astype(F32)
        uf = tmaj[...].reshape(rows, D_SSM)
        cur_u[...] = uf
        ub = uf.astype(BF16)
        for c in range(nchunk):
            uc = ub[:, c * LANE:(c + 1) * LANE]
            cur_re[:, c * cw:(c + 1) * cw] = jnp.dot(uc, bre_ref[c], preferred_element_type=F32)
            cur_im[:, c * cw:(c + 1) * cw] = jnp.dot(uc, bim_ref[c], preferred_element_type=F32)

        ys = []

        def out_chunk(c):
            ys.append(_bdot(prev_re[:, c * cw:(c + 1) * cw], cre_ref[c])
                      + _bdot(prev_im[:, c * cw:(c + 1) * cw], cim_ref[c]))

        def out_finish():
            y = jnp.concatenate(ys, axis=1) + d_ref[...] * prev_u[...]
            y = 0.5 * y * (1.0 + jnp.tanh(math.sqrt(2.0 / math.pi) * (y + 0.044715 * (y * y * y))))
            z = _bdot(y, wglu_ref[...])
            g = (z[:, :D_SSM] * jax.nn.sigmoid(z[:, D_SSM:])).astype(BF16)
            o_ref[...] = jnp.dot(perm_ref[...], g, preferred_element_type=F32).astype(BF16).reshape(
                batch, nsteps, D_SSM)

        pieces = [functools.partial(out_chunk, c) for c in range(nchunk)] + [out_finish]
        every = nsteps // len(pieces)
        pr, pi = st_re[...], st_im[...]
        for t in range(nsteps):
            r = t * batch
            lre = lre_ref[...]
            lim = lim_ref[...]
            nr = lre * pr - lim * pi + cur_re[r:r + batch, :]
            ni = lre * pi + lim * pr + cur_im[r:r + batch, :]
            cur_re[r:r + batch, :] = nr
            cur_im[r:r + batch, :] = ni
            pr, pi = nr, ni
            if t % every == every - 1 and pieces:
                pieces.pop(0)()
        while pieces:
            pieces.pop(0)()
        st_re[...] = pr
        st_im[...] = pi

    @pl.when(i % 2 == 0)
    def _():
        body(are, aim, au, bre_s, bim_s, bu)

    @pl.when(i % 2 == 1)
    def _():
        body(bre_s, bim_s, bu, are, aim, au)


def _ssm(u, wts, B, S):
    rows = T_CHUNK * B
    n = S // T_CHUNK
    nstate = SSM_GROUPS * SSM_STATE
    names = ["ssm_bre", "ssm_bim", "ssm_lre", "ssm_lim", "ssm_cre", "ssm_cim", "ssm_d", "ssm_wglu"]
    r = jnp.arange(rows, dtype=jnp.int32)
    src = (r % T_CHUNK) * B + r // T_CHUNK
    perm = (src[:, None] == jnp.arange(rows, dtype=jnp.int32)[None, :]).astype(BF16)
    ws = [wts[k] for k in names] + [perm]
    full = lambda a: pl.BlockSpec(a.shape, lambda i: (0,) * a.ndim, pipeline_mode=pl.Buffered(1))
    return pl.pallas_call(
        _ssm_kernel,
        out_shape=jax.ShapeDtypeStruct((B, S, D_SSM), BF16),
        grid=(n + 1,),
        in_specs=[pl.BlockSpec((B, T_CHUNK, D_SSM), lambda i: (0, jnp.minimum(i, n - 1), 0))]
                 + [full(a) for a in ws],
        out_specs=pl.BlockSpec((B, T_CHUNK, D_SSM), lambda i: (0, jnp.maximum(i - 1, 0), 0)),
        scratch_shapes=[pltpu.VMEM((rows, nstate), F32), pltpu.VMEM((rows, nstate), F32),
                        pltpu.VMEM((rows, D_SSM), F32),
                        pltpu.VMEM((rows, nstate), F32), pltpu.VMEM((rows, nstate), F32),
                        pltpu.VMEM((rows, D_SSM), F32),
                        pltpu.VMEM((B, nstate), F32), pltpu.VMEM((B, nstate), F32),
                        pltpu.VMEM((T_CHUNK, B, D_SSM), F32)],
        compiler_params=_cparams(1),
        name="ssm",
    )(u, *ws)


def _attn_kernel(q_ref, k_ref, vt_ref, o_ref, m_sc, acc_sc, sa_sc, sb_sc, p_sc):
    qi = pl.program_id(2)
    tq = q_ref.shape[0]
    nh = m_sc.shape[0]
    neg = -1e30
    m_sc[...] = jnp.full(m_sc.shape, neg, F32)
    acc_sc[...] = jnp.zeros(acc_sc.shape, F32)
    ones = jnp.ones((DEN_ROWS, tq), BF16)

    def scores(j, s_sc):
        off = pl.multiple_of(j * tq, tq)
        for hh in range(nh):
            q = q_ref[:, hh * HEAD_PAD:(hh + 1) * HEAD_PAD]
            kb = k_ref[pl.ds(off, tq), hh * HEAD_PAD:(hh + 1) * HEAD_PAD]
            s_sc[hh] = lax.dot_general(kb, q, (((1,), (1,)), ((), ())),
                                       preferred_element_type=F32)

    def update(j, s_sc, masked):
        scale = []
        for hh in range(nh):
            if masked:
                keyi = lax.broadcasted_iota(jnp.int32, (tq, tq), 0)
                qryi = lax.broadcasted_iota(jnp.int32, (tq, tq), 1)
                s_sc[hh] = jnp.where(keyi <= qryi, s_sc[hh], neg)
            m_prev = m_sc[hh]
            m_new = jnp.maximum(m_prev, jnp.max(s_sc[hh], axis=0, keepdims=True))
            m_sc[hh] = m_new
            p_sc[hh] = jnp.exp2(s_sc[hh] - m_new).astype(BF16)
            scale.append(jnp.exp2(m_prev - m_new))
        for hh in range(nh):
            vte = jnp.concatenate([vt_ref[j, hh * V_HEAD:(hh + 1) * V_HEAD, :], ones], axis=0)
            acc_sc[hh] = scale[hh] * acc_sc[hh] + jnp.dot(vte, p_sc[hh], preferred_element_type=F32)

    scores(0, sa_sc)

    def pair(pp, c):
        j = 2 * pp
        scores(j + 1, sb_sc)
        update(j, sa_sc, False)
        scores(j + 2, sa_sc)
        update(j + 1, sb_sc, False)
        return c

    npair = qi // 2
    lax.fori_loop(0, npair, pair, 0)
    j0 = 2 * npair

    @pl.when(qi % 2 == 0)
    def _():
        update(j0, sa_sc, True)

    @pl.when(qi % 2 == 1)
    def _():
        scores(j0 + 1, sb_sc)
        update(j0, sa_sc, False)
        update(j0 + 1, sb_sc, True)

    outs = []
    for pr in range(nh // 2):
        halves = []
        for hh in (2 * pr, 2 * pr + 1):
            acc = acc_sc[hh]
            halves.append(acc[:V_HEAD] * (1.0 / acc[V_HEAD:V_HEAD + 1]))
        outs.append(jnp.concatenate(halves, axis=0).T)
    o_ref[...] = jnp.concatenate(outs, axis=1).astype(BF16)


def _attention(q, k, vt, B, S):
    nq = S // TQ
    nh = ATTN_HEADS
    hg = MLA_HEADS // nh
    return pl.pallas_call(
        _attn_kernel,
        out_shape=jax.ShapeDtypeStruct((B * S, MLA_HEADS * V_HEAD), BF16),
        grid=(B, hg, nq),
        in_specs=[pl.BlockSpec((TQ, nh * HEAD_PAD), lambda b, h, i: (b * nq + i, h)),
                  pl.BlockSpec((S, nh * HEAD_PAD), lambda b, h, i: (b, h)),
                  pl.BlockSpec((None, nq, nh * V_HEAD, TQ), lambda b, h, i: (b, 0, h, 0))],
        out_specs=pl.BlockSpec((TQ, nh * V_HEAD), lambda b, h, i: (b * nq + i, h)),
        scratch_shapes=[pltpu.VMEM((nh, 1, TQ), F32),
                        pltpu.VMEM((nh, V_HEAD + DEN_ROWS, TQ), F32),
                        pltpu.VMEM((nh, TQ, TQ), F32), pltpu.VMEM((nh, TQ, TQ), F32),
                        pltpu.VMEM((nh, TQ, TQ), BF16)],
        compiler_params=_cparams(3),
        name="attention",
    )(q, k, vt)


def _merge_kernel(x_ref, ada_ref, pc_ref, sgs_ref, sga_ref, ys_ref, ya_ref,
                  w_ups, w_upa, w_o, lng, lnb, rw, rb, ws1, ws3, ws2,
                  base_ref, hr_ref, rt_ref, *, alpha):
    D = x_ref.shape[1]
    tm = x_ref.shape[0]
    ada = ada_ref[...]
    merged = (pc_ref[...].astype(F32)
              + sgs_ref[...].astype(F32) * jnp.dot(ys_ref[...], w_ups[...], preferred_element_type=F32)
              + sga_ref[...].astype(F32) * jnp.dot(ya_ref[...], w_upa[...], preferred_element_type=F32))
    y = _bdot(merged, w_o[...])
    x1 = _ln_plain(alpha * x_ref[...] + ada[2:3, :] * y) * lng[...] + lnb[...]
    h2 = _ln_plain(x1) * (1.0 + ada[4:5, :]) + ada[3:4, :]
    hr_ref[:, :D] = h2

    G = N_EXPERT_GROUPS
    logits = lax.dot_general(rw[...], h2, (((1,), (1,)), ((), ())),
                             precision=lax.Precision.HIGHEST, preferred_element_type=F32)
    sc0 = jax.nn.sigmoid(logits)
    sl0 = sc0 + rb[...]
    sc = [sc0[j * G:(j + 1) * G] for j in range(EXPERTS_PER_GROUP)]
    sl = [sl0[j * G:(j + 1) * G] for j in range(EXPERTS_PER_GROUP)]
    hi1, lo1 = jnp.maximum(sl[0], sl[1]), jnp.minimum(sl[0], sl[1])
    hi2, lo2 = jnp.maximum(sl[2], sl[3]), jnp.minimum(sl[2], sl[3])
    gscore = jnp.maximum(hi1, hi2) + jnp.maximum(jnp.minimum(hi1, hi2), jnp.maximum(lo1, lo2))
    sub = lax.broadcasted_iota(jnp.int32, (G, tm), 0)
    gmax = jnp.max(gscore, axis=0, keepdims=True)
    gidx = jnp.min(jnp.where(gscore == gmax, sub, G), axis=0, keepdims=True)
    pick = sub == gidx
    vs = [jnp.sum(jnp.where(pick, a, 0.0), axis=0, keepdims=True) for a in sl]
    ss = [jnp.sum(jnp.where(pick, a, 0.0), axis=0, keepdims=True) for a in sc]
    chosen = []
    for j in range(EXPERTS_PER_GROUP):
        cnt = jnp.zeros((1, tm), jnp.int32)
        for k in range(EXPERTS_PER_GROUP):
            if k == j:
                continue
            beats = (vs[k] > vs[j]) | ((vs[k] == vs[j]) & (k < j))
            cnt = cnt + beats.astype(jnp.int32)
        chosen.append(cnt < 2)
    wsel = [jnp.where(chosen[j], ss[j], 0.0) for j in range(EXPERTS_PER_GROUP)]
    wsum = wsel[0] + wsel[1] + wsel[2] + wsel[3]
    rows = [wsel[j] / wsum for j in range(EXPERTS_PER_GROUP)] + [gidx.astype(F32)]
    rows += [jnp.zeros((1, tm), F32)] * (8 - len(rows))
    route_t = jnp.concatenate(rows, axis=0)
    rt_ref[...] = route_t
    eye = (lax.broadcasted_iota(jnp.int32, (8, ROUTE_W), 0)
           == lax.broadcasted_iota(jnp.int32, (8, ROUTE_W), 1)).astype(F32)
    hr_ref[:, D:] = lax.dot_general(route_t, eye, (((0,), (0,)), ((), ())),
                                    precision=lax.Precision.HIGHEST, preferred_element_type=F32)

    h2b = h2.astype(BF16)
    a1 = jnp.dot(h2b, ws1[...], preferred_element_type=F32)
    a3 = jnp.dot(h2b, ws3[...], preferred_element_type=F32)
    ysh = _bdot(a1 * jax.nn.sigmoid(a1) * a3, ws2[...])
    base_ref[...] = alpha * x1 + ada[5:6, :] * ysh


def _merge(x2d, ada_l, pc, sgs, sga, ys, ya, wts, B, S, alpha):
    T, D = x2d.shape
    tm = TM_PROJ
    ns = S // tm
    row = lambda b, s: (b * ns + s, 0)
    names = ["w_ups", "w_upa", "w_o", "lng0", "lnb0", "rw", "rb", "ws1", "ws3", "ws2"]
    ws = [wts[n] for n in names]
    full = lambda a: pl.BlockSpec(a.shape, lambda b, s: (0,) * a.ndim, pipeline_mode=pl.Buffered(1))
    return pl.pallas_call(
        functools.partial(_merge_kernel, alpha=alpha),
        out_shape=(jax.ShapeDtypeStruct((T, D), F32),
                   jax.ShapeDtypeStruct((T, D + ROUTE_W), F32),
                   jax.ShapeDtypeStruct((8, T), F32)),
        grid=(B, ns),
        in_specs=[pl.BlockSpec((tm, D), row),
                  pl.BlockSpec((None, 6, D), lambda b, s: (b, 0, 0)),
                  pl.BlockSpec((tm, D), row), pl.BlockSpec((tm, D), row), pl.BlockSpec((tm, D), row),
                  pl.BlockSpec((tm, D_SSM), row),
                  pl.BlockSpec((tm, MLA_HEADS * V_HEAD), row)]
                 + [full(a) for a in ws],
        out_specs=(pl.BlockSpec((tm, D), row), pl.BlockSpec((tm, D + ROUTE_W), row),
                   pl.BlockSpec((8, tm), lambda b, s: (0, b * ns + s))),
        compiler_params=_cparams(2),
        name="merge",
    )(x2d, ada_l, pc, sgs, sga, ys, ya, *ws)


def _route_plan(gid, T):
    R = MOE_ROWS
    G = N_EXPERT_GROUPS
    onehot = (gid[:, None] == jnp.arange(G, dtype=jnp.int32)[None, :]).astype(jnp.int32)
    csum = jnp.cumsum(onehot, axis=0)
    counts = csum[-1]
    rank = jnp.sum(onehot * csum, axis=1) - 1
    padded = (counts + R - 1) // R * R
    pad_end = jnp.cumsum(padded)
    pad_start = pad_end - padded
    dest = jnp.sum(onehot * pad_start[None, :], axis=1) + rank
    nb = T // R + G
    starts = jnp.arange(nb, dtype=jnp.int32) * R
    block_g = jnp.minimum(jnp.sum((starts[:, None] >= pad_end[None, :]).astype(jnp.int32), axis=1), G - 1)
    gsel = (block_g[:, None] == jnp.arange(G, dtype=jnp.int32)[None, :]).astype(jnp.int32)
    group_end = jnp.sum(gsel * (pad_start + counts)[None, :], axis=1)
    nvalid = jnp.clip(group_end - starts, 0, R)
    return dest.astype(jnp.int32), block_g.astype(jnp.int32), nvalid.astype(jnp.int32)


def _permute_kernel(dest_ref, nvalid_ref, hr_ref, hs_hbm, zbuf, sem, zsem):
    i = pl.program_id(0)
    tp = hr_ref.shape[0]
    R = zbuf.shape[0]
    nb = nvalid_ref.shape[0]

    @pl.when(i == 0)
    def _():
        zbuf[...] = jnp.zeros(zbuf.shape, F32)

        def fill(b, n):
            partial = nvalid_ref[b] < R

            @pl.when(partial)
            def _():
                pltpu.make_async_copy(zbuf, hs_hbm.at[pl.ds(b * R, R)], zsem).start()

            return n + partial.astype(jnp.int32)

        nfill = lax.fori_loop(0, nb, fill, 0)

        def drain(k, c):
            pltpu.make_async_copy(zbuf, hs_hbm.at[pl.ds(0, R)], zsem).wait()
            return c

        lax.fori_loop(0, nfill, drain, 0)

    def body(g, c):
        r0 = pl.multiple_of(g * 8, 8)
        for k in range(8):
            d = dest_ref[i * tp + r0 + k]
            pltpu.make_async_copy(hr_ref.at[pl.ds(r0 + k, 1)], hs_hbm.at[pl.ds(d, 1)],
                                  sem).start(priority=k % 2)
        return c

    lax.fori_loop(0, tp // 8, body, 0)
    pltpu.make_async_copy(hr_ref, hs_hbm.at[pl.ds(0, tp)], sem).wait()


def _permute(hr, dest, nvalid):
    T, W = hr.shape
    R = MOE_ROWS
    tp = PERM_ROWS
    nb = nvalid.shape[0]
    return pl.pallas_call(
        _permute_kernel,
        out_shape=jax.ShapeDtypeStruct((nb * R, W), F32),
        grid_spec=pltpu.PrefetchScalarGridSpec(
            num_scalar_prefetch=2,
            grid=(T // tp,),
            in_specs=[pl.BlockSpec((tp, W), lambda i, d, n: (i, 0))],
            out_specs=pl.BlockSpec(memory_space=pl.ANY),
            scratch_shapes=[pltpu.VMEM((R, W), F32), pltpu.SemaphoreType.DMA, pltpu.SemaphoreType.DMA]),
        compiler_params=_cparams(1),
        name="permute",
    )(dest, nvalid, hr)


def _moe_kernel(bg_ref, nvalid_ref, hs_ref, w1_ref, w3_ref, w2_ref, y_ref, w1b, w3b, w2b):
    i = pl.program_id(0)
    R, D = y_ref.shape

    @pl.when((i == 0) | (bg_ref[i] != bg_ref[jnp.maximum(i - 1, 0)]))
    def _():
        w1b[...] = w1_ref[...].astype(BF16)
        w3b[...] = w3_ref[...].astype(BF16)
        w2b[...] = w2_ref[...].astype(BF16)

    @pl.when(nvalid_ref[i] > 0)
    def _():
        xb = hs_ref[:, :D].astype(BF16)
        acc = jnp.zeros((R, D), F32)
        for j in range(EXPERTS_PER_GROUP):
            a1 = jnp.dot(xb, w1b[j], preferred_element_type=F32)
            a3 = jnp.dot(xb, w3b[j], preferred_element_type=F32)
            hj = a1 * jax.nn.sigmoid(a1) * a3 * hs_ref[:, D + j:D + j + 1]
            acc = acc + _bdot(hj, w2b[j])
        y_ref[...] = acc

    @pl.when(nvalid_ref[i] == 0)
    def _():
        y_ref[...] = jnp.zeros((R, D), F32)


def _moe(hs, block_g, nvalid, w1, w3, w2, layer):
    D = w1.shape[2]
    W = hs.shape[1]
    R = MOE_ROWS
    nb = nvalid.shape[0]
    wspec_in = pl.BlockSpec((None, EXPERTS_PER_GROUP, D, D_EXPERT), lambda i, bg, nv: (layer, bg[i], 0, 0))
    wspec_out = pl.BlockSpec((None, EXPERTS_PER_GROUP, D_EXPERT, D), lambda i, bg, nv: (layer, bg[i], 0, 0))
    return pl.pallas_call(
        _moe_kernel,
        out_shape=jax.ShapeDtypeStruct((nb * R, D), F32),
        grid_spec=pltpu.PrefetchScalarGridSpec(
            num_scalar_prefetch=2,
            grid=(nb,),
            in_specs=[pl.BlockSpec((R, W), lambda i, bg, nv: (i, 0)), wspec_in, wspec_in, wspec_out],
            out_specs=pl.BlockSpec((R, D), lambda i, bg, nv: (i, 0)),
            scratch_shapes=[pltpu.VMEM((EXPERTS_PER_GROUP, D, D_EXPERT), BF16),
                            pltpu.VMEM((EXPERTS_PER_GROUP, D, D_EXPERT), BF16),
                            pltpu.VMEM((EXPERTS_PER_GROUP, D_EXPERT, D), BF16)]),
        compiler_params=_cparams(1),
        name="moe",
    )(block_g, nvalid, hs, w1, w3, w2)


def _final_kernel(dest_ref, base_ref, ys_hbm, ada_ref, lng, lnb, o_ref, ybuf, sem):
    i = pl.program_id(0)
    n = pl.num_programs(0)
    tf = base_ref.shape[0]
    slot = i % 2

    def start_gather(step, s):
        def body(g, c):
            r0 = pl.multiple_of(g * 8, 8)
            for k in range(8):
                d = dest_ref[step * tf + r0 + k]
                pltpu.make_async_copy(ys_hbm.at[pl.ds(d, 1)], ybuf.at[s, pl.ds(r0 + k, 1)],
                                      sem.at[s]).start(priority=k % 2)
            return c

        lax.fori_loop(0, tf // 8, body, 0)

    @pl.when(i == 0)
    def _():
        start_gather(0, 0)

    @pl.when(i + 1 < n)
    def _():
        start_gather(i + 1, 1 - slot)

    pltpu.make_async_copy(ys_hbm.at[pl.ds(0, tf)], ybuf.at[slot], sem.at[slot]).wait()
    ada = ada_ref[...]
    o_ref[...] = _ln_plain(base_ref[...] + ada[5:6, :] * ybuf[slot]) * lng[...] + lnb[...]


def _final(base, ys, dest, ada_l, lng, lnb, B, S):
    T, D = base.shape
    tf = FINAL_ROWS
    per_batch = S // tf
    return pl.pallas_call(
        _final_kernel,
        out_shape=jax.ShapeDtypeStruct((T, D), F32),
        grid_spec=pltpu.PrefetchScalarGridSpec(
            num_scalar_prefetch=1,
            grid=(T // tf,),
            in_specs=[pl.BlockSpec((tf, D), lambda i, d: (i, 0)),
                      pl.BlockSpec(memory_space=pl.ANY),
                      pl.BlockSpec((None, 6, D), lambda i, d: (i // per_batch, 0, 0)),
                      pl.BlockSpec((1, D), lambda i, d: (0, 0)), pl.BlockSpec((1, D), lambda i, d: (0, 0))],
            out_specs=pl.BlockSpec((tf, D), lambda i, d: (i, 0)),
            scratch_shapes=[pltpu.VMEM((2, tf, D), F32), pltpu.SemaphoreType.DMA((2,))]),
        compiler_params=_cparams(1),
        name="final_ln",
    )(dest, base, ys, ada_l, lng, lnb)


def _rope_tables(positions):
    inv_freq = ROPE_THETA ** (-jnp.arange(0, QK_ROPE, 2, dtype=F32) / QK_ROPE)
    ang = positions.astype(F32)[..., None] * inv_freq
    dense = lax.optimization_barrier(ang.reshape(-1, LANE))
    cos, sin = lax.optimization_barrier((jnp.cos(dense), jnp.sin(dense)))
    cos, sin = cos.reshape(ang.shape), sin.reshape(ang.shape)
    ones = jnp.ones(cos.shape[:-1] + (QK_NOPE,), F32)
    zpad = jnp.zeros(cos.shape[:-1] + (HEAD_PAD - QK_NOPE - QK_ROPE,), F32)
    cos128 = jnp.concatenate([ones, cos, cos, zpad], axis=-1)
    sin128 = jnp.concatenate([0.0 * ones, -sin, sin, zpad], axis=-1)
    T = positions.shape[0] * positions.shape[1]
    return cos128.reshape(T, HEAD_PAD), sin128.reshape(T, HEAD_PAD)


def _swap_halves(w):
    half = w.shape[-1] // 2
    return jnp.concatenate([w[..., half:], w[..., :half]], axis=-1)


def _layer_weights(l, p):
    D = p["w_in"].shape[1]
    w_in = p["w_in"][l]
    o = 0
    seg = {}
    for name, width in (("cv", 3 * D_CONV), ("ssm", D_SSM), ("cq", Q_LORA), ("ckv", KV_LORA),
                        ("kr", QK_ROPE), ("g", 3 * D)):
        seg[name] = w_in[:, o:o + width]
        o += width
    pad_r = HEAD_PAD - QK_NOPE - QK_ROPE
    zl = jnp.zeros((D, QK_NOPE), F32)
    zr = jnp.zeros((D, pad_r), F32)
    w = {
        "w_cv": seg["cv"].astype(BF16), "w_ssm": seg["ssm"].astype(BF16),
        "w_lat": jnp.concatenate([seg["cq"], seg["ckv"], zl, seg["kr"], zr,
                                  zl, _swap_halves(seg["kr"]), zr], axis=1).astype(BF16),
        "w_g": seg["g"].astype(BF16),
        "convw": p["conv_w"][l],
        "qn": p["q_norm"][l].reshape(1, Q_LORA), "kvn": p["kv_norm"][l].reshape(1, KV_LORA),
        "w_upc": p["w_up_conv"][l].astype(BF16),
    }
    scale = (QK_NOPE + QK_ROPE) ** -0.5 * math.log2(math.e)
    wq = p["w_uq"][l].reshape(Q_LORA, MLA_HEADS, QK_NOPE + QK_ROPE) * scale
    zq = jnp.zeros((Q_LORA, MLA_HEADS, pad_r), F32)
    zn = jnp.zeros((Q_LORA, MLA_HEADS, QK_NOPE), F32)
    w["wq_a"] = jnp.concatenate([wq, zq], axis=-1).reshape(Q_LORA, -1).astype(BF16)
    w["wq_b"] = jnp.concatenate([zn, _swap_halves(wq[..., QK_NOPE:]), zq],
                                axis=-1).reshape(Q_LORA, -1).astype(BF16)
    wk = p["w_uk"][l].reshape(KV_LORA, MLA_HEADS, QK_NOPE)
    zk = jnp.zeros((KV_LORA, MLA_HEADS, HEAD_PAD - QK_NOPE), F32)
    w["wk"] = jnp.concatenate([wk, zk], axis=-1).reshape(KV_LORA, -1).astype(BF16)
    w["wv"] = p["w_uv"][l].T.astype(BF16)

    ar, ai = p["ssm_a_re"][l], p["ssm_a_im"][l]
    dt = jnp.exp(p["ssm_log_dt"][l])[:, None]
    mag = jnp.exp(ar * dt)
    lb_re, lb_im = mag * jnp.cos(ai * dt), mag * jnp.sin(ai * dt)
    den = ar * ar + ai * ai
    nr, ni = lb_re - 1.0, lb_im
    f_re, f_im = (nr * ar + ni * ai) / den, (ni * ar - nr * ai) / den
    br, bi = p["ssm_b_re"][l], p["ssm_b_im"][l]
    bb_re = f_re[..., None] * br - f_im[..., None] * bi
    bb_im = f_re[..., None] * bi + f_im[..., None] * br
    gpc = LANE // SSM_GROUP_DIM
    nch = SSM_GROUPS // gpc
    eye = jnp.eye(gpc, dtype=F32)

    def bmat(bb):
        return jnp.einsum("cgnp,gh->cgphn", bb.reshape(nch, gpc, SSM_STATE, SSM_GROUP_DIM),
                          eye).reshape(nch, LANE, gpc * SSM_STATE).astype(BF16)

    def cmat(cc):
        return jnp.einsum("cgpn,gh->cgnhp", cc.reshape(nch, gpc, SSM_GROUP_DIM, SSM_STATE),
                          eye).reshape(nch, gpc * SSM_STATE, LANE).astype(BF16)

    w["ssm_bre"], w["ssm_bim"] = bmat(bb_re), bmat(bb_im)
    w["ssm_cre"], w["ssm_cim"] = cmat(p["ssm_c_re"][l]), cmat(-p["ssm_c_im"][l])
    nrow = 8
    w["ssm_lre"] = jnp.broadcast_to(lb_re.reshape(1, -1), (nrow, SSM_GROUPS * SSM_STATE))
    w["ssm_lim"] = jnp.broadcast_to(lb_im.reshape(1, -1), (nrow, SSM_GROUPS * SSM_STATE))
    w["ssm_d"] = p["ssm_d"][l].reshape(1, D_SSM)
    w["ssm_wglu"] = p["ssm_w_glu"][l].astype(BF16)

    w["w_ups"] = p["w_up_ssm"][l].astype(BF16)
    w["w_upa"] = p["w_up_attn"][l].astype(BF16)
    w["w_o"] = p["w_o"][l].astype(BF16)
    w["lng0"], w["lnb0"] = p["ln_g"][l, 0].reshape(1, D), p["ln_b"][l, 0].reshape(1, D)
    w["lng1"], w["lnb1"] = p["ln_g"][l, 1].reshape(1, D), p["ln_b"][l, 1].reshape(1, D)
    rwg = p["router_w"].reshape(D, N_EXPERT_GROUPS, EXPERTS_PER_GROUP)
    rbg = p["router_bias"].reshape(N_EXPERT_GROUPS, EXPERTS_PER_GROUP)
    w["rw"] = jnp.transpose(rwg, (2, 1, 0)).reshape(N_EXPERTS, D)
    w["rb"] = jnp.transpose(rbg, (1, 0)).reshape(N_EXPERTS, 1)
    w["ws1"] = p["shared_w1"][l].astype(BF16)
    w["ws3"] = p["shared_w3"][l].astype(BF16)
    w["ws2"] = p["shared_w2"][l].astype(BF16)
    return w


def kernel(x, c, positions, w_ada, b_ada, w_in, conv_w, ssm_a_re, ssm_a_im, ssm_b_re, ssm_b_im,
           ssm_c_re, ssm_c_im, ssm_d, ssm_log_dt, ssm_w_glu, q_norm, w_uq, kv_norm, w_uk, w_uv,
           w_up_conv, w_up_ssm, w_up_attn, w_o, ln_g, ln_b, router_w, router_bias,
           exp_w1, exp_w3, exp_w2, shared_w1, shared_w3, shared_w2):
    B, S, D = x.shape
    L = w_in.shape[0]
    T = B * S
    assert B == 8 and S % max(TM_PROJ, T_CHUNK, TQ, FINAL_ROWS) == 0 and T % max(MOE_ROWS, PERM_ROWS) == 0
    assert TM_PROJ == TQ
    p = dict(w_in=w_in, conv_w=conv_w, ssm_a_re=ssm_a_re, ssm_a_im=ssm_a_im, ssm_b_re=ssm_b_re,
             ssm_b_im=ssm_b_im, ssm_c_re=ssm_c_re, ssm_c_im=ssm_c_im, ssm_d=ssm_d,
             ssm_log_dt=ssm_log_dt, ssm_w_glu=ssm_w_glu, q_norm=q_norm, w_uq=w_uq, kv_norm=kv_norm,
             w_uk=w_uk, w_uv=w_uv, w_up_conv=w_up_conv, w_up_ssm=w_up_ssm, w_up_attn=w_up_attn,
             w_o=w_o, ln_g=ln_g, ln_b=ln_b, router_w=router_w, router_bias=router_bias,
             exp_w1=exp_w1, exp_w3=exp_w3, exp_w2=exp_w2, shared_w1=shared_w1,
             shared_w3=shared_w3, shared_w2=shared_w2)
    alpha = (2 * L) ** 0.25
    cos128, sin128 = _rope_tables(positions)
    ada = _ada(c, w_ada, b_ada).reshape(L, B, 6, D)
    x2d = x.reshape(T, D)
    for l in range(L):
        wts = _layer_weights(l, p)
        pc, sgs, sga, u, q, k, v = _inproj(x2d, ada[l], cos128, sin128, wts, B, S)
        ys = _ssm(u.reshape(B, S, D_SSM), wts, B, S).reshape(T, D_SSM)
        ya = _attention(q, k, v, B, S)
        base, hr, rt = _merge(x2d, ada[l], pc, sgs, sga, ys, ya, wts, B, S, alpha)
        gid = rt[EXPERTS_PER_GROUP].astype(jnp.int32)
        dest, block_g, nvalid = _route_plan(gid, T)
        hs = _permute(hr, dest, nvalid)
        ysort = _moe(hs, block_g, nvalid, exp_w1, exp_w3, exp_w2, l)
        x2d = _final(base, ysort, dest, ada[l], wts["lng1"], wts["lnb1"], B, S)
    return x2d.reshape(B, S, D)
```

```python
import functools
import math

import jax
import jax.numpy as jnp
from jax import lax
from jax.experimental import pallas as pl
from jax.experimental.pallas import tpu as pltpu

F32 = jnp.float32
BF16 = jnp.bfloat16

D_CONV = 512
D_SSM = 512
SSM_GROUP_DIM = 16
SSM_GROUPS = 32
SSM_STATE = 64
MLA_HEADS = 8
QK_NOPE = 64
QK_ROPE = 32
V_HEAD = 64
Q_LORA = 256
KV_LORA = 128
ROPE_THETA = 10000.0
N_EXPERTS = 32
N_EXPERT_GROUPS = 8
EXPERTS_PER_GROUP = 4
D_EXPERT = 256
LN_EPS = 1e-5
RMS_EPS = 1e-6

LANE = 128
HEAD_PAD = 128
VMEM_LIMIT = 56 * 1024 * 1024

TM_PROJ = 512
T_CHUNK = 64
TQ = 512
ATTN_HEADS = 8
DEN_ROWS = 16
MOE_ROWS = 512
PERM_ROWS = 1024
FINAL_ROWS = 512
ROUTE_W = 128


def _cparams(n_axes):
    return pltpu.CompilerParams(dimension_semantics=("arbitrary",) * n_axes,
                                vmem_limit_bytes=VMEM_LIMIT)


def _ln_plain(x):
    mu = jnp.mean(x, axis=-1, keepdims=True)
    xc = x - mu
    var = jnp.mean(xc * xc, axis=-1, keepdims=True)
    return xc * lax.rsqrt(var + LN_EPS)


def _rms(x, g):
    return x * lax.rsqrt(jnp.mean(x * x, axis=-1, keepdims=True) + RMS_EPS) * g


def _bdot(a, b):
    return jnp.dot(a.astype(BF16), b, preferred_element_type=F32)


def _ada_kernel(c_ref, w_ref, b_ref, o_ref):
    c = c_ref[...]
    cond = c * jax.nn.sigmoid(c)
    o_ref[...] = jnp.dot(cond, w_ref[...], precision=lax.Precision.HIGHEST,
                         preferred_element_type=F32) + b_ref[...]


def _ada(c, w_ada, b_ada):
    L, D, D6 = w_ada.shape
    B = c.shape[0]
    wn = D6 // 2
    nj = D6 // wn
    return pl.pallas_call(
        _ada_kernel,
        out_shape=jax.ShapeDtypeStruct((L, B, D6), F32),
        grid=(L, nj),
        in_specs=[pl.BlockSpec((B, D), lambda l, j: (0, 0)),
                  pl.BlockSpec((None, D, wn), lambda l, j: (l, 0, j)),
                  pl.BlockSpec((None, 1, wn), lambda l, j: (l, 0, j))],
        out_specs=pl.BlockSpec((None, B, wn), lambda l, j: (l, 0, j)),
        compiler_params=_cparams(2),
        name="ada",
    )(c, w_ada, b_ada.reshape(L, 1, D6))


def _inproj_kernel(x_ref, ada_ref, cos_ref, sin_ref,
                   w_cv, w_ssm, w_lat, w_g,
                   convw, qn, wq_a, wq_b, kvn, wk, wv, w_upc,
                   pc_ref, sgs_ref, sga_ref, u_ref, q_ref, k_ref, v_ref,
                   carry_ref):
    si = pl.program_id(1)
    D = x_ref.shape[1]
    tm = x_ref.shape[0]

    @pl.when(si == 0)
    def _():
        carry_ref[...] = jnp.zeros_like(carry_ref)

    ada = ada_ref[...]
    h = _ln_plain(x_ref[...]) * (1.0 + ada[1:2, :]) + ada[0:1, :]
    hb = h.astype(BF16)

    pcv = jnp.dot(hb, w_cv[...], preferred_element_type=F32)
    u = pcv[:, D_CONV:2 * D_CONV] * pcv[:, :D_CONV]
    gb = pcv[:, 2 * D_CONV:]
    tail = carry_ref[...]
    row8 = lax.broadcasted_iota(jnp.int32, (8, D_CONV), 0)
    r1 = pltpu.roll(u, 1, 0)
    r2 = pltpu.roll(u, 2, 0)
    t1 = pltpu.roll(tail, 1, 0)
    t2 = pltpu.roll(tail, 2, 0)
    u1 = jnp.concatenate([jnp.where(row8 < 1, t1, r1[:8]), r1[8:]], axis=0)
    u2 = jnp.concatenate([jnp.where(row8 < 2, t2, r2[:8]), r2[8:]], axis=0)
    cw = convw[...]
    yc = gb * (cw[0:1, :] * u2 + cw[1:2, :] * u1 + cw[2:3, :] * u)
    carry_ref[...] = u[tm - 8:, :]

    gates = jnp.dot(hb, w_g[...], preferred_element_type=F32)
    pc_ref[...] = (jax.nn.sigmoid(gates[:, :D]) * _bdot(yc, w_upc[...])).astype(BF16)
    sgs_ref[...] = jax.nn.sigmoid(gates[:, D:2 * D]).astype(BF16)
    sga_ref[...] = jax.nn.sigmoid(gates[:, 2 * D:]).astype(BF16)

    u_ref[...] = jnp.dot(hb, w_ssm[...], preferred_element_type=F32).astype(BF16)

    cos = cos_ref[...]
    sin = sin_ref[...]
    lat = jnp.dot(hb, w_lat[...], preferred_element_type=F32)
    o_kv, o_kr = Q_LORA, Q_LORA + KV_LORA
    cq = _rms(lat[:, :o_kv], qn[...]).astype(BF16)
    qa = jnp.dot(cq, wq_a[...], preferred_element_type=F32)
    qb = jnp.dot(cq, wq_b[...], preferred_element_type=F32)
    ckv = _rms(lat[:, o_kv:o_kr], kvn[...]).astype(BF16)
    kn = jnp.dot(ckv, wk[...], preferred_element_type=F32)
    kr = lat[:, o_kr:o_kr + HEAD_PAD] * cos + lat[:, o_kr + HEAD_PAD:] * sin
    for hd in range(MLA_HEADS):
        sl = slice(hd * HEAD_PAD, (hd + 1) * HEAD_PAD)
        q_ref[:, sl] = (qa[:, sl] * cos + qb[:, sl] * sin).astype(BF16)
        k_ref[:, sl] = (kn[:, sl] + kr).astype(BF16)
    v_ref[...] = lax.dot_general(wv[...], ckv, (((1,), (1,)), ((), ())),
                                 preferred_element_type=F32).astype(BF16)


def _inproj(x2d, ada_l, cos128, sin128, wts, B, S):
    T, D = x2d.shape
    tm = TM_PROJ
    ns = S // tm
    row = lambda b, s: (b * ns + s, 0)
    full = lambda a: pl.BlockSpec(a.shape, lambda b, s: (0,) * a.ndim, pipeline_mode=pl.Buffered(1))
    names = ["w_cv", "w_ssm", "w_lat", "w_g", "convw", "qn",
             "wq_a", "wq_b", "kvn", "wk", "wv", "w_upc"]
    ws = [wts[n] for n in names]
    out_shape = (
        jax.ShapeDtypeStruct((T, D), BF16),
        jax.ShapeDtypeStruct((T, D), BF16),
        jax.ShapeDtypeStruct((T, D), BF16),
        jax.ShapeDtypeStruct((T, D_SSM), BF16),
        jax.ShapeDtypeStruct((T, MLA_HEADS * HEAD_PAD), BF16),
        jax.ShapeDtypeStruct((T, MLA_HEADS * HEAD_PAD), BF16),
        jax.ShapeDtypeStruct((B, ns, MLA_HEADS * V_HEAD, tm), BF16),
    )
    out_specs = (
        pl.BlockSpec((tm, D), row), pl.BlockSpec((tm, D), row), pl.BlockSpec((tm, D), row),
        pl.BlockSpec((tm, D_SSM), row),
        pl.BlockSpec((tm, MLA_HEADS * HEAD_PAD), row),
        pl.BlockSpec((tm, MLA_HEADS * HEAD_PAD), row),
        pl.BlockSpec((None, None, MLA_HEADS * V_HEAD, tm), lambda b, s: (b, s, 0, 0)),
    )
    return pl.pallas_call(
        _inproj_kernel,
        out_shape=out_shape,
        grid=(B, ns),
        in_specs=[pl.BlockSpec((tm, D), row),
                  pl.BlockSpec((None, 6, D), lambda b, s: (b, 0, 0)),
                  pl.BlockSpec((tm, LANE), row), pl.BlockSpec((tm, LANE), row)]
                 + [full(a) for a in ws],
        out_specs=out_specs,
        scratch_shapes=[pltpu.VMEM((8, D_CONV), F32)],
        compiler_params=_cparams(2),
        name="inproj",
    )(x2d, ada_l, cos128, sin128, *ws)


def _ssm_kernel(u_ref, bre_ref, bim_ref, lre_ref, lim_ref, cre_ref, cim_ref, d_ref, wglu_ref, perm_ref,
                o_ref, are, aim, au, bre_s, bim_s, bu, st_re, st_im, tmaj):
    i = pl.program_id(0)
    batch, nsteps = u_ref.shape[0], u_ref.shape[1]
    rows = batch * nsteps
    nchunk = bre_ref.shape[0]
    cw = bre_ref.shape[2]

    @pl.when(i == 0)
    def _():
        st_re[...] = jnp.zeros_like(st_re)
        st_im[...] = jnp.zeros_like(st_im)
        bre_s[...] = jnp.zeros_like(bre_s)
        bim_s[...] = jnp.zeros_like(bim_s)
        bu[...] = jnp.zeros_like(bu)

    def body(cur_re, cur_im, cur_u, prev_re, prev_im, prev_u):
        for b in range(batch):
            tmaj[:, b, :] = u_ref[b].astype(F32)
        uf = tmaj[...].reshape(rows, D_SSM)
        cur_u[...] = uf
        ub = uf.astype(BF16)
        for c in range(nchunk):
            uc = ub[:, c * LANE:(c + 1) * LANE]
            cur_re[:, c * cw:(c + 1) * cw] = jnp.dot(uc, bre_ref[c], preferred_element_type=F32)
            cur_im[:, c * cw:(c + 1) * cw] = jnp.dot(uc, bim_ref[c], preferred_element_type=F32)

        ys = []

        def out_chunk(c):
            ys.append(_bdot(prev_re[:, c * cw:(c + 1) * cw], cre_ref[c])
                      + _bdot(prev_im[:, c * cw:(c + 1) * cw], cim_ref[c]))

        def out_finish():
            y = jnp.concatenate(ys, axis=1) + d_ref[...] * prev_u[...]
            y = 0.5 * y * (1.0 + jnp.tanh(math.sqrt(2.0 / math.pi) * (y + 0.044715 * (y * y * y))))
            z = _bdot(y, wglu_ref[...])
            g = (z[:, :D_SSM] * jax.nn.sigmoid(z[:, D_SSM:])).astype(BF16)
            o_ref[...] = jnp.dot(perm_ref[...], g, preferred_element_type=F32).astype(BF16).reshape(
                batch, nsteps, D_SSM)

        pieces = [functools.partial(out_chunk, c) for c in range(nchunk)] + [out_finish]
        every = nsteps // len(pieces)
        pr, pi = st_re[...], st_im[...]
        for t in range(nsteps):
            r = t * batch
            lre = lre_ref[...]
            lim = lim_ref[...]
            nr = lre * pr - lim * pi + cur_re[r:r + batch, :]
            ni = lre * pi + lim * pr + cur_im[r:r + batch, :]
            cur_re[r:r + batch, :] = nr
            cur_im[r:r + batch, :] = ni
            pr, pi = nr, ni
            if t % every == every - 1 and pieces:
                pieces.pop(0)()
        while pieces:
            pieces.pop(0)()
        st_re[...] = pr
        st_im[...] = pi

    @pl.when(i % 2 == 0)
    def _():
        body(are, aim, au, bre_s, bim_s, bu)

    @pl.when(i % 2 == 1)
    def _():
        body(bre_s, bim_s, bu, are, aim, au)


def _ssm(u, wts, B, S):
    rows = T_CHUNK * B
    n = S // T_CHUNK
    nstate = SSM_GROUPS * SSM_STATE
    names = ["ssm_bre", "ssm_bim", "ssm_lre", "ssm_lim", "ssm_cre", "ssm_cim", "ssm_d", "ssm_wglu"]
    r = jnp.arange(rows, dtype=jnp.int32)
    src = (r % T_CHUNK) * B + r // T_CHUNK
    perm = (src[:, None] == jnp.arange(rows, dtype=jnp.int32)[None, :]).astype(BF16)
    ws = [wts[k] for k in names] + [perm]
    full = lambda a: pl.BlockSpec(a.shape, lambda i: (0,) * a.ndim, pipeline_mode=pl.Buffered(1))
    return pl.pallas_call(
        _ssm_kernel,
        out_shape=jax.ShapeDtypeStruct((B, S, D_SSM), BF16),
        grid=(n + 1,),
        in_specs=[pl.BlockSpec((B, T_CHUNK, D_SSM), lambda i: (0, jnp.minimum(i, n - 1), 0))]
                 + [full(a) for a in ws],
        out_specs=pl.BlockSpec((B, T_CHUNK, D_SSM), lambda i: (0, jnp.maximum(i - 1, 0), 0)),
        scratch_shapes=[pltpu.VMEM((rows, nstate), F32), pltpu.VMEM((rows, nstate), F32),
                        pltpu.VMEM((rows, D_SSM), F32),
                        pltpu.VMEM((rows, nstate), F32), pltpu.VMEM((rows, nstate), F32),
                        pltpu.VMEM((rows, D_SSM), F32),
                        pltpu.VMEM((B, nstate), F32), pltpu.VMEM((B, nstate), F32),
                        pltpu.VMEM((T_CHUNK, B, D_SSM), F32)],
        compiler_params=_cparams(1),
        name="ssm",
    )(u, *ws)


def _attn_kernel(q_ref, k_ref, vt_ref, o_ref, m_sc, acc_sc, sa_sc, sb_sc, p_sc):
    qi = pl.program_id(2)
    tq = q_ref.shape[0]
    nh = m_sc.shape[0]
    neg = -1e30
    m_sc[...] = jnp.full(m_sc.shape, neg, F32)
    acc_sc[...] = jnp.zeros(acc_sc.shape, F32)
    ones = jnp.ones((DEN_ROWS, tq), BF16)

    def scores(j, s_sc):
        off = pl.multiple_of(j * tq, tq)
        for hh in range(nh):
            q = q_ref[:, hh * HEAD_PAD:(hh + 1) * HEAD_PAD]
            kb = k_ref[pl.ds(off, tq), hh * HEAD_PAD:(hh + 1) * HEAD_PAD]
            s_sc[hh] = lax.dot_general(kb, q, (((1,), (1,)), ((), ())),
                                       preferred_element_type=F32)

    def update(j, s_sc, masked):
        scale = []
        for hh in range(nh):
            if masked:
                keyi = lax.broadcasted_iota(jnp.int32, (tq, tq), 0)
                qryi = lax.broadcasted_iota(jnp.int32, (tq, tq), 1)
                s_sc[hh] = jnp.where(keyi <= qryi, s_sc[hh], neg)
            m_prev = m_sc[hh]
            m_new = jnp.maximum(m_prev, jnp.max(s_sc[hh], axis=0, keepdims=True))
            m_sc[hh] = m_new
            p_sc[hh] = jnp.exp2(s_sc[hh] - m_new).astype(BF16)
            scale.append(jnp.exp2(m_prev - m_new))
        for hh in range(nh):
            vte = jnp.concatenate([vt_ref[j, hh * V_HEAD:(hh + 1) * V_HEAD, :], ones], axis=0)
            acc_sc[hh] = scale[hh] * acc_sc[hh] + jnp.dot(vte, p_sc[hh], preferred_element_type=F32)

    scores(0, sa_sc)

    def pair(pp, c):
        j = 2 * pp
        scores(j + 1, sb_sc)
        update(j, sa_sc, False)
        scores(j + 2, sa_sc)
        update(j + 1, sb_sc, False)
        return c

    npair = qi // 2
    lax.fori_loop(0, npair, pair, 0)
    j0 = 2 * npair

    @pl.when(qi % 2 == 0)
    def _():
        update(j0, sa_sc, True)

    @pl.when(qi % 2 == 1)
    def _():
        scores(j0 + 1, sb_sc)
        update(j0, sa_sc, False)
        update(j0 + 1, sb_sc, True)

    outs = []
    for pr in range(nh // 2):
        halves = []
        for hh in (2 * pr, 2 * pr + 1):
            acc = acc_sc[hh]
            halves.append(acc[:V_HEAD] * (1.0 / acc[V_HEAD:V_HEAD + 1]))
        outs.append(jnp.concatenate(halves, axis=0).T)
    o_ref[...] = jnp.concatenate(outs, axis=1).astype(BF16)


def _attention(q, k, vt, B, S):
    nq = S // TQ
    nh = ATTN_HEADS
    hg = MLA_HEADS // nh
    return pl.pallas_call(
        _attn_kernel,
        out_shape=jax.ShapeDtypeStruct((B * S, MLA_HEADS * V_HEAD), BF16),
        grid=(B, hg, nq),
        in_specs=[pl.BlockSpec((TQ, nh * HEAD_PAD), lambda b, h, i: (b * nq + i, h)),
                  pl.BlockSpec((S, nh * HEAD_PAD), lambda b, h, i: (b, h)),
                  pl.BlockSpec((None, nq, nh * V_HEAD, TQ), lambda b, h, i: (b, 0, h, 0))],
        out_specs=pl.BlockSpec((TQ, nh * V_HEAD), lambda b, h, i: (b * nq + i, h)),
        scratch_shapes=[pltpu.VMEM((nh, 1, TQ), F32),
                        pltpu.VMEM((nh, V_HEAD + DEN_ROWS, TQ), F32),
                        pltpu.VMEM((nh, TQ, TQ), F32), pltpu.VMEM((nh, TQ, TQ), F32),
                        pltpu.VMEM((nh, TQ, TQ), BF16)],
        compiler_params=_cparams(3),
        name="attention",
    )(q, k, vt)


def _merge_kernel(x_ref, ada_ref, pc_ref, sgs_ref, sga_ref, ys_ref, ya_ref,
                  w_ups, w_upa, w_o, lng, lnb, rw, rb, ws1, ws3, ws2,
                  base_ref, hr_ref, rt_ref, *, alpha):
    D = x_ref.shape[1]
    tm = x_ref.shape[0]
    ada = ada_ref[...]
    merged = (pc_ref[...].astype(F32)
              + sgs_ref[...].astype(F32) * jnp.dot(ys_ref[...], w_ups[...], preferred_element_type=F32)
              + sga_ref[...].astype(F32) * jnp.dot(ya_ref[...], w_upa[...], preferred_element_type=F32))
    y = _bdot(merged, w_o[...])
    x1 = _ln_plain(alpha * x_ref[...] + ada[2:3, :] * y) * lng[...] + lnb[...]
    h2 = _ln_plain(x1) * (1.0 + ada[4:5, :]) + ada[3:4, :]
    hr_ref[:, :D] = h2

    G = N_EXPERT_GROUPS
    logits = lax.dot_general(rw[...], h2, (((1,), (1,)), ((), ())),
                             precision=lax.Precision.HIGHEST, preferred_element_type=F32)
    sc0 = jax.nn.sigmoid(logits)
    sl0 = sc0 + rb[...]
    sc = [sc0[j * G:(j + 1) * G] for j in range(EXPERTS_PER_GROUP)]
    sl = [sl0[j * G:(j + 1) * G] for j in range(EXPERTS_PER_GROUP)]
    hi1, lo1 = jnp.maximum(sl[0], sl[1]), jnp.minimum(sl[0], sl[1])
    hi2, lo2 = jnp.maximum(sl[2], sl[3]), jnp.minimum(sl[2], sl[3])
    gscore = jnp.maximum(hi1, hi2) + jnp.maximum(jnp.minimum(hi1, hi2), jnp.maximum(lo1, lo2))
    sub = lax.broadcasted_iota(jnp.int32, (G, tm), 0)
    gmax = jnp.max(gscore, axis=0, keepdims=True)
    gidx = jnp.min(jnp.where(gscore == gmax, sub, G), axis=0, keepdims=True)
    pick = sub == gidx
    vs = [jnp.sum(jnp.where(pick, a, 0.0), axis=0, keepdims=True) for a in sl]
    ss = [jnp.sum(jnp.where(pick, a, 0.0), axis=0, keepdims=True) for a in sc]
    chosen = []
    for j in range(EXPERTS_PER_GROUP):
        cnt = jnp.zeros((1, tm), jnp.int32)
        for k in range(EXPERTS_PER_GROUP):
            if k == j:
                continue
            beats = (vs[k] > vs[j]) | ((vs[k] == vs[j]) & (k < j))
            cnt = cnt + beats.astype(jnp.int32)
        chosen.append(cnt < 2)
    wsel = [jnp.where(chosen[j], ss[j], 0.0) for j in range(EXPERTS_PER_GROUP)]
    wsum = wsel[0] + wsel[1] + wsel[2] + wsel[3]
    rows = [wsel[j] / wsum for j in range(EXPERTS_PER_GROUP)] + [gidx.astype(F32)]
    rows += [jnp.zeros((1, tm), F32)] * (8 - len(rows))
    route_t = jnp.concatenate(rows, axis=0)
    rt_ref[...] = route_t
    eye = (lax.broadcasted_iota(jnp.int32, (8, ROUTE_W), 0)
           == lax.broadcasted_iota(jnp.int32, (8, ROUTE_W), 1)).astype(F32)
    hr_ref[:, D:] = lax.dot_general(route_t, eye, (((0,), (0,)), ((), ())),
                                    precision=lax.Precision.HIGHEST, preferred_element_type=F32)

    h2b = h2.astype(BF16)
    a1 = jnp.dot(h2b, ws1[...], preferred_element_type=F32)
    a3 = jnp.dot(h2b, ws3[...], preferred_element_type=F32)
    ysh = _bdot(a1 * jax.nn.sigmoid(a1) * a3, ws2[...])
    base_ref[...] = alpha * x1 + ada[5:6, :] * ysh


def _merge(x2d, ada_l, pc, sgs, sga, ys, ya, wts, B, S, alpha):
    T, D = x2d.shape
    tm = TM_PROJ
    ns = S // tm
    row = lambda b, s: (b * ns + s, 0)
    names = ["w_ups", "w_upa", "w_o", "lng0", "lnb0", "rw", "rb", "ws1", "ws3", "ws2"]
    ws = [wts[n] for n in names]
    full = lambda a: pl.BlockSpec(a.shape, lambda b, s: (0,) * a.ndim, pipeline_mode=pl.Buffered(1))
    return pl.pallas_call(
        functools.partial(_merge_kernel, alpha=alpha),
        out_shape=(jax.ShapeDtypeStruct((T, D), F32),
                   jax.ShapeDtypeStruct((T, D + ROUTE_W), F32),
                   jax.ShapeDtypeStruct((8, T), F32)),
        grid=(B, ns),
        in_specs=[pl.BlockSpec((tm, D), row),
                  pl.BlockSpec((None, 6, D), lambda b, s: (b, 0, 0)),
                  pl.BlockSpec((tm, D), row), pl.BlockSpec((tm, D), row), pl.BlockSpec((tm, D), row),
                  pl.BlockSpec((tm, D_SSM), row),
                  pl.BlockSpec((tm, MLA_HEADS * V_HEAD), row)]
                 + [full(a) for a in ws],
        out_specs=(pl.BlockSpec((tm, D), row), pl.BlockSpec((tm, D + ROUTE_W), row),
                   pl.BlockSpec((8, tm), lambda b, s: (0, b * ns + s))),
        compiler_params=_cparams(2),
        name="merge",
    )(x2d, ada_l, pc, sgs, sga, ys, ya, *ws)


def _route_plan(gid, T):
    R = MOE_ROWS
    G = N_EXPERT_GROUPS
    onehot = (gid[:, None] == jnp.arange(G, dtype=jnp.int32)[None, :]).astype(jnp.int32)
    csum = jnp.cumsum(onehot, axis=0)
    counts = csum[-1]
    rank = jnp.sum(onehot * csum, axis=1) - 1
    padded = (counts + R - 1) // R * R
    pad_end = jnp.cumsum(padded)
    pad_start = pad_end - padded
    dest = jnp.sum(onehot * pad_start[None, :], axis=1) + rank
    nb = T // R + G
    starts = jnp.arange(nb, dtype=jnp.int32) * R
    block_g = jnp.minimum(jnp.sum((starts[:, None] >= pad_end[None, :]).astype(jnp.int32), axis=1), G - 1)
    gsel = (block_g[:, None] == jnp.arange(G, dtype=jnp.int32)[None, :]).astype(jnp.int32)
    group_end = jnp.sum(gsel * (pad_start + counts)[None, :], axis=1)
    nvalid = jnp.clip(group_end - starts, 0, R)
    return dest.astype(jnp.int32), block_g.astype(jnp.int32), nvalid.astype(jnp.int32)


def _permute_kernel(dest_ref, nvalid_ref, hr_ref, hs_hbm, zbuf, sem, zsem):
    i = pl.program_id(0)
    tp = hr_ref.shape[0]
    R = zbuf.shape[0]
    nb = nvalid_ref.shape[0]

    @pl.when(i == 0)
    def _():
        zbuf[...] = jnp.zeros(zbuf.shape, F32)

        def fill(b, n):
            partial = nvalid_ref[b] < R

            @pl.when(partial)
            def _():
                pltpu.make_async_copy(zbuf, hs_hbm.at[pl.ds(b * R, R)], zsem).start()

            return n + partial.astype(jnp.int32)

        nfill = lax.fori_loop(0, nb, fill, 0)

        def drain(k, c):
            pltpu.make_async_copy(zbuf, hs_hbm.at[pl.ds(0, R)], zsem).wait()
            return c

        lax.fori_loop(0, nfill, drain, 0)

    def body(g, c):
        r0 = pl.multiple_of(g * 8, 8)
        for k in range(8):
            d = dest_ref[i * tp + r0 + k]
            pltpu.make_async_copy(hr_ref.at[pl.ds(r0 + k, 1)], hs_hbm.at[pl.ds(d, 1)], sem).start()
        return c

    lax.fori_loop(0, tp // 8, body, 0)
    pltpu.make_async_copy(hr_ref, hs_hbm.at[pl.ds(0, tp)], sem).wait()


def _permute(hr, dest, nvalid):
    T, W = hr.shape
    R = MOE_ROWS
    tp = PERM_ROWS
    nb = nvalid.shape[0]
    return pl.pallas_call(
        _permute_kernel,
        out_shape=jax.ShapeDtypeStruct((nb * R, W), F32),
        grid_spec=pltpu.PrefetchScalarGridSpec(
            num_scalar_prefetch=2,
            grid=(T // tp,),
            in_specs=[pl.BlockSpec((tp, W), lambda i, d, n: (i, 0))],
            out_specs=pl.BlockSpec(memory_space=pl.ANY),
            scratch_shapes=[pltpu.VMEM((R, W), F32), pltpu.SemaphoreType.DMA, pltpu.SemaphoreType.DMA]),
        compiler_params=_cparams(1),
        name="permute",
    )(dest, nvalid, hr)


def _moe_kernel(bg_ref, nvalid_ref, hs_ref, w1_ref, w3_ref, w2_ref, y_ref, w1b, w3b, w2b):
    i = pl.program_id(0)
    R, D = y_ref.shape

    @pl.when((i == 0) | (bg_ref[i] != bg_ref[jnp.maximum(i - 1, 0)]))
    def _():
        w1b[...] = w1_ref[...].astype(BF16)
        w3b[...] = w3_ref[...].astype(BF16)
        w2b[...] = w2_ref[...].astype(BF16)

    @pl.when(nvalid_ref[i] > 0)
    def _():
        xb = hs_ref[:, :D].astype(BF16)
        acc = jnp.zeros((R, D), F32)
        for j in range(EXPERTS_PER_GROUP):
            a1 = jnp.dot(xb, w1b[j], preferred_element_type=F32)
            a3 = jnp.dot(xb, w3b[j], preferred_element_type=F32)
            hj = a1 * jax.nn.sigmoid(a1) * a3 * hs_ref[:, D + j:D + j + 1]
            acc = acc + _bdot(hj, w2b[j])
        y_ref[...] = acc

    @pl.when(nvalid_ref[i] == 0)
    def _():
        y_ref[...] = jnp.zeros((R, D), F32)


def _moe(hs, block_g, nvalid, w1, w3, w2, layer):
    D = w1.shape[2]
    W = hs.shape[1]
    R = MOE_ROWS
    nb = nvalid.shape[0]
    wspec_in = pl.BlockSpec((None, EXPERTS_PER_GROUP, D, D_EXPERT), lambda i, bg, nv: (layer, bg[i], 0, 0))
    wspec_out = pl.BlockSpec((None, EXPERTS_PER_GROUP, D_EXPERT, D), lambda i, bg, nv: (layer, bg[i], 0, 0))
    return pl.pallas_call(
        _moe_kernel,
        out_shape=jax.ShapeDtypeStruct((nb * R, D), F32),
        grid_spec=pltpu.PrefetchScalarGridSpec(
            num_scalar_prefetch=2,
            grid=(nb,),
            in_specs=[pl.BlockSpec((R, W), lambda i, bg, nv: (i, 0)), wspec_in, wspec_in, wspec_out],
            out_specs=pl.BlockSpec((R, D), lambda i, bg, nv: (i, 0)),
            scratch_shapes=[pltpu.VMEM((EXPERTS_PER_GROUP, D, D_EXPERT), BF16),
                            pltpu.VMEM((EXPERTS_PER_GROUP, D, D_EXPERT), BF16),
                            pltpu.VMEM((EXPERTS_PER_GROUP, D_EXPERT, D), BF16)]),
        compiler_params=_cparams(1),
        name="moe",
    )(block_g, nvalid, hs, w1, w3, w2)


def _final_kernel(dest_ref, base_ref, ys_hbm, ada_ref, lng, lnb, o_ref, ybuf, sem):
    i = pl.program_id(0)
    n = pl.num_programs(0)
    tf = base_ref.shape[0]
    slot = i % 2

    def start_gather(step, s):
        def body(g, c):
            r0 = pl.multiple_of(g * 8, 8)
            for k in range(8):
                d = dest_ref[step * tf + r0 + k]
                pltpu.make_async_copy(ys_hbm.at[pl.ds(d, 1)], ybuf.at[s, pl.ds(r0 + k, 1)],
                                      sem.at[s]).start()
            return c

        lax.fori_loop(0, tf // 8, body, 0)

    @pl.when(i == 0)
    def _():
        start_gather(0, 0)

    @pl.when(i + 1 < n)
    def _():
        start_gather(i + 1, 1 - slot)

    pltpu.make_async_copy(ys_hbm.at[pl.ds(0, tf)], ybuf.at[slot], sem.at[slot]).wait()
    ada = ada_ref[...]
    o_ref[...] = _ln_plain(base_ref[...] + ada[5:6, :] * ybuf[slot]) * lng[...] + lnb[...]


def _final(base, ys, dest, ada_l, lng, lnb, B, S):
    T, D = base.shape
    tf = FINAL_ROWS
    per_batch = S // tf
    return pl.pallas_call(
        _final_kernel,
        out_shape=jax.ShapeDtypeStruct((T, D), F32),
        grid_spec=pltpu.PrefetchScalarGridSpec(
            num_scalar_prefetch=1,
            grid=(T // tf,),
            in_specs=[pl.BlockSpec((tf, D), lambda i, d: (i, 0)),
                      pl.BlockSpec(memory_space=pl.ANY),
                      pl.BlockSpec((None, 6, D), lambda i, d: (i // per_batch, 0, 0)),
                      pl.BlockSpec((1, D), lambda i, d: (0, 0)), pl.BlockSpec((1, D), lambda i, d: (0, 0))],
            out_specs=pl.BlockSpec((tf, D), lambda i, d: (i, 0)),
            scratch_shapes=[pltpu.VMEM((2, tf, D), F32), pltpu.SemaphoreType.DMA((2,))]),
        compiler_params=_cparams(1),
        name="final_ln",
    )(dest, base, ys, ada_l, lng, lnb)


def _rope_tables(positions):
    inv_freq = ROPE_THETA ** (-jnp.arange(0, QK_ROPE, 2, dtype=F32) / QK_ROPE)
    ang = positions.astype(F32)[..., None] * inv_freq
    dense = lax.optimization_barrier(ang.reshape(-1, LANE))
    cos, sin = lax.optimization_barrier((jnp.cos(dense), jnp.sin(dense)))
    cos, sin = cos.reshape(ang.shape), sin.reshape(ang.shape)
    ones = jnp.ones(cos.shape[:-1] + (QK_NOPE,), F32)
    zpad = jnp.zeros(cos.shape[:-1] + (HEAD_PAD - QK_NOPE - QK_ROPE,), F32)
    cos128 = jnp.concatenate([ones, cos, cos, zpad], axis=-1)
    sin128 = jnp.concatenate([0.0 * ones, -sin, sin, zpad], axis=-1)
    T = positions.shape[0] * positions.shape[1]
    return cos128.reshape(T, HEAD_PAD), sin128.reshape(T, HEAD_PAD)


def _swap_halves(w):
    half = w.shape[-1] // 2
    return jnp.concatenate([w[..., half:], w[..., :half]], axis=-1)


def _layer_weights(l, p):
    D = p["w_in"].shape[1]
    w_in = p["w_in"][l]
    o = 0
    seg = {}
    for name, width in (("cv", 3 * D_CONV), ("ssm", D_SSM), ("cq", Q_LORA), ("ckv", KV_LORA),
                        ("kr", QK_ROPE), ("g", 3 * D)):
        seg[name] = w_in[:, o:o + width]
        o += width
    pad_r = HEAD_PAD - QK_NOPE - QK_ROPE
    zl = jnp.zeros((D, QK_NOPE), F32)
    zr = jnp.zeros((D, pad_r), F32)
    w = {
        "w_cv": seg["cv"].astype(BF16), "w_ssm": seg["ssm"].astype(BF16),
        "w_lat": jnp.concatenate([seg["cq"], seg["ckv"], zl, seg["kr"], zr,
                                  zl, _swap_halves(seg["kr"]), zr], axis=1).astype(BF16),
        "w_g": seg["g"].astype(BF16),
        "convw": p["conv_w"][l],
        "qn": p["q_norm"][l].reshape(1, Q_LORA), "kvn": p["kv_norm"][l].reshape(1, KV_LORA),
        "w_upc": p["w_up_conv"][l].astype(BF16),
    }
    scale = (QK_NOPE + QK_ROPE) ** -0.5 * math.log2(math.e)
    wq = p["w_uq"][l].reshape(Q_LORA, MLA_HEADS, QK_NOPE + QK_ROPE) * scale
    zq = jnp.zeros((Q_LORA, MLA_HEADS, pad_r), F32)
    zn = jnp.zeros((Q_LORA, MLA_HEADS, QK_NOPE), F32)
    w["wq_a"] = jnp.concatenate([wq, zq], axis=-1).reshape(Q_LORA, -1).astype(BF16)
    w["wq_b"] = jnp.concatenate([zn, _swap_halves(wq[..., QK_NOPE:]), zq],
                                axis=-1).reshape(Q_LORA, -1).astype(BF16)
    wk = p["w_uk"][l].reshape(KV_LORA, MLA_HEADS, QK_NOPE)
    zk = jnp.zeros((KV_LORA, MLA_HEADS, HEAD_PAD - QK_NOPE), F32)
    w["wk"] = jnp.concatenate([wk, zk], axis=-1).reshape(KV_LORA, -1).astype(BF16)
    w["wv"] = p["w_uv"][l].T.astype(BF16)

    ar, ai = p["ssm_a_re"][l], p["ssm_a_im"][l]
    dt = jnp.exp(p["ssm_log_dt"][l])[:, None]
    mag = jnp.exp(ar * dt)
    lb_re, lb_im = mag * jnp.cos(ai * dt), mag * jnp.sin(ai * dt)
    den = ar * ar + ai * ai
    nr, ni = lb_re - 1.0, lb_im
    f_re, f_im = (nr * ar + ni * ai) / den, (ni * ar - nr * ai) / den
    br, bi = p["ssm_b_re"][l], p["ssm_b_im"][l]
    bb_re = f_re[..., None] * br - f_im[..., None] * bi
    bb_im = f_re[..., None] * bi + f_im[..., None] * br
    gpc = LANE // SSM_GROUP_DIM
    nch = SSM_GROUPS // gpc
    eye = jnp.eye(gpc, dtype=F32)

    def bmat(bb):
        return jnp.einsum("cgnp,gh->cgphn", bb.reshape(nch, gpc, SSM_STATE, SSM_GROUP_DIM),
                          eye).reshape(nch, LANE, gpc * SSM_STATE).astype(BF16)

    def cmat(cc):
        return jnp.einsum("cgpn,gh->cgnhp", cc.reshape(nch, gpc, SSM_GROUP_DIM, SSM_STATE),
                          eye).reshape(nch, gpc * SSM_STATE, LANE).astype(BF16)

    w["ssm_bre"], w["ssm_bim"] = bmat(bb_re), bmat(bb_im)
    w["ssm_cre"], w["ssm_cim"] = cmat(p["ssm_c_re"][l]), cmat(-p["ssm_c_im"][l])
    nrow = 8
    w["ssm_lre"] = jnp.broadcast_to(lb_re.reshape(1, -1), (nrow, SSM_GROUPS * SSM_STATE))
    w["ssm_lim"] = jnp.broadcast_to(lb_im.reshape(1, -1), (nrow, SSM_GROUPS * SSM_STATE))
    w["ssm_d"] = p["ssm_d"][l].reshape(1, D_SSM)
    w["ssm_wglu"] = p["ssm_w_glu"][l].astype(BF16)

    w["w_ups"] = p["w_up_ssm"][l].astype(BF16)
    w["w_upa"] = p["w_up_attn"][l].astype(BF16)
    w["w_o"] = p["w_o"][l].astype(BF16)
    w["lng0"], w["lnb0"] = p["ln_g"][l, 0].reshape(1, D), p["ln_b"][l, 0].reshape(1, D)
    w["lng1"], w["lnb1"] = p["ln_g"][l, 1].reshape(1, D), p["ln_b"][l, 1].reshape(1, D)
    rwg = p["router_w"].reshape(D, N_EXPERT_GROUPS, EXPERTS_PER_GROUP)
    rbg = p["router_bias"].reshape(N_EXPERT_GROUPS, EXPERTS_PER_GROUP)
    w["rw"] = jnp.transpose(rwg, (2, 1, 0)).reshape(N_EXPERTS, D)
    w["rb"] = jnp.transpose(rbg, (1, 0)).reshape(N_EXPERTS, 1)
    w["ws1"] = p["shared_w1"][l].astype(BF16)
    w["ws3"] = p["shared_w3"][l].astype(BF16)
    w["ws2"] = p["shared_w2"][l].astype(BF16)
    return w


def kernel(x, c, positions, w_ada, b_ada, w_in, conv_w, ssm_a_re, ssm_a_im, ssm_b_re, ssm_b_im,
           ssm_c_re, ssm_c_im, ssm_d, ssm_log_dt, ssm_w_glu, q_norm, w_uq, kv_norm, w_uk, w_uv,
           w_up_conv, w_up_ssm, w_up_attn, w_o, ln_g, ln_b, router_w, router_bias,
           exp_w1, exp_w3, exp_w2, shared_w1, shared_w3, shared_w2):
    B, S, D = x.shape
    L = w_in.shape[0]
    T = B * S
    assert B == 8 and S % max(TM_PROJ, T_CHUNK, TQ, FINAL_ROWS) == 0 and T % max(MOE_ROWS, PERM_ROWS) == 0
    assert TM_PROJ == TQ
    p = dict(w_in=w_in, conv_w=conv_w, ssm_a_re=ssm_a_re, ssm_a_im=ssm_a_im, ssm_b_re=ssm_b_re,
             ssm_b_im=ssm_b_im, ssm_c_re=ssm_c_re, ssm_c_im=ssm_c_im, ssm_d=ssm_d,
             ssm_log_dt=ssm_log_dt, ssm_w_glu=ssm_w_glu, q_norm=q_norm, w_uq=w_uq, kv_norm=kv_norm,
             w_uk=w_uk, w_uv=w_uv, w_up_conv=w_up_conv, w_up_ssm=w_up_ssm, w_up_attn=w_up_attn,
             w_o=w_o, ln_g=ln_g, ln_b=ln_b, router_w=router_w, router_bias=router_bias,
             exp_w1=exp_w1, exp_w3=exp_w3, exp_w2=exp_w2, shared_w1=shared_w1,
             shared_w3=shared_w3, shared_w2=shared_w2)
    alpha = (2 * L) ** 0.25
    cos128, sin128 = _rope_tables(positions)
    ada = _ada(c, w_ada, b_ada).reshape(L, B, 6, D)
    x2d = x.reshape(T, D)
    for l in range(L):
        wts = _layer_weights(l, p)
        pc, sgs, sga, u, q, k, v = _inproj(x2d, ada[l], cos128, sin128, wts, B, S)
        ys = _ssm(u.reshape(B, S, D_SSM), wts, B, S).reshape(T, D_SSM)
        ya = _attention(q, k, v, B, S)
        base, hr, rt = _merge(x2d, ada[l], pc, sgs, sga, ys, ya, wts, B, S, alpha)
        gid = rt[EXPERTS_PER_GROUP].astype(jnp.int32)
        dest, block_g, nvalid = _route_plan(gid, T)
        hs = _permute(hr, dest, nvalid)
        ysort = _moe(hs, block_g, nvalid, exp_w1, exp_w3, exp_w2, l)
        x2d = _final(base, ysort, dest, ada[l], wts["lng1"], wts["lnb1"], B, S)
    return x2d.reshape(B, S, D)
```

```python
import functools
import math

import jax
import jax.numpy as jnp
from jax import lax
from jax.experimental import pallas as pl
from jax.experimental.pallas import tpu as pltpu

F32 = jnp.float32
BF16 = jnp.bfloat16

D_CONV = 512
D_SSM = 512
SSM_GROUP_DIM = 16
SSM_GROUPS = 32
SSM_STATE = 64
MLA_HEADS = 8
QK_NOPE = 64
QK_ROPE = 32
V_HEAD = 64
Q_LORA = 256
KV_LORA = 128
ROPE_THETA = 10000.0
N_EXPERTS = 32
N_EXPERT_GROUPS = 8
EXPERTS_PER_GROUP = 4
D_EXPERT = 256
LN_EPS = 1e-5
RMS_EPS = 1e-6

LANE = 128
HEAD_PAD = 128
VMEM_LIMIT = 56 * 1024 * 1024

TM_PROJ = 512
T_CHUNK = 64
TQ = 512
ATTN_HEADS = 8
DEN_ROWS = 16
MOE_ROWS = 512
PERM_ROWS = 2048
FINAL_ROWS = 1024
ROUTE_W = 128


def _cparams(n_axes):
    return pltpu.CompilerParams(dimension_semantics=("arbitrary",) * n_axes,
                                vmem_limit_bytes=VMEM_LIMIT)


def _ln_plain(x):
    mu = jnp.mean(x, axis=-1, keepdims=True)
    xc = x - mu
    var = jnp.mean(xc * xc, axis=-1, keepdims=True)
    return xc * lax.rsqrt(var + LN_EPS)


def _rms(x, g):
    return x * lax.rsqrt(jnp.mean(x * x, axis=-1, keepdims=True) + RMS_EPS) * g


def _bdot(a, b):
    return jnp.dot(a.astype(BF16), b, preferred_element_type=F32)


def _ada_kernel(c_ref, w_ref, b_ref, o_ref):
    c = c_ref[...]
    cond = c * jax.nn.sigmoid(c)
    o_ref[...] = jnp.dot(cond, w_ref[...], precision=lax.Precision.HIGHEST,
                         preferred_element_type=F32) + b_ref[...]


def _ada(c, w_ada, b_ada):
    L, D, D6 = w_ada.shape
    B = c.shape[0]
    wn = D6 // 2
    nj = D6 // wn
    return pl.pallas_call(
        _ada_kernel,
        out_shape=jax.ShapeDtypeStruct((L, B, D6), F32),
        grid=(L, nj),
        in_specs=[pl.BlockSpec((B, D), lambda l, j: (0, 0)),
                  pl.BlockSpec((None, D, wn), lambda l, j: (l, 0, j)),
                  pl.BlockSpec((None, 1, wn), lambda l, j: (l, 0, j))],
        out_specs=pl.BlockSpec((None, B, wn), lambda l, j: (l, 0, j)),
        compiler_params=_cparams(2),
        name="ada",
    )(c, w_ada, b_ada.reshape(L, 1, D6))


def _inproj_kernel(x_ref, ada_ref, cos_ref, sin_ref,
                   w_cv, w_ssm, w_lat, w_g,
                   convw, qn, wq_a, wq_b, kvn, wk, wv, w_upc,
                   pc_ref, sgs_ref, sga_ref, u_ref, q_ref, k_ref, v_ref,
                   carry_ref):
    si = pl.program_id(1)
    D = x_ref.shape[1]
    tm = x_ref.shape[0]

    @pl.when(si == 0)
    def _():
        carry_ref[...] = jnp.zeros_like(carry_ref)

    ada = ada_ref[...]
    h = _ln_plain(x_ref[...]) * (1.0 + ada[1:2, :]) + ada[0:1, :]
    hb = h.astype(BF16)

    pcv = jnp.dot(hb, w_cv[...], preferred_element_type=F32)
    u = pcv[:, D_CONV:2 * D_CONV] * pcv[:, :D_CONV]
    gb = pcv[:, 2 * D_CONV:]
    tail = carry_ref[...]
    row8 = lax.broadcasted_iota(jnp.int32, (8, D_CONV), 0)
    r1 = pltpu.roll(u, 1, 0)
    r2 = pltpu.roll(u, 2, 0)
    t1 = pltpu.roll(tail, 1, 0)
    t2 = pltpu.roll(tail, 2, 0)
    u1 = jnp.concatenate([jnp.where(row8 < 1, t1, r1[:8]), r1[8:]], axis=0)
    u2 = jnp.concatenate([jnp.where(row8 < 2, t2, r2[:8]), r2[8:]], axis=0)
    cw = convw[...]
    yc = gb * (cw[0:1, :] * u2 + cw[1:2, :] * u1 + cw[2:3, :] * u)
    carry_ref[...] = u[tm - 8:, :]

    gates = jnp.dot(hb, w_g[...], preferred_element_type=F32)
    pc_ref[...] = (jax.nn.sigmoid(gates[:, :D]) * _bdot(yc, w_upc[...])).astype(BF16)
    sgs_ref[...] = jax.nn.sigmoid(gates[:, D:2 * D]).astype(BF16)
    sga_ref[...] = jax.nn.sigmoid(gates[:, 2 * D:]).astype(BF16)

    u_ref[...] = jnp.dot(hb, w_ssm[...], preferred_element_type=F32).astype(BF16)

    cos = cos_ref[...]
    sin = sin_ref[...]
    lat = jnp.dot(hb, w_lat[...], preferred_element_type=F32)
    o_kv, o_kr = Q_LORA, Q_LORA + KV_LORA
    cq = _rms(lat[:, :o_kv], qn[...]).astype(BF16)
    qa = jnp.dot(cq, wq_a[...], preferred_element_type=F32)
    qb = jnp.dot(cq, wq_b[...], preferred_element_type=F32)
    ckv = _rms(lat[:, o_kv:o_kr], kvn[...]).astype(BF16)
    kn = jnp.dot(ckv, wk[...], preferred_element_type=F32)
    kr = lat[:, o_kr:o_kr + HEAD_PAD] * cos + lat[:, o_kr + HEAD_PAD:] * sin
    for hd in range(MLA_HEADS):
        sl = slice(hd * HEAD_PAD, (hd + 1) * HEAD_PAD)
        q_ref[:, sl] = (qa[:, sl] * cos + qb[:, sl] * sin).astype(BF16)
        k_ref[:, sl] = (kn[:, sl] + kr).astype(BF16)
    v_ref[...] = lax.dot_general(wv[...], ckv, (((1,), (1,)), ((), ())),
                                 preferred_element_type=F32).astype(BF16)


def _inproj(x2d, ada_l, cos128, sin128, wts, B, S):
    T, D = x2d.shape
    tm = TM_PROJ
    ns = S // tm
    row = lambda b, s: (b * ns + s, 0)
    full = lambda a: pl.BlockSpec(a.shape, lambda b, s: (0,) * a.ndim, pipeline_mode=pl.Buffered(1))
    names = ["w_cv", "w_ssm", "w_lat", "w_g", "convw", "qn",
             "wq_a", "wq_b", "kvn", "wk", "wv", "w_upc"]
    ws = [wts[n] for n in names]
    out_shape = (
        jax.ShapeDtypeStruct((T, D), BF16),
        jax.ShapeDtypeStruct((T, D), BF16),
        jax.ShapeDtypeStruct((T, D), BF16),
        jax.ShapeDtypeStruct((T, D_SSM), BF16),
        jax.ShapeDtypeStruct((T, MLA_HEADS * HEAD_PAD), BF16),
        jax.ShapeDtypeStruct((T, MLA_HEADS * HEAD_PAD), BF16),
        jax.ShapeDtypeStruct((B, ns, MLA_HEADS * V_HEAD, tm), BF16),
    )
    out_specs = (
        pl.BlockSpec((tm, D), row), pl.BlockSpec((tm, D), row), pl.BlockSpec((tm, D), row),
        pl.BlockSpec((tm, D_SSM), row),
        pl.BlockSpec((tm, MLA_HEADS * HEAD_PAD), row),
        pl.BlockSpec((tm, MLA_HEADS * HEAD_PAD), row),
        pl.BlockSpec((None, None, MLA_HEADS * V_HEAD, tm), lambda b, s: (b, s, 0, 0)),
    )
    return pl.pallas_call(
        _inproj_kernel,
        out_shape=out_shape,
        grid=(B, ns),
        in_specs=[pl.BlockSpec((tm, D), row),
                  pl.BlockSpec((None, 6, D), lambda b, s: (b, 0, 0)),
                  pl.BlockSpec((tm, LANE), row), pl.BlockSpec((tm, LANE), row)]
                 + [full(a) for a in ws],
        out_specs=out_specs,
        scratch_shapes=[pltpu.VMEM((8, D_CONV), F32)],
        compiler_params=_cparams(2),
        name="inproj",
    )(x2d, ada_l, cos128, sin128, *ws)


def _ssm_kernel(u_ref, bre_ref, bim_ref, lre_ref, lim_ref, cre_ref, cim_ref, d_ref, wglu_ref, perm_ref,
                o_ref, are, aim, au, bre_s, bim_s, bu, st_re, st_im, tmaj):
    i = pl.program_id(0)
    batch, nsteps = u_ref.shape[0], u_ref.shape[1]
    rows = batch * nsteps
    nchunk = bre_ref.shape[0]
    cw = bre_ref.shape[2]

    @pl.when(i == 0)
    def _():
        st_re[...] = jnp.zeros_like(st_re)
        st_im[...] = jnp.zeros_like(st_im)
        bre_s[...] = jnp.zeros_like(bre_s)
        bim_s[...] = jnp.zeros_like(bim_s)
        bu[...] = jnp.zeros_like(bu)

    def body(cur_re, cur_im, cur_u, prev_re, prev_im, prev_u):
        for b in range(batch):
            tmaj[:, b, :] = u_ref[b].astype(F32)
        uf = tmaj[...].reshape(rows, D_SSM)
        cur_u[...] = uf
        ub = uf.astype(BF16)
        for c in range(nchunk):
            uc = ub[:, c * LANE:(c + 1) * LANE]
            cur_re[:, c * cw:(c + 1) * cw] = jnp.dot(uc, bre_ref[c], preferred_element_type=F32)
            cur_im[:, c * cw:(c + 1) * cw] = jnp.dot(uc, bim_ref[c], preferred_element_type=F32)

        ys = []

        def out_chunk(c):
            ys.append(_bdot(prev_re[:, c * cw:(c + 1) * cw], cre_ref[c])
                      + _bdot(prev_im[:, c * cw:(c + 1) * cw], cim_ref[c]))

        def out_finish():
            y = jnp.concatenate(ys, axis=1) + d_ref[...] * prev_u[...]
            y = 0.5 * y * (1.0 + jnp.tanh(math.sqrt(2.0 / math.pi) * (y + 0.044715 * (y * y * y))))
            z = _bdot(y, wglu_ref[...])
            g = (z[:, :D_SSM] * jax.nn.sigmoid(z[:, D_SSM:])).astype(BF16)
            o_ref[...] = jnp.dot(perm_ref[...], g, preferred_element_type=F32).astype(BF16).reshape(
                batch, nsteps, D_SSM)

        pieces = [functools.partial(out_chunk, c) for c in range(nchunk)] + [out_finish]
        every = nsteps // len(pieces)
        pr, pi = st_re[...], st_im[...]
        for t in range(nsteps):
            r = t * batch
            lre = lre_ref[...]
            lim = lim_ref[...]
            nr = lre * pr - lim * pi + cur_re[r:r + batch, :]
            ni = lre * pi + lim * pr + cur_im[r:r + batch, :]
            cur_re[r:r + batch, :] = nr
            cur_im[r:r + batch, :] = ni
            pr, pi = nr, ni
            if t % every == every - 1 and pieces:
                pieces.pop(0)()
        while pieces:
            pieces.pop(0)()
        st_re[...] = pr
        st_im[...] = pi

    @pl.when(i % 2 == 0)
    def _():
        body(are, aim, au, bre_s, bim_s, bu)

    @pl.when(i % 2 == 1)
    def _():
        body(bre_s, bim_s, bu, are, aim, au)


def _ssm(u, wts, B, S):
    rows = T_CHUNK * B
    n = S // T_CHUNK
    nstate = SSM_GROUPS * SSM_STATE
    names = ["ssm_bre", "ssm_bim", "ssm_lre", "ssm_lim", "ssm_cre", "ssm_cim", "ssm_d", "ssm_wglu"]
    r = jnp.arange(rows, dtype=jnp.int32)
    src = (r % T_CHUNK) * B + r // T_CHUNK
    perm = (src[:, None] == jnp.arange(rows, dtype=jnp.int32)[None, :]).astype(BF16)
    ws = [wts[k] for k in names] + [perm]
    full = lambda a: pl.BlockSpec(a.shape, lambda i: (0,) * a.ndim, pipeline_mode=pl.Buffered(1))
    return pl.pallas_call(
        _ssm_kernel,
        out_shape=jax.ShapeDtypeStruct((B, S, D_SSM), BF16),
        grid=(n + 1,),
        in_specs=[pl.BlockSpec((B, T_CHUNK, D_SSM), lambda i: (0, jnp.minimum(i, n - 1), 0))]
                 + [full(a) for a in ws],
        out_specs=pl.BlockSpec((B, T_CHUNK, D_SSM), lambda i: (0, jnp.maximum(i - 1, 0), 0)),
        scratch_shapes=[pltpu.VMEM((rows, nstate), F32), pltpu.VMEM((rows, nstate), F32),
                        pltpu.VMEM((rows, D_SSM), F32),
                        pltpu.VMEM((rows, nstate), F32), pltpu.VMEM((rows, nstate), F32),
                        pltpu.VMEM((rows, D_SSM), F32),
                        pltpu.VMEM((B, nstate), F32), pltpu.VMEM((B, nstate), F32),
                        pltpu.VMEM((T_CHUNK, B, D_SSM), F32)],
        compiler_params=_cparams(1),
        name="ssm",
    )(u, *ws)


def _attn_kernel(q_ref, k_ref, vt_ref, o_ref, m_sc, acc_sc, sa_sc, sb_sc, p_sc):
    qi = pl.program_id(2)
    tq = q_ref.shape[0]
    nh = m_sc.shape[0]
    neg = -1e30
    m_sc[...] = jnp.full(m_sc.shape, neg, F32)
    acc_sc[...] = jnp.zeros(acc_sc.shape, F32)
    ones = jnp.ones((DEN_ROWS, tq), BF16)

    def scores(j, s_sc):
        off = pl.multiple_of(j * tq, tq)
        for hh in range(nh):
            q = q_ref[:, hh * HEAD_PAD:(hh + 1) * HEAD_PAD]
            kb = k_ref[pl.ds(off, tq), hh * HEAD_PAD:(hh + 1) * HEAD_PAD]
            s_sc[hh] = lax.dot_general(kb, q, (((1,), (1,)), ((), ())),
                                       preferred_element_type=F32)

    def update(j, s_sc, masked):
        scale = []
        for hh in range(nh):
            if masked:
                keyi = lax.broadcasted_iota(jnp.int32, (tq, tq), 0)
                qryi = lax.broadcasted_iota(jnp.int32, (tq, tq), 1)
                s_sc[hh] = jnp.where(keyi <= qryi, s_sc[hh], neg)
            m_prev = m_sc[hh]
            m_new = jnp.maximum(m_prev, jnp.max(s_sc[hh], axis=0, keepdims=True))
            m_sc[hh] = m_new
            p_sc[hh] = jnp.exp2(s_sc[hh] - m_new).astype(BF16)
            scale.append(jnp.exp2(m_prev - m_new))
        for hh in range(nh):
            vte = jnp.concatenate([vt_ref[j, hh * V_HEAD:(hh + 1) * V_HEAD, :], ones], axis=0)
            acc_sc[hh] = scale[hh] * acc_sc[hh] + jnp.dot(vte, p_sc[hh], preferred_element_type=F32)

    scores(0, sa_sc)

    def pair(pp, c):
        j = 2 * pp
        scores(j + 1, sb_sc)
        update(j, sa_sc, False)
        scores(j + 2, sa_sc)
        update(j + 1, sb_sc, False)
        return c

    npair = qi // 2
    lax.fori_loop(0, npair, pair, 0)
    j0 = 2 * npair

    @pl.when(qi % 2 == 0)
    def _():
        update(j0, sa_sc, True)

    @pl.when(qi % 2 == 1)
    def _():
        scores(j0 + 1, sb_sc)
        update(j0, sa_sc, False)
        update(j0 + 1, sb_sc, True)

    outs = []
    for pr in range(nh // 2):
        halves = []
        for hh in (2 * pr, 2 * pr + 1):
            acc = acc_sc[hh]
            halves.append(acc[:V_HEAD] * (1.0 / acc[V_HEAD:V_HEAD + 1]))
        outs.append(jnp.concatenate(halves, axis=0).T)
    o_ref[...] = jnp.concatenate(outs, axis=1).astype(BF16)


def _attention(q, k, vt, B, S):
    nq = S // TQ
    nh = ATTN_HEADS
    hg = MLA_HEADS // nh
    return pl.pallas_call(
        _attn_kernel,
        out_shape=jax.ShapeDtypeStruct((B * S, MLA_HEADS * V_HEAD), BF16),
        grid=(B, hg, nq),
        in_specs=[pl.BlockSpec((TQ, nh * HEAD_PAD), lambda b, h, i: (b * nq + i, h)),
                  pl.BlockSpec((S, nh * HEAD_PAD), lambda b, h, i: (b, h)),
                  pl.BlockSpec((None, nq, nh * V_HEAD, TQ), lambda b, h, i: (b, 0, h, 0))],
        out_specs=pl.BlockSpec((TQ, nh * V_HEAD), lambda b, h, i: (b * nq + i, h)),
        scratch_shapes=[pltpu.VMEM((nh, 1, TQ), F32),
                        pltpu.VMEM((nh, V_HEAD + DEN_ROWS, TQ), F32),
                        pltpu.VMEM((nh, TQ, TQ), F32), pltpu.VMEM((nh, TQ, TQ), F32),
                        pltpu.VMEM((nh, TQ, TQ), BF16)],
        compiler_params=_cparams(3),
        name="attention",
    )(q, k, vt)


def _merge_kernel(x_ref, ada_ref, pc_ref, sgs_ref, sga_ref, ys_ref, ya_ref,
                  w_ups, w_upa, w_o, lng, lnb, rw, rb, ws1, ws3, ws2,
                  base_ref, hr_ref, rt_ref, *, alpha):
    D = x_ref.shape[1]
    tm = x_ref.shape[0]
    ada = ada_ref[...]
    merged = (pc_ref[...].astype(F32)
              + sgs_ref[...].astype(F32) * jnp.dot(ys_ref[...], w_ups[...], preferred_element_type=F32)
              + sga_ref[...].astype(F32) * jnp.dot(ya_ref[...], w_upa[...], preferred_element_type=F32))
    y = _bdot(merged, w_o[...])
    x1 = _ln_plain(alpha * x_ref[...] + ada[2:3, :] * y) * lng[...] + lnb[...]
    h2 = _ln_plain(x1) * (1.0 + ada[4:5, :]) + ada[3:4, :]
    hr_ref[:, :D] = h2

    G = N_EXPERT_GROUPS
    logits = lax.dot_general(rw[...], h2, (((1,), (1,)), ((), ())),
                             precision=lax.Precision.HIGHEST, preferred_element_type=F32)
    sc0 = jax.nn.sigmoid(logits)
    sl0 = sc0 + rb[...]
    sc = [sc0[j * G:(j + 1) * G] for j in range(EXPERTS_PER_GROUP)]
    sl = [sl0[j * G:(j + 1) * G] for j in range(EXPERTS_PER_GROUP)]
    hi1, lo1 = jnp.maximum(sl[0], sl[1]), jnp.minimum(sl[0], sl[1])
    hi2, lo2 = jnp.maximum(sl[2], sl[3]), jnp.minimum(sl[2], sl[3])
    gscore = jnp.maximum(hi1, hi2) + jnp.maximum(jnp.minimum(hi1, hi2), jnp.maximum(lo1, lo2))
    sub = lax.broadcasted_iota(jnp.int32, (G, tm), 0)
    gmax = jnp.max(gscore, axis=0, keepdims=True)
    gidx = jnp.min(jnp.where(gscore == gmax, sub, G), axis=0, keepdims=True)
    pick = sub == gidx
    vs = [jnp.sum(jnp.where(pick, a, 0.0), axis=0, keepdims=True) for a in sl]
    ss = [jnp.sum(jnp.where(pick, a, 0.0), axis=0, keepdims=True) for a in sc]
    chosen = []
    for j in range(EXPERTS_PER_GROUP):
        cnt = jnp.zeros((1, tm), jnp.int32)
        for k in range(EXPERTS_PER_GROUP):
            if k == j:
                continue
            beats = (vs[k] > vs[j]) | ((vs[k] == vs[j]) & (k < j))
            cnt = cnt + beats.astype(jnp.int32)
        chosen.append(cnt < 2)
    wsel = [jnp.where(chosen[j], ss[j], 0.0) for j in range(EXPERTS_PER_GROUP)]
    wsum = wsel[0] + wsel[1] + wsel[2] + wsel[3]
    rows = [wsel[j] / wsum for j in range(EXPERTS_PER_GROUP)] + [gidx.astype(F32)]
    rows += [jnp.zeros((1, tm), F32)] * (8 - len(rows))
    route_t = jnp.concatenate(rows, axis=0)
    rt_ref[...] = route_t
    eye = (lax.broadcasted_iota(jnp.int32, (8, ROUTE_W), 0)
           == lax.broadcasted_iota(jnp.int32, (8, ROUTE_W), 1)).astype(F32)
    hr_ref[:, D:] = lax.dot_general(route_t, eye, (((0,), (0,)), ((), ())),
                                    precision=lax.Precision.HIGHEST, preferred_element_type=F32)

    h2b = h2.astype(BF16)
    a1 = jnp.dot(h2b, ws1[...], preferred_element_type=F32)
    a3 = jnp.dot(h2b, ws3[...], preferred_element_type=F32)
    ysh = _bdot(a1 * jax.nn.sigmoid(a1) * a3, ws2[...])
    base_ref[...] = alpha * x1 + ada[5:6, :] * ysh


def _merge(x2d, ada_l, pc, sgs, sga, ys, ya, wts, B, S, alpha):
    T, D = x2d.shape
    tm = TM_PROJ
    ns = S // tm
    row = lambda b, s: (b * ns + s, 0)
    names = ["w_ups", "w_upa", "w_o", "lng0", "lnb0", "rw", "rb", "ws1", "ws3", "ws2"]
    ws = [wts[n] for n in names]
    full = lambda a: pl.BlockSpec(a.shape, lambda b, s: (0,) * a.ndim, pipeline_mode=pl.Buffered(1))
    return pl.pallas_call(
        functools.partial(_merge_kernel, alpha=alpha),
        out_shape=(jax.ShapeDtypeStruct((T, D), F32),
                   jax.ShapeDtypeStruct((T, D + ROUTE_W), F32),
                   jax.ShapeDtypeStruct((8, T), F32)),
        grid=(B, ns),
        in_specs=[pl.BlockSpec((tm, D), row),
                  pl.BlockSpec((None, 6, D), lambda b, s: (b, 0, 0)),
                  pl.BlockSpec((tm, D), row), pl.BlockSpec((tm, D), row), pl.BlockSpec((tm, D), row),
                  pl.BlockSpec((tm, D_SSM), row),
                  pl.BlockSpec((tm, MLA_HEADS * V_HEAD), row)]
                 + [full(a) for a in ws],
        out_specs=(pl.BlockSpec((tm, D), row), pl.BlockSpec((tm, D + ROUTE_W), row),
                   pl.BlockSpec((8, tm), lambda b, s: (0, b * ns + s))),
        compiler_params=_cparams(2),
        name="merge",
    )(x2d, ada_l, pc, sgs, sga, ys, ya, *ws)


def _route_plan(gid, T):
    R = MOE_ROWS
    G = N_EXPERT_GROUPS
    onehot = (gid[:, None] == jnp.arange(G, dtype=jnp.int32)[None, :]).astype(jnp.int32)
    csum = jnp.cumsum(onehot, axis=0)
    counts = csum[-1]
    rank = jnp.sum(onehot * csum, axis=1) - 1
    padded = (counts + R - 1) // R * R
    pad_end = jnp.cumsum(padded)
    pad_start = pad_end - padded
    dest = jnp.sum(onehot * pad_start[None, :], axis=1) + rank
    nb = T // R + G
    starts = jnp.arange(nb, dtype=jnp.int32) * R
    block_g = jnp.minimum(jnp.sum((starts[:, None] >= pad_end[None, :]).astype(jnp.int32), axis=1), G - 1)
    gsel = (block_g[:, None] == jnp.arange(G, dtype=jnp.int32)[None, :]).astype(jnp.int32)
    group_end = jnp.sum(gsel * (pad_start + counts)[None, :], axis=1)
    nvalid = jnp.clip(group_end - starts, 0, R)
    return dest.astype(jnp.int32), block_g.astype(jnp.int32), nvalid.astype(jnp.int32)


def _permute_kernel(dest_ref, nvalid_ref, hr_ref, hs_hbm, zbuf, sem, zsem):
    i = pl.program_id(0)
    tp = hr_ref.shape[0]
    R = zbuf.shape[0]
    nb = nvalid_ref.shape[0]

    @pl.when(i == 0)
    def _():
        zbuf[...] = jnp.zeros(zbuf.shape, F32)

        def fill(b, n):
            partial = nvalid_ref[b] < R

            @pl.when(partial)
            def _():
                pltpu.make_async_copy(zbuf, hs_hbm.at[pl.ds(b * R, R)], zsem).start()

            return n + partial.astype(jnp.int32)

        nfill = lax.fori_loop(0, nb, fill, 0)

        def drain(k, c):
            pltpu.make_async_copy(zbuf, hs_hbm.at[pl.ds(0, R)], zsem).wait()
            return c

        lax.fori_loop(0, nfill, drain, 0)

    def body(g, c):
        r0 = pl.multiple_of(g * 8, 8)
        for k in range(8):
            d = dest_ref[i * tp + r0 + k]
            pltpu.make_async_copy(hr_ref.at[pl.ds(r0 + k, 1)], hs_hbm.at[pl.ds(d, 1)], sem).start()
        return c

    lax.fori_loop(0, tp // 8, body, 0)
    pltpu.make_async_copy(hr_ref, hs_hbm.at[pl.ds(0, tp)], sem).wait()


def _permute(hr, dest, nvalid):
    T, W = hr.shape
    R = MOE_ROWS
    tp = PERM_ROWS
    nb = nvalid.shape[0]
    return pl.pallas_call(
        _permute_kernel,
        out_shape=jax.ShapeDtypeStruct((nb * R, W), F32),
        grid_spec=pltpu.PrefetchScalarGridSpec(
            num_scalar_prefetch=2,
            grid=(T // tp,),
            in_specs=[pl.BlockSpec((tp, W), lambda i, d, n: (i, 0))],
            out_specs=pl.BlockSpec(memory_space=pl.ANY),
            scratch_shapes=[pltpu.VMEM((R, W), F32), pltpu.SemaphoreType.DMA, pltpu.SemaphoreType.DMA]),
        compiler_params=_cparams(1),
        name="permute",
    )(dest, nvalid, hr)


def _moe_kernel(bg_ref, nvalid_ref, hs_ref, w1_ref, w3_ref, w2_ref, y_ref, w1b, w3b, w2b):
    i = pl.program_id(0)
    R, D = y_ref.shape

    @pl.when((i == 0) | (bg_ref[i] != bg_ref[jnp.maximum(i - 1, 0)]))
    def _():
        w1b[...] = w1_ref[...].astype(BF16)
        w3b[...] = w3_ref[...].astype(BF16)
        w2b[...] = w2_ref[...].astype(BF16)

    @pl.when(nvalid_ref[i] > 0)
    def _():
        xb = hs_ref[:, :D].astype(BF16)
        acc = jnp.zeros((R, D), F32)
        for j in range(EXPERTS_PER_GROUP):
            a1 = jnp.dot(xb, w1b[j], preferred_element_type=F32)
            a3 = jnp.dot(xb, w3b[j], preferred_element_type=F32)
            hj = a1 * jax.nn.sigmoid(a1) * a3 * hs_ref[:, D + j:D + j + 1]
            acc = acc + _bdot(hj, w2b[j])
        y_ref[...] = acc

    @pl.when(nvalid_ref[i] == 0)
    def _():
        y_ref[...] = jnp.zeros((R, D), F32)


def _moe(hs, block_g, nvalid, w1, w3, w2, layer):
    D = w1.shape[2]
    W = hs.shape[1]
    R = MOE_ROWS
    nb = nvalid.shape[0]
    wspec_in = pl.BlockSpec((None, EXPERTS_PER_GROUP, D, D_EXPERT), lambda i, bg, nv: (layer, bg[i], 0, 0))
    wspec_out = pl.BlockSpec((None, EXPERTS_PER_GROUP, D_EXPERT, D), lambda i, bg, nv: (layer, bg[i], 0, 0))
    return pl.pallas_call(
        _moe_kernel,
        out_shape=jax.ShapeDtypeStruct((nb * R, D), F32),
        grid_spec=pltpu.PrefetchScalarGridSpec(
            num_scalar_prefetch=2,
            grid=(nb,),
            in_specs=[pl.BlockSpec((R, W), lambda i, bg, nv: (i, 0)), wspec_in, wspec_in, wspec_out],
            out_specs=pl.BlockSpec((R, D), lambda i, bg, nv: (i, 0)),
            scratch_shapes=[pltpu.VMEM((EXPERTS_PER_GROUP, D, D_EXPERT), BF16),
                            pltpu.VMEM((EXPERTS_PER_GROUP, D, D_EXPERT), BF16),
                            pltpu.VMEM((EXPERTS_PER_GROUP, D_EXPERT, D), BF16)]),
        compiler_params=_cparams(1),
        name="moe",
    )(block_g, nvalid, hs, w1, w3, w2)


def _final_kernel(dest_ref, base_ref, ys_hbm, ada_ref, lng, lnb, o_ref, ybuf, sem):
    i = pl.program_id(0)
    n = pl.num_programs(0)
    tf = base_ref.shape[0]
    slot = i % 2

    def start_gather(step, s):
        def body(g, c):
            r0 = pl.multiple_of(g * 8, 8)
            for k in range(8):
                d = dest_ref[step * tf + r0 + k]
                pltpu.make_async_copy(ys_hbm.at[pl.ds(d, 1)], ybuf.at[s, pl.ds(r0 + k, 1)],
                                      sem.at[s]).start()
            return c

        lax.fori_loop(0, tf // 8, body, 0)

    @pl.when(i == 0)
    def _():
        start_gather(0, 0)

    @pl.when(i + 1 < n)
    def _():
        start_gather(i + 1, 1 - slot)

    pltpu.make_async_copy(ys_hbm.at[pl.ds(0, tf)], ybuf.at[slot], sem.at[slot]).wait()
    ada = ada_ref[...]
    o_ref[...] = _ln_plain(base_ref[...] + ada[5:6, :] * ybuf[slot]) * lng[...] + lnb[...]


def _final(base, ys, dest, ada_l, lng, lnb, B, S):
    T, D = base.shape
    tf = FINAL_ROWS
    per_batch = S // tf
    return pl.pallas_call(
        _final_kernel,
        out_shape=jax.ShapeDtypeStruct((T, D), F32),
        grid_spec=pltpu.PrefetchScalarGridSpec(
            num_scalar_prefetch=1,
            grid=(T // tf,),
            in_specs=[pl.BlockSpec((tf, D), lambda i, d: (i, 0)),
                      pl.BlockSpec(memory_space=pl.ANY),
                      pl.BlockSpec((None, 6, D), lambda i, d: (i // per_batch, 0, 0)),
                      pl.BlockSpec((1, D), lambda i, d: (0, 0)), pl.BlockSpec((1, D), lambda i, d: (0, 0))],
            out_specs=pl.BlockSpec((tf, D), lambda i, d: (i, 0)),
            scratch_shapes=[pltpu.VMEM((2, tf, D), F32), pltpu.SemaphoreType.DMA((2,))]),
        compiler_params=_cparams(1),
        name="final_ln",
    )(dest, base, ys, ada_l, lng, lnb)


def _rope_tables(positions):
    inv_freq = ROPE_THETA ** (-jnp.arange(0, QK_ROPE, 2, dtype=F32) / QK_ROPE)
    ang = positions.astype(F32)[..., None] * inv_freq
    dense = lax.optimization_barrier(ang.reshape(-1, LANE))
    cos, sin = lax.optimization_barrier((jnp.cos(dense), jnp.sin(dense)))
    cos, sin = cos.reshape(ang.shape), sin.reshape(ang.shape)
    ones = jnp.ones(cos.shape[:-1] + (QK_NOPE,), F32)
    zpad = jnp.zeros(cos.shape[:-1] + (HEAD_PAD - QK_NOPE - QK_ROPE,), F32)
    cos128 = jnp.concatenate([ones, cos, cos, zpad], axis=-1)
    sin128 = jnp.concatenate([0.0 * ones, -sin, sin, zpad], axis=-1)
    T = positions.shape[0] * positions.shape[1]
    return cos128.reshape(T, HEAD_PAD), sin128.reshape(T, HEAD_PAD)


def _swap_halves(w):
    half = w.shape[-1] // 2
    return jnp.concatenate([w[..., half:], w[..., :half]], axis=-1)


def _layer_weights(l, p):
    D = p["w_in"].shape[1]
    w_in = p["w_in"][l]
    o = 0
    seg = {}
    for name, width in (("cv", 3 * D_CONV), ("ssm", D_SSM), ("cq", Q_LORA), ("ckv", KV_LORA),
                        ("kr", QK_ROPE), ("g", 3 * D)):
        seg[name] = w_in[:, o:o + width]
        o += width
    pad_r = HEAD_PAD - QK_NOPE - QK_ROPE
    zl = jnp.zeros((D, QK_NOPE), F32)
    zr = jnp.zeros((D, pad_r), F32)
    w = {
        "w_cv": seg["cv"].astype(BF16), "w_ssm": seg["ssm"].astype(BF16),
        "w_lat": jnp.concatenate([seg["cq"], seg["ckv"], zl, seg["kr"], zr,
                                  zl, _swap_halves(seg["kr"]), zr], axis=1).astype(BF16),
        "w_g": seg["g"].astype(BF16),
        "convw": p["conv_w"][l],
        "qn": p["q_norm"][l].reshape(1, Q_LORA), "kvn": p["kv_norm"][l].reshape(1, KV_LORA),
        "w_upc": p["w_up_conv"][l].astype(BF16),
    }
    scale = (QK_NOPE + QK_ROPE) ** -0.5 * math.log2(math.e)
    wq = p["w_uq"][l].reshape(Q_LORA, MLA_HEADS, QK_NOPE + QK_ROPE) * scale
    zq = jnp.zeros((Q_LORA, MLA_HEADS, pad_r), F32)
    zn = jnp.zeros((Q_LORA, MLA_HEADS, QK_NOPE), F32)
    w["wq_a"] = jnp.concatenate([wq, zq], axis=-1).reshape(Q_LORA, -1).astype(BF16)
    w["wq_b"] = jnp.concatenate([zn, _swap_halves(wq[..., QK_NOPE:]), zq],
                                axis=-1).reshape(Q_LORA, -1).astype(BF16)
    wk = p["w_uk"][l].reshape(KV_LORA, MLA_HEADS, QK_NOPE)
    zk = jnp.zeros((KV_LORA, MLA_HEADS, HEAD_PAD - QK_NOPE), F32)
    w["wk"] = jnp.concatenate([wk, zk], axis=-1).reshape(KV_LORA, -1).astype(BF16)
    w["wv"] = p["w_uv"][l].T.astype(BF16)

    ar, ai = p["ssm_a_re"][l], p["ssm_a_im"][l]
    dt = jnp.exp(p["ssm_log_dt"][l])[:, None]
    mag = jnp.exp(ar * dt)
    lb_re, lb_im = mag * jnp.cos(ai * dt), mag * jnp.sin(ai * dt)
    den = ar * ar + ai * ai
    nr, ni = lb_re - 1.0, lb_im
    f_re, f_im = (nr * ar + ni * ai) / den, (ni * ar - nr * ai) / den
    br, bi = p["ssm_b_re"][l], p["ssm_b_im"][l]
    bb_re = f_re[..., None] * br - f_im[..., None] * bi
    bb_im = f_re[..., None] * bi + f_im[..., None] * br
    gpc = LANE // SSM_GROUP_DIM
    nch = SSM_GROUPS // gpc
    eye = jnp.eye(gpc, dtype=F32)

    def bmat(bb):
        return jnp.einsum("cgnp,gh->cgphn", bb.reshape(nch, gpc, SSM_STATE, SSM_GROUP_DIM),
                          eye).reshape(nch, LANE, gpc * SSM_STATE).astype(BF16)

    def cmat(cc):
        return jnp.einsum("cgpn,gh->cgnhp", cc.reshape(nch, gpc, SSM_GROUP_DIM, SSM_STATE),
                          eye).reshape(nch, gpc * SSM_STATE, LANE).astype(BF16)

    w["ssm_bre"], w["ssm_bim"] = bmat(bb_re), bmat(bb_im)
    w["ssm_cre"], w["ssm_cim"] = cmat(p["ssm_c_re"][l]), cmat(-p["ssm_c_im"][l])
    nrow = 8
    w["ssm_lre"] = jnp.broadcast_to(lb_re.reshape(1, -1), (nrow, SSM_GROUPS * SSM_STATE))
    w["ssm_lim"] = jnp.broadcast_to(lb_im.reshape(1, -1), (nrow, SSM_GROUPS * SSM_STATE))
    w["ssm_d"] = p["ssm_d"][l].reshape(1, D_SSM)
    w["ssm_wglu"] = p["ssm_w_glu"][l].astype(BF16)

    w["w_ups"] = p["w_up_ssm"][l].astype(BF16)
    w["w_upa"] = p["w_up_attn"][l].astype(BF16)
    w["w_o"] = p["w_o"][l].astype(BF16)
    w["lng0"], w["lnb0"] = p["ln_g"][l, 0].reshape(1, D), p["ln_b"][l, 0].reshape(1, D)
    w["lng1"], w["lnb1"] = p["ln_g"][l, 1].reshape(1, D), p["ln_b"][l, 1].reshape(1, D)
    rwg = p["router_w"].reshape(D, N_EXPERT_GROUPS, EXPERTS_PER_GROUP)
    rbg = p["router_bias"].reshape(N_EXPERT_GROUPS, EXPERTS_PER_GROUP)
    w["rw"] = jnp.transpose(rwg, (2, 1, 0)).reshape(N_EXPERTS, D)
    w["rb"] = jnp.transpose(rbg, (1, 0)).reshape(N_EXPERTS, 1)
    w["ws1"] = p["shared_w1"][l].astype(BF16)
    w["ws3"] = p["shared_w3"][l].astype(BF16)
    w["ws2"] = p["shared_w2"][l].astype(BF16)
    return w


def kernel(x, c, positions, w_ada, b_ada, w_in, conv_w, ssm_a_re, ssm_a_im, ssm_b_re, ssm_b_im,
           ssm_c_re, ssm_c_im, ssm_d, ssm_log_dt, ssm_w_glu, q_norm, w_uq, kv_norm, w_uk, w_uv,
           w_up_conv, w_up_ssm, w_up_attn, w_o, ln_g, ln_b, router_w, router_bias,
           exp_w1, exp_w3, exp_w2, shared_w1, shared_w3, shared_w2):
    B, S, D = x.shape
    L = w_in.shape[0]
    T = B * S
    assert B == 8 and S % max(TM_PROJ, T_CHUNK, TQ, FINAL_ROWS) == 0 and T % max(MOE_ROWS, PERM_ROWS) == 0
    assert TM_PROJ == TQ
    p = dict(w_in=w_in, conv_w=conv_w, ssm_a_re=ssm_a_re, ssm_a_im=ssm_a_im, ssm_b_re=ssm_b_re,
             ssm_b_im=ssm_b_im, ssm_c_re=ssm_c_re, ssm_c_im=ssm_c_im, ssm_d=ssm_d,
             ssm_log_dt=ssm_log_dt, ssm_w_glu=ssm_w_glu, q_norm=q_norm, w_uq=w_uq, kv_norm=kv_norm,
             w_uk=w_uk, w_uv=w_uv, w_up_conv=w_up_conv, w_up_ssm=w_up_ssm, w_up_attn=w_up_attn,
             w_o=w_o, ln_g=ln_g, ln_b=ln_b, router_w=router_w, router_bias=router_bias,
             exp_w1=exp_w1, exp_w3=exp_w3, exp_w2=exp_w2, shared_w1=shared_w1,
             shared_w3=shared_w3, shared_w2=shared_w2)
    alpha = (2 * L) ** 0.25
    cos128, sin128 = _rope_tables(positions)
    ada = _ada(c, w_ada, b_ada).reshape(L, B, 6, D)
    x2d = x.reshape(T, D)
    for l in range(L):
        wts = _layer_weights(l, p)
        pc, sgs, sga, u, q, k, v = _inproj(x2d, ada[l], cos128, sin128, wts, B, S)
        ys = _ssm(u.reshape(B, S, D_SSM), wts, B, S).reshape(T, D_SSM)
        ya = _attention(q, k, v, B, S)
        base, hr, rt = _merge(x2d, ada[l], pc, sgs, sga, ys, ya, wts, B, S, alpha)
        gid = rt[EXPERTS_PER_GROUP].astype(jnp.int32)
        dest, block_g, nvalid = _route_plan(gid, T)
        hs = _permute(hr, dest, nvalid)
        ysort = _moe(hs, block_g, nvalid, exp_w1, exp_w3, exp_w2, l)
        x2d = _final(base, ysort, dest, ada[l], wts["lng1"], wts["lnb1"], B, S)
    return x2d.reshape(B, S, D)
```

```python
import functools
import math

import jax
import jax.numpy as jnp
from jax import lax
from jax.experimental import pallas as pl
from jax.experimental.pallas import tpu as pltpu

F32 = jnp.float32
BF16 = jnp.bfloat16

D_CONV = 512
D_SSM = 512
SSM_GROUP_DIM = 16
SSM_GROUPS = 32
SSM_STATE = 64
MLA_HEADS = 8
QK_NOPE = 64
QK_ROPE = 32
V_HEAD = 64
Q_LORA = 256
KV_LORA = 128
ROPE_THETA = 10000.0
N_EXPERTS = 32
N_EXPERT_GROUPS = 8
EXPERTS_PER_GROUP = 4
D_EXPERT = 256
LN_EPS = 1e-5
RMS_EPS = 1e-6

LANE = 128
HEAD_PAD = 128
VMEM_LIMIT = 56 * 1024 * 1024

TM_PROJ = 512
T_CHUNK = 64
TQ = 512
ATTN_HEADS = 8
DEN_ROWS = 16
MOE_ROWS = 512
PERM_ROWS = 2048
FINAL_ROWS = 1024
ROUTE_W = 128


def _cparams(n_axes):
    return pltpu.CompilerParams(dimension_semantics=("arbitrary",) * n_axes,
                                vmem_limit_bytes=VMEM_LIMIT)


def _ln_plain(x):
    mu = jnp.mean(x, axis=-1, keepdims=True)
    xc = x - mu
    var = jnp.mean(xc * xc, axis=-1, keepdims=True)
    return xc * lax.rsqrt(var + LN_EPS)


def _rms(x, g):
    return x * lax.rsqrt(jnp.mean(x * x, axis=-1, keepdims=True) + RMS_EPS) * g


def _bdot(a, b):
    return jnp.dot(a.astype(BF16), b, preferred_element_type=F32)


def _split_bf16(x):
    hi = pltpu.bitcast(pltpu.bitcast(x, jnp.uint32) & jnp.uint32(0xFFFF0000), F32)
    return hi.astype(BF16), (x - hi).astype(BF16)


def _ada_kernel(c_ref, w_ref, b_ref, o_ref):
    c = c_ref[...]
    cond = c * jax.nn.sigmoid(c)
    c_hi, c_lo = _split_bf16(cond)
    w_hi, w_lo = _split_bf16(w_ref[...])
    o_ref[...] = (jnp.dot(c_hi, w_hi, preferred_element_type=F32)
                  + jnp.dot(c_lo, w_hi, preferred_element_type=F32)
                  + jnp.dot(c_hi, w_lo, preferred_element_type=F32)
                  + jnp.dot(c_lo, w_lo, preferred_element_type=F32)) + b_ref[...]


def _ada(c, w_ada, b_ada):
    L, D, D6 = w_ada.shape
    B = c.shape[0]
    wn = D6 // 2
    nj = D6 // wn
    return pl.pallas_call(
        _ada_kernel,
        out_shape=jax.ShapeDtypeStruct((L, B, D6), F32),
        grid=(L, nj),
        in_specs=[pl.BlockSpec((B, D), lambda l, j: (0, 0)),
                  pl.BlockSpec((None, D, wn), lambda l, j: (l, 0, j)),
                  pl.BlockSpec((None, 1, wn), lambda l, j: (l, 0, j))],
        out_specs=pl.BlockSpec((None, B, wn), lambda l, j: (l, 0, j)),
        compiler_params=_cparams(2),
        name="ada",
    )(c, w_ada, b_ada.reshape(L, 1, D6))


def _inproj_kernel(x_ref, ada_ref, cos_ref, sin_ref,
                   w_cv, w_ssm, w_lat, w_g,
                   convw, qn, wq_a, wq_b, kvn, wk, wv, w_upc,
                   pc_ref, sgs_ref, sga_ref, u_ref, q_ref, k_ref, v_ref,
                   carry_ref):
    si = pl.program_id(1)
    D = x_ref.shape[1]
    tm = x_ref.shape[0]

    @pl.when(si == 0)
    def _():
        carry_ref[...] = jnp.zeros_like(carry_ref)

    ada = ada_ref[...]
    h = _ln_plain(x_ref[...]) * (1.0 + ada[1:2, :]) + ada[0:1, :]
    hb = h.astype(BF16)

    pcv = jnp.dot(hb, w_cv[...], preferred_element_type=F32)
    u = pcv[:, D_CONV:2 * D_CONV] * pcv[:, :D_CONV]
    gb = pcv[:, 2 * D_CONV:]
    tail = carry_ref[...]
    row8 = lax.broadcasted_iota(jnp.int32, (8, D_CONV), 0)
    r1 = pltpu.roll(u, 1, 0)
    r2 = pltpu.roll(u, 2, 0)
    t1 = pltpu.roll(tail, 1, 0)
    t2 = pltpu.roll(tail, 2, 0)
    u1 = jnp.concatenate([jnp.where(row8 < 1, t1, r1[:8]), r1[8:]], axis=0)
    u2 = jnp.concatenate([jnp.where(row8 < 2, t2, r2[:8]), r2[8:]], axis=0)
    cw = convw[...]
    yc = gb * (cw[0:1, :] * u2 + cw[1:2, :] * u1 + cw[2:3, :] * u)
    carry_ref[...] = u[tm - 8:, :]

    gates = jnp.dot(hb, w_g[...], preferred_element_type=F32)
    pc_ref[...] = (jax.nn.sigmoid(gates[:, :D]) * _bdot(yc, w_upc[...])).astype(BF16)
    sgs_ref[...] = jax.nn.sigmoid(gates[:, D:2 * D]).astype(BF16)
    sga_ref[...] = jax.nn.sigmoid(gates[:, 2 * D:]).astype(BF16)

    u_ref[...] = jnp.dot(hb, w_ssm[...], preferred_element_type=F32).astype(BF16)

    cos = cos_ref[...]
    sin = sin_ref[...]
    lat = jnp.dot(hb, w_lat[...], preferred_element_type=F32)
    o_kv, o_kr = Q_LORA, Q_LORA + KV_LORA
    cq = _rms(lat[:, :o_kv], qn[...]).astype(BF16)
    qa = jnp.dot(cq, wq_a[...], preferred_element_type=F32)
    qb = jnp.dot(cq, wq_b[...], preferred_element_type=F32)
    ckv = _rms(lat[:, o_kv:o_kr], kvn[...]).astype(BF16)
    kn = jnp.dot(ckv, wk[...], preferred_element_type=F32)
    kr = lat[:, o_kr:o_kr + HEAD_PAD] * cos + lat[:, o_kr + HEAD_PAD:] * sin
    for hd in range(MLA_HEADS):
        sl = slice(hd * HEAD_PAD, (hd + 1) * HEAD_PAD)
        q_ref[:, sl] = (qa[:, sl] * cos + qb[:, sl] * sin).astype(BF16)
        k_ref[:, sl] = (kn[:, sl] + kr).astype(BF16)
    v_ref[...] = lax.dot_general(wv[...], ckv, (((1,), (1,)), ((), ())),
                                 preferred_element_type=F32).astype(BF16)


def _inproj(x2d, ada_l, cos128, sin128, wts, B, S):
    T, D = x2d.shape
    tm = TM_PROJ
    ns = S // tm
    row = lambda b, s: (b * ns + s, 0)
    full = lambda a: pl.BlockSpec(a.shape, lambda b, s: (0,) * a.ndim, pipeline_mode=pl.Buffered(1))
    names = ["w_cv", "w_ssm", "w_lat", "w_g", "convw", "qn",
             "wq_a", "wq_b", "kvn", "wk", "wv", "w_upc"]
    ws = [wts[n] for n in names]
    out_shape = (
        jax.ShapeDtypeStruct((T, D), BF16),
        jax.ShapeDtypeStruct((T, D), BF16),
        jax.ShapeDtypeStruct((T, D), BF16),
        jax.ShapeDtypeStruct((T, D_SSM), BF16),
        jax.ShapeDtypeStruct((T, MLA_HEADS * HEAD_PAD), BF16),
        jax.ShapeDtypeStruct((T, MLA_HEADS * HEAD_PAD), BF16),
        jax.ShapeDtypeStruct((B, ns, MLA_HEADS * V_HEAD, tm), BF16),
    )
    out_specs = (
        pl.BlockSpec((tm, D), row), pl.BlockSpec((tm, D), row), pl.BlockSpec((tm, D), row),
        pl.BlockSpec((tm, D_SSM), row),
        pl.BlockSpec((tm, MLA_HEADS * HEAD_PAD), row),
        pl.BlockSpec((tm, MLA_HEADS * HEAD_PAD), row),
        pl.BlockSpec((None, None, MLA_HEADS * V_HEAD, tm), lambda b, s: (b, s, 0, 0)),
    )
    return pl.pallas_call(
        _inproj_kernel,
        out_shape=out_shape,
        grid=(B, ns),
        in_specs=[pl.BlockSpec((tm, D), row),
                  pl.BlockSpec((None, 6, D), lambda b, s: (b, 0, 0)),
                  pl.BlockSpec((tm, LANE), row), pl.BlockSpec((tm, LANE), row)]
                 + [full(a) for a in ws],
        out_specs=out_specs,
        scratch_shapes=[pltpu.VMEM((8, D_CONV), F32)],
        compiler_params=_cparams(2),
        name="inproj",
    )(x2d, ada_l, cos128, sin128, *ws)


def _ssm_kernel(u_ref, bre_ref, bim_ref, lre_ref, lim_ref, cre_ref, cim_ref, d_ref, wglu_ref, perm_ref,
                o_ref, are, aim, au, bre_s, bim_s, bu, st_re, st_im, tmaj):
    i = pl.program_id(0)
    batch, nsteps = u_ref.shape[0], u_ref.shape[1]
    rows = batch * nsteps
    nchunk = bre_ref.shape[0]
    cw = bre_ref.shape[2]

    @pl.when(i == 0)
    def _():
        st_re[...] = jnp.zeros_like(st_re)
        st_im[...] = jnp.zeros_like(st_im)
        bre_s[...] = jnp.zeros_like(bre_s)
        bim_s[...] = jnp.zeros_like(bim_s)
        bu[...] = jnp.zeros_like(bu)

    def body(cur_re, cur_im, cur_u, prev_re, prev_im, prev_u):
        for b in range(batch):
            tmaj[:, b, :] = u_ref[b].astype(F32)
        uf = tmaj[...].reshape(rows, D_SSM)
        cur_u[...] = uf
        ub = uf.astype(BF16)
        for c in range(nchunk):
            uc = ub[:, c * LANE:(c + 1) * LANE]
            cur_re[:, c * cw:(c + 1) * cw] = jnp.dot(uc, bre_ref[c], preferred_element_type=F32)
            cur_im[:, c * cw:(c + 1) * cw] = jnp.dot(uc, bim_ref[c], preferred_element_type=F32)

        ys = []

        def out_chunk(c):
            ys.append(_bdot(prev_re[:, c * cw:(c + 1) * cw], cre_ref[c])
                      + _bdot(prev_im[:, c * cw:(c + 1) * cw], cim_ref[c]))

        def out_finish():
            y = jnp.concatenate(ys, axis=1) + d_ref[...] * prev_u[...]
            y = 0.5 * y * (1.0 + jnp.tanh(math.sqrt(2.0 / math.pi) * (y + 0.044715 * (y * y * y))))
            z = _bdot(y, wglu_ref[...])
            g = (z[:, :D_SSM] * jax.nn.sigmoid(z[:, D_SSM:])).astype(BF16)
            o_ref[...] = jnp.dot(perm_ref[...], g, preferred_element_type=F32).astype(BF16).reshape(
                batch, nsteps, D_SSM)

        pieces = [functools.partial(out_chunk, c) for c in range(nchunk)] + [out_finish]
        every = nsteps // len(pieces)
        pr, pi = st_re[...], st_im[...]
        for t in range(nsteps):
            r = t * batch
            lre = lre_ref[...]
            lim = lim_ref[...]
            nr = lre * pr - lim * pi + cur_re[r:r + batch, :]
            ni = lre * pi + lim * pr + cur_im[r:r + batch, :]
            cur_re[r:r + batch, :] = nr
            cur_im[r:r + batch, :] = ni
            pr, pi = nr, ni
            if t % every == every - 1 and pieces:
                pieces.pop(0)()
        while pieces:
            pieces.pop(0)()
        st_re[...] = pr
        st_im[...] = pi

    @pl.when(i % 2 == 0)
    def _():
        body(are, aim, au, bre_s, bim_s, bu)

    @pl.when(i % 2 == 1)
    def _():
        body(bre_s, bim_s, bu, are, aim, au)


def _ssm(u, wts, B, S):
    rows = T_CHUNK * B
    n = S // T_CHUNK
    nstate = SSM_GROUPS * SSM_STATE
    names = ["ssm_bre", "ssm_bim", "ssm_lre", "ssm_lim", "ssm_cre", "ssm_cim", "ssm_d", "ssm_wglu"]
    r = jnp.arange(rows, dtype=jnp.int32)
    src = (r % T_CHUNK) * B + r // T_CHUNK
    perm = (src[:, None] == jnp.arange(rows, dtype=jnp.int32)[None, :]).astype(BF16)
    ws = [wts[k] for k in names] + [perm]
    full = lambda a: pl.BlockSpec(a.shape, lambda i: (0,) * a.ndim, pipeline_mode=pl.Buffered(1))
    return pl.pallas_call(
        _ssm_kernel,
        out_shape=jax.ShapeDtypeStruct((B, S, D_SSM), BF16),
        grid=(n + 1,),
        in_specs=[pl.BlockSpec((B, T_CHUNK, D_SSM), lambda i: (0, jnp.minimum(i, n - 1), 0))]
                 + [full(a) for a in ws],
        out_specs=pl.BlockSpec((B, T_CHUNK, D_SSM), lambda i: (0, jnp.maximum(i - 1, 0), 0)),
        scratch_shapes=[pltpu.VMEM((rows, nstate), F32), pltpu.VMEM((rows, nstate), F32),
                        pltpu.VMEM((rows, D_SSM), F32),
                        pltpu.VMEM((rows, nstate), F32), pltpu.VMEM((rows, nstate), F32),
                        pltpu.VMEM((rows, D_SSM), F32),
                        pltpu.VMEM((B, nstate), F32), pltpu.VMEM((B, nstate), F32),
                        pltpu.VMEM((T_CHUNK, B, D_SSM), F32)],
        compiler_params=_cparams(1),
        name="ssm",
    )(u, *ws)


def _attn_kernel(q_ref, k_ref, vt_ref, o_ref, m_sc, acc_sc, sa_sc, sb_sc, p_sc):
    qi = pl.program_id(2)
    tq = q_ref.shape[0]
    nh = m_sc.shape[0]
    neg = -1e30
    m_sc[...] = jnp.full(m_sc.shape, neg, F32)
    acc_sc[...] = jnp.zeros(acc_sc.shape, F32)
    ones = jnp.ones((DEN_ROWS, tq), BF16)

    def scores(j, s_sc):
        off = pl.multiple_of(j * tq, tq)
        for hh in range(nh):
            q = q_ref[:, hh * HEAD_PAD:(hh + 1) * HEAD_PAD]
            kb = k_ref[pl.ds(off, tq), hh * HEAD_PAD:(hh + 1) * HEAD_PAD]
            s_sc[hh] = lax.dot_general(kb, q, (((1,), (1,)), ((), ())),
                                       preferred_element_type=F32)

    def update(j, s_sc, masked):
        scale = []
        for hh in range(nh):
            if masked:
                keyi = lax.broadcasted_iota(jnp.int32, (tq, tq), 0)
                qryi = lax.broadcasted_iota(jnp.int32, (tq, tq), 1)
                s_sc[hh] = jnp.where(keyi <= qryi, s_sc[hh], neg)
            m_prev = m_sc[hh]
            m_new = jnp.maximum(m_prev, jnp.max(s_sc[hh], axis=0, keepdims=True))
            m_sc[hh] = m_new
            p_sc[hh] = jnp.exp2(s_sc[hh] - m_new).astype(BF16)
            scale.append(jnp.exp2(m_prev - m_new))
        for hh in range(nh):
            vte = jnp.concatenate([vt_ref[j, hh * V_HEAD:(hh + 1) * V_HEAD, :], ones], axis=0)
            acc_sc[hh] = scale[hh] * acc_sc[hh] + jnp.dot(vte, p_sc[hh], preferred_element_type=F32)

    scores(0, sa_sc)

    def pair(pp, c):
        j = 2 * pp
        scores(j + 1, sb_sc)
        update(j, sa_sc, False)
        scores(j + 2, sa_sc)
        update(j + 1, sb_sc, False)
        return c

    npair = qi // 2
    lax.fori_loop(0, npair, pair, 0)
    j0 = 2 * npair

    @pl.when(qi % 2 == 0)
    def _():
        update(j0, sa_sc, True)

    @pl.when(qi % 2 == 1)
    def _():
        scores(j0 + 1, sb_sc)
        update(j0, sa_sc, False)
        update(j0 + 1, sb_sc, True)

    outs = []
    for pr in range(nh // 2):
        halves = []
        for hh in (2 * pr, 2 * pr + 1):
            acc = acc_sc[hh]
            halves.append(acc[:V_HEAD] * (1.0 / acc[V_HEAD:V_HEAD + 1]))
        outs.append(jnp.concatenate(halves, axis=0).T)
    o_ref[...] = jnp.concatenate(outs, axis=1).astype(BF16)


def _attention(q, k, vt, B, S):
    nq = S // TQ
    nh = ATTN_HEADS
    hg = MLA_HEADS // nh
    return pl.pallas_call(
        _attn_kernel,
        out_shape=jax.ShapeDtypeStruct((B * S, MLA_HEADS * V_HEAD), BF16),
        grid=(B, hg, nq),
        in_specs=[pl.BlockSpec((TQ, nh * HEAD_PAD), lambda b, h, i: (b * nq + i, h)),
                  pl.BlockSpec((S, nh * HEAD_PAD), lambda b, h, i: (b, h)),
                  pl.BlockSpec((None, nq, nh * V_HEAD, TQ), lambda b, h, i: (b, 0, h, 0))],
        out_specs=pl.BlockSpec((TQ, nh * V_HEAD), lambda b, h, i: (b * nq + i, h)),
        scratch_shapes=[pltpu.VMEM((nh, 1, TQ), F32),
                        pltpu.VMEM((nh, V_HEAD + DEN_ROWS, TQ), F32),
                        pltpu.VMEM((nh, TQ, TQ), F32), pltpu.VMEM((nh, TQ, TQ), F32),
                        pltpu.VMEM((nh, TQ, TQ), BF16)],
        compiler_params=_cparams(3),
        name="attention",
    )(q, k, vt)


def _merge_kernel(x_ref, ada_ref, pc_ref, sgs_ref, sga_ref, ys_ref, ya_ref,
                  w_ups, w_upa, w_o, lng, lnb, rw, rb, ws1, ws3, ws2,
                  base_ref, hr_ref, rt_ref, *, alpha):
    D = x_ref.shape[1]
    tm = x_ref.shape[0]
    ada = ada_ref[...]
    merged = (pc_ref[...].astype(F32)
              + sgs_ref[...].astype(F32) * jnp.dot(ys_ref[...], w_ups[...], preferred_element_type=F32)
              + sga_ref[...].astype(F32) * jnp.dot(ya_ref[...], w_upa[...], preferred_element_type=F32))
    y = _bdot(merged, w_o[...])
    x1 = _ln_plain(alpha * x_ref[...] + ada[2:3, :] * y) * lng[...] + lnb[...]
    h2 = _ln_plain(x1) * (1.0 + ada[4:5, :]) + ada[3:4, :]
    hr_ref[:, :D] = h2

    G = N_EXPERT_GROUPS
    nt = (((1,), (1,)), ((), ()))
    h2_hi, h2_lo = _split_bf16(h2)
    rw_hi, rw_lo = rw[0], rw[1]
    logits = (lax.dot_general(rw_hi, h2_hi, nt, preferred_element_type=F32)
              + lax.dot_general(rw_lo, h2_hi, nt, preferred_element_type=F32)
              + lax.dot_general(rw_hi, h2_lo, nt, preferred_element_type=F32)
              + lax.dot_general(rw_lo, h2_lo, nt, preferred_element_type=F32))
    sc0 = jax.nn.sigmoid(logits)
    sl0 = sc0 + rb[...]
    sc = [sc0[j * G:(j + 1) * G] for j in range(EXPERTS_PER_GROUP)]
    sl = [sl0[j * G:(j + 1) * G] for j in range(EXPERTS_PER_GROUP)]
    hi1, lo1 = jnp.maximum(sl[0], sl[1]), jnp.minimum(sl[0], sl[1])
    hi2, lo2 = jnp.maximum(sl[2], sl[3]), jnp.minimum(sl[2], sl[3])
    gscore = jnp.maximum(hi1, hi2) + jnp.maximum(jnp.minimum(hi1, hi2), jnp.maximum(lo1, lo2))
    sub = lax.broadcasted_iota(jnp.int32, (G, tm), 0)
    gmax = jnp.max(gscore, axis=0, keepdims=True)
    gidx = jnp.min(jnp.where(gscore == gmax, sub, G), axis=0, keepdims=True)
    pick = sub == gidx
    vs = [jnp.sum(jnp.where(pick, a, 0.0), axis=0, keepdims=True) for a in sl]
    ss = [jnp.sum(jnp.where(pick, a, 0.0), axis=0, keepdims=True) for a in sc]
    chosen = []
    for j in range(EXPERTS_PER_GROUP):
        cnt = jnp.zeros((1, tm), jnp.int32)
        for k in range(EXPERTS_PER_GROUP):
            if k == j:
                continue
            beats = (vs[k] > vs[j]) | ((vs[k] == vs[j]) & (k < j))
            cnt = cnt + beats.astype(jnp.int32)
        chosen.append(cnt < 2)
    wsel = [jnp.where(chosen[j], ss[j], 0.0) for j in range(EXPERTS_PER_GROUP)]
    wsum = wsel[0] + wsel[1] + wsel[2] + wsel[3]
    rows = [wsel[j] / wsum for j in range(EXPERTS_PER_GROUP)] + [gidx.astype(F32)]
    rows += [jnp.zeros((1, tm), F32)] * (8 - len(rows))
    route_t = jnp.concatenate(rows, axis=0)
    rt_ref[...] = route_t
    eye = (lax.broadcasted_iota(jnp.int32, (8, ROUTE_W), 0)
           == lax.broadcasted_iota(jnp.int32, (8, ROUTE_W), 1)).astype(F32)
    hr_ref[:, D:] = lax.dot_general(route_t, eye, (((0,), (0,)), ((), ())),
                                    precision=lax.Precision.HIGHEST, preferred_element_type=F32)

    h2b = h2.astype(BF16)
    a1 = jnp.dot(h2b, ws1[...], preferred_element_type=F32)
    a3 = jnp.dot(h2b, ws3[...], preferred_element_type=F32)
    ysh = _bdot(a1 * jax.nn.sigmoid(a1) * a3, ws2[...])
    base_ref[...] = alpha * x1 + ada[5:6, :] * ysh


def _merge(x2d, ada_l, pc, sgs, sga, ys, ya, wts, B, S, alpha):
    T, D = x2d.shape
    tm = TM_PROJ
    ns = S // tm
    row = lambda b, s: (b * ns + s, 0)
    names = ["w_ups", "w_upa", "w_o", "lng0", "lnb0", "rw", "rb", "ws1", "ws3", "ws2"]
    ws = [wts[n] for n in names]
    full = lambda a: pl.BlockSpec(a.shape, lambda b, s: (0,) * a.ndim, pipeline_mode=pl.Buffered(1))
    return pl.pallas_call(
        functools.partial(_merge_kernel, alpha=alpha),
        out_shape=(jax.ShapeDtypeStruct((T, D), F32),
                   jax.ShapeDtypeStruct((T, D + ROUTE_W), F32),
                   jax.ShapeDtypeStruct((8, T), F32)),
        grid=(B, ns),
        in_specs=[pl.BlockSpec((tm, D), row),
                  pl.BlockSpec((None, 6, D), lambda b, s: (b, 0, 0)),
                  pl.BlockSpec((tm, D), row), pl.BlockSpec((tm, D), row), pl.BlockSpec((tm, D), row),
                  pl.BlockSpec((tm, D_SSM), row),
                  pl.BlockSpec((tm, MLA_HEADS * V_HEAD), row)]
                 + [full(a) for a in ws],
        out_specs=(pl.BlockSpec((tm, D), row), pl.BlockSpec((tm, D + ROUTE_W), row),
                   pl.BlockSpec((8, tm), lambda b, s: (0, b * ns + s))),
        compiler_params=_cparams(2),
        name="merge",
    )(x2d, ada_l, pc, sgs, sga, ys, ya, *ws)


def _route_plan(gid, T):
    R = MOE_ROWS
    G = N_EXPERT_GROUPS
    onehot = (gid[:, None] == jnp.arange(G, dtype=jnp.int32)[None, :]).astype(jnp.int32)
    csum = jnp.cumsum(onehot, axis=0)
    counts = csum[-1]
    rank = jnp.sum(onehot * csum, axis=1) - 1
    padded = (counts + R - 1) // R * R
    pad_end = jnp.cumsum(padded)
    pad_start = pad_end - padded
    dest = jnp.sum(onehot * pad_start[None, :], axis=1) + rank
    nb = T // R + G
    starts = jnp.arange(nb, dtype=jnp.int32) * R
    block_g = jnp.minimum(jnp.sum((starts[:, None] >= pad_end[None, :]).astype(jnp.int32), axis=1), G - 1)
    gsel = (block_g[:, None] == jnp.arange(G, dtype=jnp.int32)[None, :]).astype(jnp.int32)
    group_end = jnp.sum(gsel * (pad_start + counts)[None, :], axis=1)
    nvalid = jnp.clip(group_end - starts, 0, R)
    return dest.astype(jnp.int32), block_g.astype(jnp.int32), nvalid.astype(jnp.int32)


def _permute_kernel(dest_ref, nvalid_ref, hr_ref, hs_hbm, zbuf, sem, zsem):
    i = pl.program_id(0)
    tp = hr_ref.shape[0]
    R = zbuf.shape[0]
    nb = nvalid_ref.shape[0]

    @pl.when(i == 0)
    def _():
        zbuf[...] = jnp.zeros(zbuf.shape, F32)

        def fill(b, n):
            partial = nvalid_ref[b] < R

            @pl.when(partial)
            def _():
                pltpu.make_async_copy(zbuf, hs_hbm.at[pl.ds(b * R, R)], zsem).start()

            return n + partial.astype(jnp.int32)

        nfill = lax.fori_loop(0, nb, fill, 0)

        def drain(k, c):
            pltpu.make_async_copy(zbuf, hs_hbm.at[pl.ds(0, R)], zsem).wait()
            return c

        lax.fori_loop(0, nfill, drain, 0)

    def body(g, c):
        r0 = pl.multiple_of(g * 8, 8)
        for k in range(8):
            d = dest_ref[i * tp + r0 + k]
            pltpu.make_async_copy(hr_ref.at[pl.ds(r0 + k, 1)], hs_hbm.at[pl.ds(d, 1)], sem).start()
        return c

    lax.fori_loop(0, tp // 8, body, 0)
    pltpu.make_async_copy(hr_ref, hs_hbm.at[pl.ds(0, tp)], sem).wait()


def _permute(hr, dest, nvalid):
    T, W = hr.shape
    R = MOE_ROWS
    tp = PERM_ROWS
    nb = nvalid.shape[0]
    return pl.pallas_call(
        _permute_kernel,
        out_shape=jax.ShapeDtypeStruct((nb * R, W), F32),
        grid_spec=pltpu.PrefetchScalarGridSpec(
            num_scalar_prefetch=2,
            grid=(T // tp,),
            in_specs=[pl.BlockSpec((tp, W), lambda i, d, n: (i, 0))],
            out_specs=pl.BlockSpec(memory_space=pl.ANY),
            scratch_shapes=[pltpu.VMEM((R, W), F32), pltpu.SemaphoreType.DMA, pltpu.SemaphoreType.DMA]),
        compiler_params=_cparams(1),
        name="permute",
    )(dest, nvalid, hr)


def _moe_kernel(bg_ref, nvalid_ref, hs_ref, w1_ref, w3_ref, w2_ref, y_ref, w1b, w3b, w2b):
    i = pl.program_id(0)
    R, D = y_ref.shape

    @pl.when((i == 0) | (bg_ref[i] != bg_ref[jnp.maximum(i - 1, 0)]))
    def _():
        w1b[...] = w1_ref[...].astype(BF16)
        w3b[...] = w3_ref[...].astype(BF16)
        w2b[...] = w2_ref[...].astype(BF16)

    @pl.when(nvalid_ref[i] > 0)
    def _():
        xb = hs_ref[:, :D].astype(BF16)
        acc = jnp.zeros((R, D), F32)
        for j in range(EXPERTS_PER_GROUP):
            a1 = jnp.dot(xb, w1b[j], preferred_element_type=F32)
            a3 = jnp.dot(xb, w3b[j], preferred_element_type=F32)
            hj = a1 * jax.nn.sigmoid(a1) * a3 * hs_ref[:, D + j:D + j + 1]
            acc = acc + _bdot(hj, w2b[j])
        y_ref[...] = acc

    @pl.when(nvalid_ref[i] == 0)
    def _():
        y_ref[...] = jnp.zeros((R, D), F32)


def _moe(hs, block_g, nvalid, w1, w3, w2, layer):
    D = w1.shape[2]
    W = hs.shape[1]
    R = MOE_ROWS
    nb = nvalid.shape[0]
    wspec_in = pl.BlockSpec((None, EXPERTS_PER_GROUP, D, D_EXPERT), lambda i, bg, nv: (layer, bg[i], 0, 0))
    wspec_out = pl.BlockSpec((None, EXPERTS_PER_GROUP, D_EXPERT, D), lambda i, bg, nv: (layer, bg[i], 0, 0))
    return pl.pallas_call(
        _moe_kernel,
        out_shape=jax.ShapeDtypeStruct((nb * R, D), F32),
        grid_spec=pltpu.PrefetchScalarGridSpec(
            num_scalar_prefetch=2,
            grid=(nb,),
            in_specs=[pl.BlockSpec((R, W), lambda i, bg, nv: (i, 0)), wspec_in, wspec_in, wspec_out],
            out_specs=pl.BlockSpec((R, D), lambda i, bg, nv: (i, 0)),
            scratch_shapes=[pltpu.VMEM((EXPERTS_PER_GROUP, D, D_EXPERT), BF16),
                            pltpu.VMEM((EXPERTS_PER_GROUP, D, D_EXPERT), BF16),
                            pltpu.VMEM((EXPERTS_PER_GROUP, D_EXPERT, D), BF16)]),
        compiler_params=_cparams(1),
        name="moe",
    )(block_g, nvalid, hs, w1, w3, w2)


def _final_kernel(dest_ref, base_ref, ys_hbm, ada_ref, lng, lnb, o_ref, ybuf, sem):
    i = pl.program_id(0)
    n = pl.num_programs(0)
    tf = base_ref.shape[0]
    slot = i % 2

    def start_gather(step, s):
        def body(g, c):
            r0 = pl.multiple_of(g * 8, 8)
            for k in range(8):
                d = dest_ref[step * tf + r0 + k]
                pltpu.make_async_copy(ys_hbm.at[pl.ds(d, 1)], ybuf.at[s, pl.ds(r0 + k, 1)],
                                      sem.at[s]).start()
            return c

        lax.fori_loop(0, tf // 8, body, 0)

    @pl.when(i == 0)
    def _():
        start_gather(0, 0)

    @pl.when(i + 1 < n)
    def _():
        start_gather(i + 1, 1 - slot)

    pltpu.make_async_copy(ys_hbm.at[pl.ds(0, tf)], ybuf.at[slot], sem.at[slot]).wait()
    ada = ada_ref[...]
    o_ref[...] = _ln_plain(base_ref[...] + ada[5:6, :] * ybuf[slot]) * lng[...] + lnb[...]


def _final(base, ys, dest, ada_l, lng, lnb, B, S):
    T, D = base.shape
    tf = FINAL_ROWS
    per_batch = S // tf
    return pl.pallas_call(
        _final_kernel,
        out_shape=jax.ShapeDtypeStruct((T, D), F32),
        grid_spec=pltpu.PrefetchScalarGridSpec(
            num_scalar_prefetch=1,
            grid=(T // tf,),
            in_specs=[pl.BlockSpec((tf, D), lambda i, d: (i, 0)),
                      pl.BlockSpec(memory_space=pl.ANY),
                      pl.BlockSpec((None, 6, D), lambda i, d: (i // per_batch, 0, 0)),
                      pl.BlockSpec((1, D), lambda i, d: (0, 0)), pl.BlockSpec((1, D), lambda i, d: (0, 0))],
            out_specs=pl.BlockSpec((tf, D), lambda i, d: (i, 0)),
            scratch_shapes=[pltpu.VMEM((2, tf, D), F32), pltpu.SemaphoreType.DMA((2,))]),
        compiler_params=_cparams(1),
        name="final_ln",
    )(dest, base, ys, ada_l, lng, lnb)


def _rope_tables(positions):
    inv_freq = ROPE_THETA ** (-jnp.arange(0, QK_ROPE, 2, dtype=F32) / QK_ROPE)
    ang = positions.astype(F32)[..., None] * inv_freq
    dense = lax.optimization_barrier(ang.reshape(-1, LANE))
    cos, sin = lax.optimization_barrier((jnp.cos(dense), jnp.sin(dense)))
    cos, sin = cos.reshape(ang.shape), sin.reshape(ang.shape)
    ones = jnp.ones(cos.shape[:-1] + (QK_NOPE,), F32)
    zpad = jnp.zeros(cos.shape[:-1] + (HEAD_PAD - QK_NOPE - QK_ROPE,), F32)
    cos128 = jnp.concatenate([ones, cos, cos, zpad], axis=-1)
    sin128 = jnp.concatenate([0.0 * ones, -sin, sin, zpad], axis=-1)
    T = positions.shape[0] * positions.shape[1]
    return cos128.reshape(T, HEAD_PAD), sin128.reshape(T, HEAD_PAD)


def _swap_halves(w):
    half = w.shape[-1] // 2
    return jnp.concatenate([w[..., half:], w[..., :half]], axis=-1)


def _layer_weights(l, p):
    D = p["w_in"].shape[1]
    w_in = p["w_in"][l]
    o = 0
    seg = {}
    for name, width in (("cv", 3 * D_CONV), ("ssm", D_SSM), ("cq", Q_LORA), ("ckv", KV_LORA),
                        ("kr", QK_ROPE), ("g", 3 * D)):
        seg[name] = w_in[:, o:o + width]
        o += width
    pad_r = HEAD_PAD - QK_NOPE - QK_ROPE
    zl = jnp.zeros((D, QK_NOPE), F32)
    zr = jnp.zeros((D, pad_r), F32)
    w = {
        "w_cv": seg["cv"].astype(BF16), "w_ssm": seg["ssm"].astype(BF16),
        "w_lat": jnp.concatenate([seg["cq"], seg["ckv"], zl, seg["kr"], zr,
                                  zl, _swap_halves(seg["kr"]), zr], axis=1).astype(BF16),
        "w_g": seg["g"].astype(BF16),
        "convw": p["conv_w"][l],
        "qn": p["q_norm"][l].reshape(1, Q_LORA), "kvn": p["kv_norm"][l].reshape(1, KV_LORA),
        "w_upc": p["w_up_conv"][l].astype(BF16),
    }
    scale = (QK_NOPE + QK_ROPE) ** -0.5 * math.log2(math.e)
    wq = p["w_uq"][l].reshape(Q_LORA, MLA_HEADS, QK_NOPE + QK_ROPE) * scale
    zq = jnp.zeros((Q_LORA, MLA_HEADS, pad_r), F32)
    zn = jnp.zeros((Q_LORA, MLA_HEADS, QK_NOPE), F32)
    w["wq_a"] = jnp.concatenate([wq, zq], axis=-1).reshape(Q_LORA, -1).astype(BF16)
    w["wq_b"] = jnp.concatenate([zn, _swap_halves(wq[..., QK_NOPE:]), zq],
                                axis=-1).reshape(Q_LORA, -1).astype(BF16)
    wk = p["w_uk"][l].reshape(KV_LORA, MLA_HEADS, QK_NOPE)
    zk = jnp.zeros((KV_LORA, MLA_HEADS, HEAD_PAD - QK_NOPE), F32)
    w["wk"] = jnp.concatenate([wk, zk], axis=-1).reshape(KV_LORA, -1).astype(BF16)
    w["wv"] = p["w_uv"][l].T.astype(BF16)

    ar, ai = p["ssm_a_re"][l], p["ssm_a_im"][l]
    dt = jnp.exp(p["ssm_log_dt"][l])[:, None]
    mag = jnp.exp(ar * dt)
    lb_re, lb_im = mag * jnp.cos(ai * dt), mag * jnp.sin(ai * dt)
    den = ar * ar + ai * ai
    nr, ni = lb_re - 1.0, lb_im
    f_re, f_im = (nr * ar + ni * ai) / den, (ni * ar - nr * ai) / den
    br, bi = p["ssm_b_re"][l], p["ssm_b_im"][l]
    bb_re = f_re[..., None] * br - f_im[..., None] * bi
    bb_im = f_re[..., None] * bi + f_im[..., None] * br
    gpc = LANE // SSM_GROUP_DIM
    nch = SSM_GROUPS // gpc
    eye = jnp.eye(gpc, dtype=F32)

    def bmat(bb):
        return jnp.einsum("cgnp,gh->cgphn", bb.reshape(nch, gpc, SSM_STATE, SSM_GROUP_DIM),
                          eye).reshape(nch, LANE, gpc * SSM_STATE).astype(BF16)

    def cmat(cc):
        return jnp.einsum("cgpn,gh->cgnhp", cc.reshape(nch, gpc, SSM_GROUP_DIM, SSM_STATE),
                          eye).reshape(nch, gpc * SSM_STATE, LANE).astype(BF16)

    w["ssm_bre"], w["ssm_bim"] = bmat(bb_re), bmat(bb_im)
    w["ssm_cre"], w["ssm_cim"] = cmat(p["ssm_c_re"][l]), cmat(-p["ssm_c_im"][l])
    nrow = 8
    w["ssm_lre"] = jnp.broadcast_to(lb_re.reshape(1, -1), (nrow, SSM_GROUPS * SSM_STATE))
    w["ssm_lim"] = jnp.broadcast_to(lb_im.reshape(1, -1), (nrow, SSM_GROUPS * SSM_STATE))
    w["ssm_d"] = p["ssm_d"][l].reshape(1, D_SSM)
    w["ssm_wglu"] = p["ssm_w_glu"][l].astype(BF16)

    w["w_ups"] = p["w_up_ssm"][l].astype(BF16)
    w["w_upa"] = p["w_up_attn"][l].astype(BF16)
    w["w_o"] = p["w_o"][l].astype(BF16)
    w["lng0"], w["lnb0"] = p["ln_g"][l, 0].reshape(1, D), p["ln_b"][l, 0].reshape(1, D)
    w["lng1"], w["lnb1"] = p["ln_g"][l, 1].reshape(1, D), p["ln_b"][l, 1].reshape(1, D)
    rwg = p["router_w"].reshape(D, N_EXPERT_GROUPS, EXPERTS_PER_GROUP)
    rbg = p["router_bias"].reshape(N_EXPERT_GROUPS, EXPERTS_PER_GROUP)
    rwt = jnp.transpose(rwg, (2, 1, 0)).reshape(N_EXPERTS, D)
    rwt_hi = lax.bitcast_convert_type(
        lax.bitcast_convert_type(rwt, jnp.uint32) & jnp.uint32(0xFFFF0000), F32)
    w["rw"] = jnp.stack([rwt_hi.astype(BF16), (rwt - rwt_hi).astype(BF16)])
    w["rb"] = jnp.transpose(rbg, (1, 0)).reshape(N_EXPERTS, 1)
    w["ws1"] = p["shared_w1"][l].astype(BF16)
    w["ws3"] = p["shared_w3"][l].astype(BF16)
    w["ws2"] = p["shared_w2"][l].astype(BF16)
    return w


def kernel(x, c, positions, w_ada, b_ada, w_in, conv_w, ssm_a_re, ssm_a_im, ssm_b_re, ssm_b_im,
           ssm_c_re, ssm_c_im, ssm_d, ssm_log_dt, ssm_w_glu, q_norm, w_uq, kv_norm, w_uk, w_uv,
           w_up_conv, w_up_ssm, w_up_attn, w_o, ln_g, ln_b, router_w, router_bias,
           exp_w1, exp_w3, exp_w2, shared_w1, shared_w3, shared_w2):
    B, S, D = x.shape
    L = w_in.shape[0]
    T = B * S
    assert B == 8 and S % max(TM_PROJ, T_CHUNK, TQ, FINAL_ROWS) == 0 and T % max(MOE_ROWS, PERM_ROWS) == 0
    assert TM_PROJ == TQ
    p = dict(w_in=w_in, conv_w=conv_w, ssm_a_re=ssm_a_re, ssm_a_im=ssm_a_im, ssm_b_re=ssm_b_re,
             ssm_b_im=ssm_b_im, ssm_c_re=ssm_c_re, ssm_c_im=ssm_c_im, ssm_d=ssm_d,
             ssm_log_dt=ssm_log_dt, ssm_w_glu=ssm_w_glu, q_norm=q_norm, w_uq=w_uq, kv_norm=kv_norm,
             w_uk=w_uk, w_uv=w_uv, w_up_conv=w_up_conv, w_up_ssm=w_up_ssm, w_up_attn=w_up_attn,
             w_o=w_o, ln_g=ln_g, ln_b=ln_b, router_w=router_w, router_bias=router_bias,
             exp_w1=exp_w1, exp_w3=exp_w3, exp_w2=exp_w2, shared_w1=shared_w1,
             shared_w3=shared_w3, shared_w2=shared_w2)
    alpha = (2 * L) ** 0.25
    cos128, sin128 = _rope_tables(positions)
    ada = _ada(c, w_ada, b_ada).reshape(L, B, 6, D)
    x2d = x.reshape(T, D)
    for l in range(L):
        wts = _layer_weights(l, p)
        pc, sgs, sga, u, q, k, v = _inproj(x2d, ada[l], cos128, sin128, wts, B, S)
        ys = _ssm(u.reshape(B, S, D_SSM), wts, B, S).reshape(T, D_SSM)
        ya = _attention(q, k, v, B, S)
        base, hr, rt = _merge(x2d, ada[l], pc, sgs, sga, ys, ya, wts, B, S, alpha)
        gid = rt[EXPERTS_PER_GROUP].astype(jnp.int32)
        dest, block_g, nvalid = _route_plan(gid, T)
        hs = _permute(hr, dest, nvalid)
        ysort = _moe(hs, block_g, nvalid, exp_w1, exp_w3, exp_w2, l)
        x2d = _final(base, ysort, dest, ada[l], wts["lng1"], wts["lnb1"], B, S)
    return x2d.reshape(B, S, D)
```

```python
import functools
import math

import jax
import jax.numpy as jnp
from jax import lax
from jax.experimental import pallas as pl
from jax.experimental.pallas import tpu as pltpu

F32 = jnp.float32
BF16 = jnp.bfloat16

D_CONV = 512
D_SSM = 512
SSM_GROUP_DIM = 16
SSM_GROUPS = 32
SSM_STATE = 64
MLA_HEADS = 8
QK_NOPE = 64
QK_ROPE = 32
V_HEAD = 64
Q_LORA = 256
KV_LORA = 128
ROPE_THETA = 10000.0
N_EXPERTS = 32
N_EXPERT_GROUPS = 8
EXPERTS_PER_GROUP = 4
D_EXPERT = 256
LN_EPS = 1e-5
RMS_EPS = 1e-6

LANE = 128
HEAD_PAD = 128
VMEM_LIMIT = 56 * 1024 * 1024

TM_PROJ = 512
T_CHUNK = 64
TQ = 512
ATTN_HEADS = 8
DEN_ROWS = 16
MOE_ROWS = 512
PERM_ROWS = 2048
FINAL_ROWS = 1024
ROUTE_W = 128


def _cparams(n_axes):
    return pltpu.CompilerParams(dimension_semantics=("arbitrary",) * n_axes,
                                vmem_limit_bytes=VMEM_LIMIT)


def _ln_plain(x):
    mu = jnp.mean(x, axis=-1, keepdims=True)
    xc = x - mu
    var = jnp.mean(xc * xc, axis=-1, keepdims=True)
    return xc * lax.rsqrt(var + LN_EPS)


def _rms(x, g):
    return x * lax.rsqrt(jnp.mean(x * x, axis=-1, keepdims=True) + RMS_EPS) * g


def _bdot(a, b):
    return jnp.dot(a.astype(BF16), b, preferred_element_type=F32)


def _split_bf16(x):
    hi = pltpu.bitcast(pltpu.bitcast(x, jnp.uint32) & jnp.uint32(0xFFFF0000), F32)
    return hi.astype(BF16), (x - hi).astype(BF16)


def _ada_kernel(c_ref, w_ref, b_ref, o_ref):
    c = c_ref[...]
    cond = c * jax.nn.sigmoid(c)
    c_hi, c_lo = _split_bf16(cond)
    w_hi, w_lo = _split_bf16(w_ref[...])
    o_ref[...] = (jnp.dot(c_hi, w_hi, preferred_element_type=F32)
                  + jnp.dot(c_lo, w_hi, preferred_element_type=F32)
                  + jnp.dot(c_hi, w_lo, preferred_element_type=F32)
                  + jnp.dot(c_lo, w_lo, preferred_element_type=F32)) + b_ref[...]


def _ada(c, w_ada, b_ada):
    L, D, D6 = w_ada.shape
    B = c.shape[0]
    wn = D6 // 2
    nj = D6 // wn
    return pl.pallas_call(
        _ada_kernel,
        out_shape=jax.ShapeDtypeStruct((L, B, D6), F32),
        grid=(L, nj),
        in_specs=[pl.BlockSpec((B, D), lambda l, j: (0, 0)),
                  pl.BlockSpec((None, D, wn), lambda l, j: (l, 0, j)),
                  pl.BlockSpec((None, 1, wn), lambda l, j: (l, 0, j))],
        out_specs=pl.BlockSpec((None, B, wn), lambda l, j: (l, 0, j)),
        compiler_params=_cparams(2),
        name="ada",
    )(c, w_ada, b_ada.reshape(L, 1, D6))


def _inproj_kernel(x_ref, ada_ref, cos_ref, sin_ref,
                   w_cv, w_ssm, w_lat, w_g,
                   convw, qn, wq_a, wq_b, kvn, wk, wv, w_upc,
                   pc_ref, sgs_ref, sga_ref, u_ref, q_ref, k_ref, v_ref,
                   carry_ref):
    si = pl.program_id(1)
    D = x_ref.shape[1]
    tm = x_ref.shape[0]

    @pl.when(si == 0)
    def _():
        carry_ref[...] = jnp.zeros_like(carry_ref)

    ada = ada_ref[...]
    h = _ln_plain(x_ref[...]) * (1.0 + ada[1:2, :]) + ada[0:1, :]
    hb = h.astype(BF16)

    pcv = jnp.dot(hb, w_cv[...], preferred_element_type=F32)
    u = pcv[:, D_CONV:2 * D_CONV] * pcv[:, :D_CONV]
    gb = pcv[:, 2 * D_CONV:]
    tail = carry_ref[...]
    row8 = lax.broadcasted_iota(jnp.int32, (8, D_CONV), 0)
    r1 = pltpu.roll(u, 1, 0)
    r2 = pltpu.roll(u, 2, 0)
    t1 = pltpu.roll(tail, 1, 0)
    t2 = pltpu.roll(tail, 2, 0)
    u1 = jnp.concatenate([jnp.where(row8 < 1, t1, r1[:8]), r1[8:]], axis=0)
    u2 = jnp.concatenate([jnp.where(row8 < 2, t2, r2[:8]), r2[8:]], axis=0)
    cw = convw[...]
    yc = gb * (cw[0:1, :] * u2 + cw[1:2, :] * u1 + cw[2:3, :] * u)
    carry_ref[...] = u[tm - 8:, :]

    gates = jnp.dot(hb, w_g[...], preferred_element_type=F32)
    pc_ref[...] = (jax.nn.sigmoid(gates[:, :D]) * _bdot(yc, w_upc[...])).astype(BF16)
    sgs_ref[...] = jax.nn.sigmoid(gates[:, D:2 * D]).astype(BF16)
    sga_ref[...] = jax.nn.sigmoid(gates[:, 2 * D:]).astype(BF16)

    u_ref[...] = jnp.dot(hb, w_ssm[...], preferred_element_type=F32).astype(BF16)

    cos = cos_ref[...]
    sin = sin_ref[...]
    lat = jnp.dot(hb, w_lat[...], preferred_element_type=F32)
    o_kv, o_kr = Q_LORA, Q_LORA + KV_LORA
    cq = _rms(lat[:, :o_kv], qn[...]).astype(BF16)
    qa = jnp.dot(cq, wq_a[...], preferred_element_type=F32)
    qb = jnp.dot(cq, wq_b[...], preferred_element_type=F32)
    ckv = _rms(lat[:, o_kv:o_kr], kvn[...]).astype(BF16)
    kn = jnp.dot(ckv, wk[...], preferred_element_type=F32)
    kr = lat[:, o_kr:o_kr + HEAD_PAD] * cos + lat[:, o_kr + HEAD_PAD:] * sin
    for hd in range(MLA_HEADS):
        sl = slice(hd * HEAD_PAD, (hd + 1) * HEAD_PAD)
        q_ref[:, sl] = (qa[:, sl] * cos + qb[:, sl] * sin).astype(BF16)
        k_ref[:, sl] = (kn[:, sl] + kr).astype(BF16)
    v_ref[...] = lax.dot_general(wv[...], ckv, (((1,), (1,)), ((), ())),
                                 preferred_element_type=F32).astype(BF16)


def _inproj(x2d, ada_l, cos128, sin128, wts, B, S):
    T, D = x2d.shape
    tm = TM_PROJ
    ns = S // tm
    row = lambda b, s: (b * ns + s, 0)
    full = lambda a: pl.BlockSpec(a.shape, lambda b, s: (0,) * a.ndim, pipeline_mode=pl.Buffered(1))
    names = ["w_cv", "w_ssm", "w_lat", "w_g", "convw", "qn",
             "wq_a", "wq_b", "kvn", "wk", "wv", "w_upc"]
    ws = [wts[n] for n in names]
    out_shape = (
        jax.ShapeDtypeStruct((T, D), BF16),
        jax.ShapeDtypeStruct((T, D), BF16),
        jax.ShapeDtypeStruct((T, D), BF16),
        jax.ShapeDtypeStruct((T, D_SSM), BF16),
        jax.ShapeDtypeStruct((T, MLA_HEADS * HEAD_PAD), BF16),
        jax.ShapeDtypeStruct((T, MLA_HEADS * HEAD_PAD), BF16),
        jax.ShapeDtypeStruct((B, ns, MLA_HEADS * V_HEAD, tm), BF16),
    )
    out_specs = (
        pl.BlockSpec((tm, D), row), pl.BlockSpec((tm, D), row), pl.BlockSpec((tm, D), row),
        pl.BlockSpec((tm, D_SSM), row),
        pl.BlockSpec((tm, MLA_HEADS * HEAD_PAD), row),
        pl.BlockSpec((tm, MLA_HEADS * HEAD_PAD), row),
        pl.BlockSpec((None, None, MLA_HEADS * V_HEAD, tm), lambda b, s: (b, s, 0, 0)),
    )
    return pl.pallas_call(
        _inproj_kernel,
        out_shape=out_shape,
        grid=(B, ns),
        in_specs=[pl.BlockSpec((tm, D), row),
                  pl.BlockSpec((None, 6, D), lambda b, s: (b, 0, 0)),
                  pl.BlockSpec((tm, LANE), row), pl.BlockSpec((tm, LANE), row)]
                 + [full(a) for a in ws],
        out_specs=out_specs,
        scratch_shapes=[pltpu.VMEM((8, D_CONV), F32)],
        compiler_params=_cparams(2),
        name="inproj",
    )(x2d, ada_l, cos128, sin128, *ws)


def _ssm_kernel(u_ref, bre_ref, bim_ref, lre_ref, lim_ref, cre_ref, cim_ref, d_ref, wglu_ref, perm_ref,
                o_ref, are, aim, au, bre_s, bim_s, bu, st_re, st_im, tmaj):
    i = pl.program_id(0)
    batch, nsteps = u_ref.shape[0], u_ref.shape[1]
    rows = batch * nsteps
    nchunk = bre_ref.shape[0]
    cw = bre_ref.shape[2]

    @pl.when(i == 0)
    def _():
        st_re[...] = jnp.zeros_like(st_re)
        st_im[...] = jnp.zeros_like(st_im)
        bre_s[...] = jnp.zeros_like(bre_s)
        bim_s[...] = jnp.zeros_like(bim_s)
        bu[...] = jnp.zeros_like(bu)

    def body(cur_re, cur_im, cur_u, prev_re, prev_im, prev_u):
        for b in range(batch):
            tmaj[:, b, :] = u_ref[b].astype(F32)
        uf = tmaj[...].reshape(rows, D_SSM)
        cur_u[...] = uf
        ub = uf.astype(BF16)
        for c in range(nchunk):
            uc = ub[:, c * LANE:(c + 1) * LANE]
            cur_re[:, c * cw:(c + 1) * cw] = jnp.dot(uc, bre_ref[c], preferred_element_type=F32)
            cur_im[:, c * cw:(c + 1) * cw] = jnp.dot(uc, bim_ref[c], preferred_element_type=F32)

        ys = []

        def out_chunk(c):
            ys.append(_bdot(prev_re[:, c * cw:(c + 1) * cw], cre_ref[c])
                      + _bdot(prev_im[:, c * cw:(c + 1) * cw], cim_ref[c]))

        def out_finish():
            y = jnp.concatenate(ys, axis=1) + d_ref[...] * prev_u[...]
            y = 0.5 * y * (1.0 + jnp.tanh(math.sqrt(2.0 / math.pi) * (y + 0.044715 * (y * y * y))))
            z = _bdot(y, wglu_ref[...])
            g = (z[:, :D_SSM] * jax.nn.sigmoid(z[:, D_SSM:])).astype(BF16)
            o_ref[...] = jnp.dot(perm_ref[...], g, preferred_element_type=F32).astype(BF16).reshape(
                batch, nsteps, D_SSM)

        pieces = [functools.partial(out_chunk, c) for c in range(nchunk)] + [out_finish]
        every = nsteps // len(pieces)
        pr, pi = st_re[...], st_im[...]
        for t in range(nsteps):
            r = t * batch
            lre = lre_ref[...]
            lim = lim_ref[...]
            nr = lre * pr - lim * pi + cur_re[r:r + batch, :]
            ni = lre * pi + lim * pr + cur_im[r:r + batch, :]
            cur_re[r:r + batch, :] = nr
            cur_im[r:r + batch, :] = ni
            pr, pi = nr, ni
            if t % every == every - 1 and pieces:
                pieces.pop(0)()
        while pieces:
            pieces.pop(0)()
        st_re[...] = pr
        st_im[...] = pi

    @pl.when(i % 2 == 0)
    def _():
        body(are, aim, au, bre_s, bim_s, bu)

    @pl.when(i % 2 == 1)
    def _():
        body(bre_s, bim_s, bu, are, aim, au)


def _ssm(u, wts, B, S):
    rows = T_CHUNK * B
    n = S // T_CHUNK
    nstate = SSM_GROUPS * SSM_STATE
    names = ["ssm_bre", "ssm_bim", "ssm_lre", "ssm_lim", "ssm_cre", "ssm_cim", "ssm_d", "ssm_wglu"]
    r = jnp.arange(rows, dtype=jnp.int32)
    src = (r % T_CHUNK) * B + r // T_CHUNK
    perm = (src[:, None] == jnp.arange(rows, dtype=jnp.int32)[None, :]).astype(BF16)
    ws = [wts[k] for k in names] + [perm]
    full = lambda a: pl.BlockSpec(a.shape, lambda i: (0,) * a.ndim, pipeline_mode=pl.Buffered(1))
    return pl.pallas_call(
        _ssm_kernel,
        out_shape=jax.ShapeDtypeStruct((B, S, D_SSM), BF16),
        grid=(n + 1,),
        in_specs=[pl.BlockSpec((B, T_CHUNK, D_SSM), lambda i: (0, jnp.minimum(i, n - 1), 0))]
                 + [full(a) for a in ws],
        out_specs=pl.BlockSpec((B, T_CHUNK, D_SSM), lambda i: (0, jnp.maximum(i - 1, 0), 0)),
        scratch_shapes=[pltpu.VMEM((rows, nstate), F32), pltpu.VMEM((rows, nstate), F32),
                        pltpu.VMEM((rows, D_SSM), F32),
                        pltpu.VMEM((rows, nstate), F32), pltpu.VMEM((rows, nstate), F32),
                        pltpu.VMEM((rows, D_SSM), F32),
                        pltpu.VMEM((B, nstate), F32), pltpu.VMEM((B, nstate), F32),
                        pltpu.VMEM((T_CHUNK, B, D_SSM), F32)],
        compiler_params=_cparams(1),
        name="ssm",
    )(u, *ws)


def _attn_kernel(q_ref, k_ref, vt_ref, o_ref, m_sc, acc_sc, sa_sc, sb_sc, p_sc):
    qi = pl.program_id(2)
    tq = q_ref.shape[0]
    nh = m_sc.shape[0]
    neg = -1e30
    m_sc[...] = jnp.full(m_sc.shape, neg, F32)
    acc_sc[...] = jnp.zeros(acc_sc.shape, F32)
    ones = jnp.ones((DEN_ROWS, tq), BF16)

    def scores(j, s_sc):
        off = pl.multiple_of(j * tq, tq)
        for hh in range(nh):
            q = q_ref[:, hh * HEAD_PAD:(hh + 1) * HEAD_PAD]
            kb = k_ref[pl.ds(off, tq), hh * HEAD_PAD:(hh + 1) * HEAD_PAD]
            s_sc[hh] = lax.dot_general(kb, q, (((1,), (1,)), ((), ())),
                                       preferred_element_type=F32)

    def update(j, s_sc, masked):
        scale = []
        for hh in range(nh):
            if masked:
                keyi = lax.broadcasted_iota(jnp.int32, (tq, tq), 0)
                qryi = lax.broadcasted_iota(jnp.int32, (tq, tq), 1)
                s_sc[hh] = jnp.where(keyi <= qryi, s_sc[hh], neg)
            m_prev = m_sc[hh]
            m_new = jnp.maximum(m_prev, jnp.max(s_sc[hh], axis=0, keepdims=True))
            m_sc[hh] = m_new
            p_sc[hh] = jnp.exp2(s_sc[hh] - m_new).astype(BF16)
            scale.append(jnp.exp2(m_prev - m_new))
        for hh in range(nh):
            vte = jnp.concatenate([vt_ref[j, hh * V_HEAD:(hh + 1) * V_HEAD, :], ones], axis=0)
            acc_sc[hh] = scale[hh] * acc_sc[hh] + jnp.dot(vte, p_sc[hh], preferred_element_type=F32)

    scores(0, sa_sc)

    def pair(pp, c):
        j = 2 * pp
        scores(j + 1, sb_sc)
        update(j, sa_sc, False)
        scores(j + 2, sa_sc)
        update(j + 1, sb_sc, False)
        return c

    npair = qi // 2
    lax.fori_loop(0, npair, pair, 0)
    j0 = 2 * npair

    @pl.when(qi % 2 == 0)
    def _():
        update(j0, sa_sc, True)

    @pl.when(qi % 2 == 1)
    def _():
        scores(j0 + 1, sb_sc)
        update(j0, sa_sc, False)
        update(j0 + 1, sb_sc, True)

    outs = []
    for pr in range(nh // 2):
        halves = []
        for hh in (2 * pr, 2 * pr + 1):
            acc = acc_sc[hh]
            halves.append(acc[:V_HEAD] * (1.0 / acc[V_HEAD:V_HEAD + 1]))
        outs.append(jnp.concatenate(halves, axis=0).T)
    o_ref[...] = jnp.concatenate(outs, axis=1).astype(BF16)


def _attention(q, k, vt, B, S):
    nq = S // TQ
    nh = ATTN_HEADS
    hg = MLA_HEADS // nh
    return pl.pallas_call(
        _attn_kernel,
        out_shape=jax.ShapeDtypeStruct((B * S, MLA_HEADS * V_HEAD), BF16),
        grid=(B, hg, nq),
        in_specs=[pl.BlockSpec((TQ, nh * HEAD_PAD), lambda b, h, i: (b * nq + i, h)),
                  pl.BlockSpec((S, nh * HEAD_PAD), lambda b, h, i: (b, h)),
                  pl.BlockSpec((None, nq, nh * V_HEAD, TQ), lambda b, h, i: (b, 0, h, 0))],
        out_specs=pl.BlockSpec((TQ, nh * V_HEAD), lambda b, h, i: (b * nq + i, h)),
        scratch_shapes=[pltpu.VMEM((nh, 1, TQ), F32),
                        pltpu.VMEM((nh, V_HEAD + DEN_ROWS, TQ), F32),
                        pltpu.VMEM((nh, TQ, TQ), F32), pltpu.VMEM((nh, TQ, TQ), F32),
                        pltpu.VMEM((nh, TQ, TQ), BF16)],
        compiler_params=_cparams(3),
        name="attention",
    )(q, k, vt)


def _merge_kernel(x_ref, ada_ref, pc_ref, sgs_ref, sga_ref, ys_ref, ya_ref,
                  w_ups, w_upa, w_o, lng, lnb, rw, rb, ws1, ws3, ws2,
                  base_ref, hr_ref, rt_ref, *, alpha):
    D = x_ref.shape[1]
    tm = x_ref.shape[0]
    ada = ada_ref[...]
    merged = (pc_ref[...]
              + sgs_ref[...] * jnp.dot(ys_ref[...], w_ups[...], preferred_element_type=F32).astype(BF16)
              + sga_ref[...] * jnp.dot(ya_ref[...], w_upa[...], preferred_element_type=F32).astype(BF16))
    y = jnp.dot(merged, w_o[...], preferred_element_type=F32)
    x1 = _ln_plain(alpha * x_ref[...] + ada[2:3, :] * y) * lng[...] + lnb[...]
    h2 = _ln_plain(x1) * (1.0 + ada[4:5, :]) + ada[3:4, :]
    hr_ref[:, :D] = h2

    G = N_EXPERT_GROUPS
    nt = (((1,), (1,)), ((), ()))
    h2_hi, h2_lo = _split_bf16(h2)
    rw_hi, rw_lo = rw[0], rw[1]
    logits = (lax.dot_general(rw_hi, h2_hi, nt, preferred_element_type=F32)
              + lax.dot_general(rw_lo, h2_hi, nt, preferred_element_type=F32)
              + lax.dot_general(rw_hi, h2_lo, nt, preferred_element_type=F32)
              + lax.dot_general(rw_lo, h2_lo, nt, preferred_element_type=F32))
    sc0 = jax.nn.sigmoid(logits)
    sl0 = sc0 + rb[...]
    sc = [sc0[j * G:(j + 1) * G] for j in range(EXPERTS_PER_GROUP)]
    sl = [sl0[j * G:(j + 1) * G] for j in range(EXPERTS_PER_GROUP)]
    hi1, lo1 = jnp.maximum(sl[0], sl[1]), jnp.minimum(sl[0], sl[1])
    hi2, lo2 = jnp.maximum(sl[2], sl[3]), jnp.minimum(sl[2], sl[3])
    gscore = jnp.maximum(hi1, hi2) + jnp.maximum(jnp.minimum(hi1, hi2), jnp.maximum(lo1, lo2))
    sub = lax.broadcasted_iota(jnp.int32, (G, tm), 0)
    gmax = jnp.max(gscore, axis=0, keepdims=True)
    gidx = jnp.min(jnp.where(gscore == gmax, sub, G), axis=0, keepdims=True)
    pick = sub == gidx
    vs = [jnp.sum(jnp.where(pick, a, 0.0), axis=0, keepdims=True) for a in sl]
    ss = [jnp.sum(jnp.where(pick, a, 0.0), axis=0, keepdims=True) for a in sc]
    chosen = []
    for j in range(EXPERTS_PER_GROUP):
        cnt = jnp.zeros((1, tm), jnp.int32)
        for k in range(EXPERTS_PER_GROUP):
            if k == j:
                continue
            beats = (vs[k] > vs[j]) | ((vs[k] == vs[j]) & (k < j))
            cnt = cnt + beats.astype(jnp.int32)
        chosen.append(cnt < 2)
    wsel = [jnp.where(chosen[j], ss[j], 0.0) for j in range(EXPERTS_PER_GROUP)]
    wsum = wsel[0] + wsel[1] + wsel[2] + wsel[3]
    rows = [wsel[j] / wsum for j in range(EXPERTS_PER_GROUP)] + [gidx.astype(F32)]
    rows += [jnp.zeros((1, tm), F32)] * (8 - len(rows))
    route_t = jnp.concatenate(rows, axis=0)
    rt_ref[...] = route_t
    eye = (lax.broadcasted_iota(jnp.int32, (8, ROUTE_W), 0)
           == lax.broadcasted_iota(jnp.int32, (8, ROUTE_W), 1)).astype(BF16)
    tn = (((0,), (0,)), ((), ()))
    rt_hi, rt_lo = _split_bf16(route_t)
    hr_ref[:, D:] = (lax.dot_general(rt_hi, eye, tn, preferred_element_type=F32)
                     + lax.dot_general(rt_lo, eye, tn, preferred_element_type=F32))

    h2b = h2.astype(BF16)
    a1 = jnp.dot(h2b, ws1[...], preferred_element_type=F32)
    a3 = jnp.dot(h2b, ws3[...], preferred_element_type=F32)
    ysh = _bdot(a1 * jax.nn.sigmoid(a1) * a3, ws2[...])
    base_ref[...] = alpha * x1 + ada[5:6, :] * ysh


def _merge(x2d, ada_l, pc, sgs, sga, ys, ya, wts, B, S, alpha):
    T, D = x2d.shape
    tm = TM_PROJ
    ns = S // tm
    row = lambda b, s: (b * ns + s, 0)
    names = ["w_ups", "w_upa", "w_o", "lng0", "lnb0", "rw", "rb", "ws1", "ws3", "ws2"]
    ws = [wts[n] for n in names]
    full = lambda a: pl.BlockSpec(a.shape, lambda b, s: (0,) * a.ndim, pipeline_mode=pl.Buffered(1))
    return pl.pallas_call(
        functools.partial(_merge_kernel, alpha=alpha),
        out_shape=(jax.ShapeDtypeStruct((T, D), F32),
                   jax.ShapeDtypeStruct((T, D + ROUTE_W), F32),
                   jax.ShapeDtypeStruct((8, T), F32)),
        grid=(B, ns),
        in_specs=[pl.BlockSpec((tm, D), row),
                  pl.BlockSpec((None, 6, D), lambda b, s: (b, 0, 0)),
                  pl.BlockSpec((tm, D), row), pl.BlockSpec((tm, D), row), pl.BlockSpec((tm, D), row),
                  pl.BlockSpec((tm, D_SSM), row),
                  pl.BlockSpec((tm, MLA_HEADS * V_HEAD), row)]
                 + [full(a) for a in ws],
        out_specs=(pl.BlockSpec((tm, D), row), pl.BlockSpec((tm, D + ROUTE_W), row),
                   pl.BlockSpec((8, tm), lambda b, s: (0, b * ns + s))),
        compiler_params=_cparams(2),
        name="merge",
    )(x2d, ada_l, pc, sgs, sga, ys, ya, *ws)


def _route_plan(gid, T):
    R = MOE_ROWS
    G = N_EXPERT_GROUPS
    onehot = (gid[:, None] == jnp.arange(G, dtype=jnp.int32)[None, :]).astype(jnp.int32)
    csum = jnp.cumsum(onehot, axis=0)
    counts = csum[-1]
    rank = jnp.sum(onehot * csum, axis=1) - 1
    padded = (counts + R - 1) // R * R
    pad_end = jnp.cumsum(padded)
    pad_start = pad_end - padded
    dest = jnp.sum(onehot * pad_start[None, :], axis=1) + rank
    nb = T // R + G
    starts = jnp.arange(nb, dtype=jnp.int32) * R
    block_g = jnp.minimum(jnp.sum((starts[:, None] >= pad_end[None, :]).astype(jnp.int32), axis=1), G - 1)
    gsel = (block_g[:, None] == jnp.arange(G, dtype=jnp.int32)[None, :]).astype(jnp.int32)
    group_end = jnp.sum(gsel * (pad_start + counts)[None, :], axis=1)
    nvalid = jnp.clip(group_end - starts, 0, R)
    return dest.astype(jnp.int32), block_g.astype(jnp.int32), nvalid.astype(jnp.int32)


def _permute_kernel(dest_ref, nvalid_ref, hr_ref, hs_hbm, zbuf, sem, zsem):
    i = pl.program_id(0)
    tp = hr_ref.shape[0]
    R = zbuf.shape[0]
    nb = nvalid_ref.shape[0]

    @pl.when(i == 0)
    def _():
        zbuf[...] = jnp.zeros(zbuf.shape, F32)

        def fill(b, n):
            partial = nvalid_ref[b] < R

            @pl.when(partial)
            def _():
                pltpu.make_async_copy(zbuf, hs_hbm.at[pl.ds(b * R, R)], zsem).start()

            return n + partial.astype(jnp.int32)

        nfill = lax.fori_loop(0, nb, fill, 0)

        def drain(k, c):
            pltpu.make_async_copy(zbuf, hs_hbm.at[pl.ds(0, R)], zsem).wait()
            return c

        lax.fori_loop(0, nfill, drain, 0)

    def body(g, c):
        r0 = pl.multiple_of(g * 8, 8)
        for k in range(8):
            d = dest_ref[i * tp + r0 + k]
            pltpu.make_async_copy(hr_ref.at[pl.ds(r0 + k, 1)], hs_hbm.at[pl.ds(d, 1)], sem).start()
        return c

    lax.fori_loop(0, tp // 8, body, 0)
    pltpu.make_async_copy(hr_ref, hs_hbm.at[pl.ds(0, tp)], sem).wait()


def _permute(hr, dest, nvalid):
    T, W = hr.shape
    R = MOE_ROWS
    tp = PERM_ROWS
    nb = nvalid.shape[0]
    return pl.pallas_call(
        _permute_kernel,
        out_shape=jax.ShapeDtypeStruct((nb * R, W), F32),
        grid_spec=pltpu.PrefetchScalarGridSpec(
            num_scalar_prefetch=2,
            grid=(T // tp,),
            in_specs=[pl.BlockSpec((tp, W), lambda i, d, n: (i, 0))],
            out_specs=pl.BlockSpec(memory_space=pl.ANY),
            scratch_shapes=[pltpu.VMEM((R, W), F32), pltpu.SemaphoreType.DMA, pltpu.SemaphoreType.DMA]),
        compiler_params=_cparams(1),
        name="permute",
    )(dest, nvalid, hr)


def _moe_kernel(bg_ref, nvalid_ref, hs_ref, w1_ref, w3_ref, w2_ref, y_ref, w1b, w3b, w2b):
    i = pl.program_id(0)
    R, D = y_ref.shape

    @pl.when((i == 0) | (bg_ref[i] != bg_ref[jnp.maximum(i - 1, 0)]))
    def _():
        w1b[...] = w1_ref[...].astype(BF16)
        w3b[...] = w3_ref[...].astype(BF16)
        w2b[...] = w2_ref[...].astype(BF16)

    @pl.when(nvalid_ref[i] > 0)
    def _():
        xb = hs_ref[:, :D].astype(BF16)
        acc = jnp.zeros((R, D), F32)
        for j in range(EXPERTS_PER_GROUP):
            a1 = jnp.dot(xb, w1b[j], preferred_element_type=F32)
            a3 = jnp.dot(xb, w3b[j], preferred_element_type=F32)
            hj = a1 * jax.nn.sigmoid(a1) * a3 * hs_ref[:, D + j:D + j + 1]
            acc = acc + _bdot(hj, w2b[j])
        y_ref[...] = acc

    @pl.when(nvalid_ref[i] == 0)
    def _():
        y_ref[...] = jnp.zeros((R, D), F32)


def _moe(hs, block_g, nvalid, w1, w3, w2, layer):
    D = w1.shape[2]
    W = hs.shape[1]
    R = MOE_ROWS
    nb = nvalid.shape[0]
    wspec_in = pl.BlockSpec((None, EXPERTS_PER_GROUP, D, D_EXPERT), lambda i, bg, nv: (layer, bg[i], 0, 0))
    wspec_out = pl.BlockSpec((None, EXPERTS_PER_GROUP, D_EXPERT, D), lambda i, bg, nv: (layer, bg[i], 0, 0))
    return pl.pallas_call(
        _moe_kernel,
        out_shape=jax.ShapeDtypeStruct((nb * R, D), F32),
        grid_spec=pltpu.PrefetchScalarGridSpec(
            num_scalar_prefetch=2,
            grid=(nb,),
            in_specs=[pl.BlockSpec((R, W), lambda i, bg, nv: (i, 0)), wspec_in, wspec_in, wspec_out],
            out_specs=pl.BlockSpec((R, D), lambda i, bg, nv: (i, 0)),
            scratch_shapes=[pltpu.VMEM((EXPERTS_PER_GROUP, D, D_EXPERT), BF16),
                            pltpu.VMEM((EXPERTS_PER_GROUP, D, D_EXPERT), BF16),
                            pltpu.VMEM((EXPERTS_PER_GROUP, D_EXPERT, D), BF16)]),
        compiler_params=_cparams(1),
        name="moe",
    )(block_g, nvalid, hs, w1, w3, w2)


def _final_kernel(dest_ref, base_ref, ys_hbm, ada_ref, lng, lnb, o_ref, ybuf, sem):
    i = pl.program_id(0)
    n = pl.num_programs(0)
    tf = base_ref.shape[0]
    slot = i % 2

    def start_gather(step, s):
        def body(g, c):
            r0 = pl.multiple_of(g * 8, 8)
            for k in range(8):
                d = dest_ref[step * tf + r0 + k]
                pltpu.make_async_copy(ys_hbm.at[pl.ds(d, 1)], ybuf.at[s, pl.ds(r0 + k, 1)],
                                      sem.at[s]).start()
            return c

        lax.fori_loop(0, tf // 8, body, 0)

    @pl.when(i == 0)
    def _():
        start_gather(0, 0)

    @pl.when(i + 1 < n)
    def _():
        start_gather(i + 1, 1 - slot)

    pltpu.make_async_copy(ys_hbm.at[pl.ds(0, tf)], ybuf.at[slot], sem.at[slot]).wait()
    ada = ada_ref[...]
    o_ref[...] = _ln_plain(base_ref[...] + ada[5:6, :] * ybuf[slot]) * lng[...] + lnb[...]


def _final(base, ys, dest, ada_l, lng, lnb, B, S):
    T, D = base.shape
    tf = FINAL_ROWS
    per_batch = S // tf
    return pl.pallas_call(
        _final_kernel,
        out_shape=jax.ShapeDtypeStruct((T, D), F32),
        grid_spec=pltpu.PrefetchScalarGridSpec(
            num_scalar_prefetch=1,
            grid=(T // tf,),
            in_specs=[pl.BlockSpec((tf, D), lambda i, d: (i, 0)),
                      pl.BlockSpec(memory_space=pl.ANY),
                      pl.BlockSpec((None, 6, D), lambda i, d: (i // per_batch, 0, 0)),
                      pl.BlockSpec((1, D), lambda i, d: (0, 0)), pl.BlockSpec((1, D), lambda i, d: (0, 0))],
            out_specs=pl.BlockSpec((tf, D), lambda i, d: (i, 0)),
            scratch_shapes=[pltpu.VMEM((2, tf, D), F32), pltpu.SemaphoreType.DMA((2,))]),
        compiler_params=_cparams(1),
        name="final_ln",
    )(dest, base, ys, ada_l, lng, lnb)


def _rope_tables(positions):
    inv_freq = ROPE_THETA ** (-jnp.arange(0, QK_ROPE, 2, dtype=F32) / QK_ROPE)
    ang = positions.astype(F32)[..., None] * inv_freq
    dense = lax.optimization_barrier(ang.reshape(-1, LANE))
    cos, sin = lax.optimization_barrier((jnp.cos(dense), jnp.sin(dense)))
    cos, sin = cos.reshape(ang.shape), sin.reshape(ang.shape)
    ones = jnp.ones(cos.shape[:-1] + (QK_NOPE,), F32)
    zpad = jnp.zeros(cos.shape[:-1] + (HEAD_PAD - QK_NOPE - QK_ROPE,), F32)
    cos128 = jnp.concatenate([ones, cos, cos, zpad], axis=-1)
    sin128 = jnp.concatenate([0.0 * ones, -sin, sin, zpad], axis=-1)
    T = positions.shape[0] * positions.shape[1]
    return cos128.reshape(T, HEAD_PAD), sin128.reshape(T, HEAD_PAD)


def _swap_halves(w):
    half = w.shape[-1] // 2
    return jnp.concatenate([w[..., half:], w[..., :half]], axis=-1)


def _layer_weights(l, p):
    D = p["w_in"].shape[1]
    w_in = p["w_in"][l]
    o = 0
    seg = {}
    for name, width in (("cv", 3 * D_CONV), ("ssm", D_SSM), ("cq", Q_LORA), ("ckv", KV_LORA),
                        ("kr", QK_ROPE), ("g", 3 * D)):
        seg[name] = w_in[:, o:o + width]
        o += width
    pad_r = HEAD_PAD - QK_NOPE - QK_ROPE
    zl = jnp.zeros((D, QK_NOPE), F32)
    zr = jnp.zeros((D, pad_r), F32)
    w = {
        "w_cv": seg["cv"].astype(BF16), "w_ssm": seg["ssm"].astype(BF16),
        "w_lat": jnp.concatenate([seg["cq"], seg["ckv"], zl, seg["kr"], zr,
                                  zl, _swap_halves(seg["kr"]), zr], axis=1).astype(BF16),
        "w_g": seg["g"].astype(BF16),
        "convw": p["conv_w"][l],
        "qn": p["q_norm"][l].reshape(1, Q_LORA), "kvn": p["kv_norm"][l].reshape(1, KV_LORA),
        "w_upc": p["w_up_conv"][l].astype(BF16),
    }
    scale = (QK_NOPE + QK_ROPE) ** -0.5 * math.log2(math.e)
    wq = p["w_uq"][l].reshape(Q_LORA, MLA_HEADS, QK_NOPE + QK_ROPE) * scale
    zq = jnp.zeros((Q_LORA, MLA_HEADS, pad_r), F32)
    zn = jnp.zeros((Q_LORA, MLA_HEADS, QK_NOPE), F32)
    w["wq_a"] = jnp.concatenate([wq, zq], axis=-1).reshape(Q_LORA, -1).astype(BF16)
    w["wq_b"] = jnp.concatenate([zn, _swap_halves(wq[..., QK_NOPE:]), zq],
                                axis=-1).reshape(Q_LORA, -1).astype(BF16)
    wk = p["w_uk"][l].reshape(KV_LORA, MLA_HEADS, QK_NOPE)
    zk = jnp.zeros((KV_LORA, MLA_HEADS, HEAD_PAD - QK_NOPE), F32)
    w["wk"] = jnp.concatenate([wk, zk], axis=-1).reshape(KV_LORA, -1).astype(BF16)
    w["wv"] = p["w_uv"][l].T.astype(BF16)

    ar, ai = p["ssm_a_re"][l], p["ssm_a_im"][l]
    dt = jnp.exp(p["ssm_log_dt"][l])[:, None]
    mag = jnp.exp(ar * dt)
    lb_re, lb_im = mag * jnp.cos(ai * dt), mag * jnp.sin(ai * dt)
    den = ar * ar + ai * ai
    nr, ni = lb_re - 1.0, lb_im
    f_re, f_im = (nr * ar + ni * ai) / den, (ni * ar - nr * ai) / den
    br, bi = p["ssm_b_re"][l], p["ssm_b_im"][l]
    bb_re = f_re[..., None] * br - f_im[..., None] * bi
    bb_im = f_re[..., None] * bi + f_im[..., None] * br
    gpc = LANE // SSM_GROUP_DIM
    nch = SSM_GROUPS // gpc
    eye = jnp.eye(gpc, dtype=F32)

    def bmat(bb):
        return jnp.einsum("cgnp,gh->cgphn", bb.reshape(nch, gpc, SSM_STATE, SSM_GROUP_DIM),
                          eye).reshape(nch, LANE, gpc * SSM_STATE).astype(BF16)

    def cmat(cc):
        return jnp.einsum("cgpn,gh->cgnhp", cc.reshape(nch, gpc, SSM_GROUP_DIM, SSM_STATE),
                          eye).reshape(nch, gpc * SSM_STATE, LANE).astype(BF16)

    w["ssm_bre"], w["ssm_bim"] = bmat(bb_re), bmat(bb_im)
    w["ssm_cre"], w["ssm_cim"] = cmat(p["ssm_c_re"][l]), cmat(-p["ssm_c_im"][l])
    nrow = 8
    w["ssm_lre"] = jnp.broadcast_to(lb_re.reshape(1, -1), (nrow, SSM_GROUPS * SSM_STATE))
    w["ssm_lim"] = jnp.broadcast_to(lb_im.reshape(1, -1), (nrow, SSM_GROUPS * SSM_STATE))
    w["ssm_d"] = p["ssm_d"][l].reshape(1, D_SSM)
    w["ssm_wglu"] = p["ssm_w_glu"][l].astype(BF16)

    w["w_ups"] = p["w_up_ssm"][l].astype(BF16)
    w["w_upa"] = p["w_up_attn"][l].astype(BF16)
    w["w_o"] = p["w_o"][l].astype(BF16)
    w["lng0"], w["lnb0"] = p["ln_g"][l, 0].reshape(1, D), p["ln_b"][l, 0].reshape(1, D)
    w["lng1"], w["lnb1"] = p["ln_g"][l, 1].reshape(1, D), p["ln_b"][l, 1].reshape(1, D)
    rwg = p["router_w"].reshape(D, N_EXPERT_GROUPS, EXPERTS_PER_GROUP)
    rbg = p["router_bias"].reshape(N_EXPERT_GROUPS, EXPERTS_PER_GROUP)
    rwt = jnp.transpose(rwg, (2, 1, 0)).reshape(N_EXPERTS, D)
    rwt_hi = lax.bitcast_convert_type(
        lax.bitcast_convert_type(rwt, jnp.uint32) & jnp.uint32(0xFFFF0000), F32)
    w["rw"] = jnp.stack([rwt_hi.astype(BF16), (rwt - rwt_hi).astype(BF16)])
    w["rb"] = jnp.transpose(rbg, (1, 0)).reshape(N_EXPERTS, 1)
    w["ws1"] = p["shared_w1"][l].astype(BF16)
    w["ws3"] = p["shared_w3"][l].astype(BF16)
    w["ws2"] = p["shared_w2"][l].astype(BF16)
    return w


def kernel(x, c, positions, w_ada, b_ada, w_in, conv_w, ssm_a_re, ssm_a_im, ssm_b_re, ssm_b_im,
           ssm_c_re, ssm_c_im, ssm_d, ssm_log_dt, ssm_w_glu, q_norm, w_uq, kv_norm, w_uk, w_uv,
           w_up_conv, w_up_ssm, w_up_attn, w_o, ln_g, ln_b, router_w, router_bias,
           exp_w1, exp_w3, exp_w2, shared_w1, shared_w3, shared_w2):
    B, S, D = x.shape
    L = w_in.shape[0]
    T = B * S
    assert B == 8 and S % max(TM_PROJ, T_CHUNK, TQ, FINAL_ROWS) == 0 and T % max(MOE_ROWS, PERM_ROWS) == 0
    assert TM_PROJ == TQ
    p = dict(w_in=w_in, conv_w=conv_w, ssm_a_re=ssm_a_re, ssm_a_im=ssm_a_im, ssm_b_re=ssm_b_re,
             ssm_b_im=ssm_b_im, ssm_c_re=ssm_c_re, ssm_c_im=ssm_c_im, ssm_d=ssm_d,
             ssm_log_dt=ssm_log_dt, ssm_w_glu=ssm_w_glu, q_norm=q_norm, w_uq=w_uq, kv_norm=kv_norm,
             w_uk=w_uk, w_uv=w_uv, w_up_conv=w_up_conv, w_up_ssm=w_up_ssm, w_up_attn=w_up_attn,
             w_o=w_o, ln_g=ln_g, ln_b=ln_b, router_w=router_w, router_bias=router_bias,
             exp_w1=exp_w1, exp_w3=exp_w3, exp_w2=exp_w2, shared_w1=shared_w1,
             shared_w3=shared_w3, shared_w2=shared_w2)
    alpha = (2 * L) ** 0.25
    cos128, sin128 = _rope_tables(positions)
    ada = _ada(c, w_ada, b_ada).reshape(L, B, 6, D)
    x2d = x.reshape(T, D)
    for l in range(L):
        wts = _layer_weights(l, p)
        pc, sgs, sga, u, q, k, v = _inproj(x2d, ada[l], cos128, sin128, wts, B, S)
        ys = _ssm(u.reshape(B, S, D_SSM), wts, B, S).reshape(T, D_SSM)
        ya = _attention(q, k, v, B, S)
        base, hr, rt = _merge(x2d, ada[l], pc, sgs, sga, ys, ya, wts, B, S, alpha)
        gid = rt[EXPERTS_PER_GROUP].astype(jnp.int32)
        dest, block_g, nvalid = _route_plan(gid, T)
        hs = _permute(hr, dest, nvalid)
        ysort = _moe(hs, block_g, nvalid, exp_w1, exp_w3, exp_w2, l)
        x2d = _final(base, ysort, dest, ada[l], wts["lng1"], wts["lnb1"], B, S)
    return x2d.reshape(B, S, D)
```

```python
import functools
import math

import jax
import jax.numpy as jnp
from jax import lax
from jax.experimental import pallas as pl
from jax.experimental.pallas import tpu as pltpu

F32 = jnp.float32
BF16 = jnp.bfloat16

D_CONV = 512
D_SSM = 512
SSM_GROUP_DIM = 16
SSM_GROUPS = 32
SSM_STATE = 64
MLA_HEADS = 8
QK_NOPE = 64
QK_ROPE = 32
V_HEAD = 64
Q_LORA = 256
KV_LORA = 128
ROPE_THETA = 10000.0
N_EXPERTS = 32
N_EXPERT_GROUPS = 8
EXPERTS_PER_GROUP = 4
D_EXPERT = 256
LN_EPS = 1e-5
RMS_EPS = 1e-6

LANE = 128
HEAD_PAD = 128
VMEM_LIMIT = 56 * 1024 * 1024

TM_PROJ = 512
T_CHUNK = 64
TQ = 512
ATTN_HEADS = 8
DEN_ROWS = 16
MOE_ROWS = 512
PERM_ROWS = 2048
FINAL_ROWS = 1024
ROUTE_W = 128


def _cparams(n_axes):
    return pltpu.CompilerParams(dimension_semantics=("arbitrary",) * n_axes,
                                vmem_limit_bytes=VMEM_LIMIT)


def _ln_plain(x):
    mu = jnp.mean(x, axis=-1, keepdims=True)
    xc = x - mu
    var = jnp.mean(xc * xc, axis=-1, keepdims=True)
    return xc * lax.rsqrt(var + LN_EPS)


def _rms(x, g):
    return x * lax.rsqrt(jnp.mean(x * x, axis=-1, keepdims=True) + RMS_EPS) * g


def _bdot(a, b):
    return jnp.dot(a.astype(BF16), b, preferred_element_type=F32)


def _split_bf16(x):
    hi = pltpu.bitcast(pltpu.bitcast(x, jnp.uint32) & jnp.uint32(0xFFFF0000), F32)
    return hi.astype(BF16), (x - hi).astype(BF16)


def _ada_kernel(c_ref, w_ref, b_ref, o_ref):
    c = c_ref[...]
    cond = c * jax.nn.sigmoid(c)
    c_hi, c_lo = _split_bf16(cond)
    w_hi, w_lo = _split_bf16(w_ref[...])
    o_ref[...] = (jnp.dot(c_hi, w_hi, preferred_element_type=F32)
                  + jnp.dot(c_lo, w_hi, preferred_element_type=F32)
                  + jnp.dot(c_hi, w_lo, preferred_element_type=F32)
                  + jnp.dot(c_lo, w_lo, preferred_element_type=F32)) + b_ref[...]


def _ada(c, w_ada, b_ada):
    L, D, D6 = w_ada.shape
    B = c.shape[0]
    wn = D6 // 2
    nj = D6 // wn
    return pl.pallas_call(
        _ada_kernel,
        out_shape=jax.ShapeDtypeStruct((L, B, D6), F32),
        grid=(L, nj),
        in_specs=[pl.BlockSpec((B, D), lambda l, j: (0, 0)),
                  pl.BlockSpec((None, D, wn), lambda l, j: (l, 0, j)),
                  pl.BlockSpec((None, 1, wn), lambda l, j: (l, 0, j))],
        out_specs=pl.BlockSpec((None, B, wn), lambda l, j: (l, 0, j)),
        compiler_params=_cparams(2),
        name="ada",
    )(c, w_ada, b_ada.reshape(L, 1, D6))


def _inproj_kernel(x_ref, ada_ref, rope_ref,
                   w_cv, w_ssm, w_lat, w_g,
                   convw, qn, wq_a, wq_b, kvn, wk, wv, w_upc,
                   pc_ref, sgs_ref, sga_ref, u_ref, q_ref, k_ref, v_ref,
                   carry_ref):
    si = pl.program_id(1)
    D = x_ref.shape[1]
    tm = x_ref.shape[0]

    @pl.when(si == 0)
    def _():
        carry_ref[...] = jnp.zeros_like(carry_ref)

    ada = ada_ref[...]
    h = _ln_plain(x_ref[...]) * (1.0 + ada[1:2, :]) + ada[0:1, :]
    hb = h.astype(BF16)

    pcv = jnp.dot(hb, w_cv[...], preferred_element_type=F32)
    u = pcv[:, D_CONV:2 * D_CONV] * pcv[:, :D_CONV]
    gb = pcv[:, 2 * D_CONV:]
    tail = carry_ref[...]
    row8 = lax.broadcasted_iota(jnp.int32, (8, D_CONV), 0)
    r1 = pltpu.roll(u, 1, 0)
    r2 = pltpu.roll(u, 2, 0)
    t1 = pltpu.roll(tail, 1, 0)
    t2 = pltpu.roll(tail, 2, 0)
    u1 = jnp.concatenate([jnp.where(row8 < 1, t1, r1[:8]), r1[8:]], axis=0)
    u2 = jnp.concatenate([jnp.where(row8 < 2, t2, r2[:8]), r2[8:]], axis=0)
    cw = convw[...]
    yc = gb * (cw[0:1, :] * u2 + cw[1:2, :] * u1 + cw[2:3, :] * u)
    carry_ref[...] = u[tm - 8:, :]

    gates = jnp.dot(hb, w_g[...], preferred_element_type=F32)
    pc_ref[...] = (jax.nn.sigmoid(gates[:, :D]) * _bdot(yc, w_upc[...])).astype(BF16)
    sgs_ref[...] = jax.nn.sigmoid(gates[:, D:2 * D]).astype(BF16)
    sga_ref[...] = jax.nn.sigmoid(gates[:, 2 * D:]).astype(BF16)

    u_ref[...] = jnp.dot(hb, w_ssm[...], preferred_element_type=F32).astype(BF16)

    tab = rope_ref[...]
    lane = lax.broadcasted_iota(jnp.int32, tab.shape, 1)
    rope_lanes = (lane >= QK_NOPE) & (lane < QK_NOPE + QK_ROPE)
    cos = jnp.where(lane < QK_NOPE + QK_ROPE, tab, 0.0)
    sin = jnp.where(rope_lanes, pltpu.roll(tab, LANE - QK_ROPE, 1), 0.0)
    lat = jnp.dot(hb, w_lat[...], preferred_element_type=F32)
    o_kv, o_kr = Q_LORA, Q_LORA + KV_LORA
    cq = _rms(lat[:, :o_kv], qn[...]).astype(BF16)
    qa = jnp.dot(cq, wq_a[...], preferred_element_type=F32)
    qb = jnp.dot(cq, wq_b[...], preferred_element_type=F32)
    ckv = _rms(lat[:, o_kv:o_kr], kvn[...]).astype(BF16)
    kn = jnp.dot(ckv, wk[...], preferred_element_type=F32)
    kr = lat[:, o_kr:o_kr + HEAD_PAD] * cos + lat[:, o_kr + HEAD_PAD:] * sin
    for hd in range(MLA_HEADS):
        sl = slice(hd * HEAD_PAD, (hd + 1) * HEAD_PAD)
        q_ref[:, sl] = (qa[:, sl] * cos + qb[:, sl] * sin).astype(BF16)
        k_ref[:, sl] = (kn[:, sl] + kr).astype(BF16)
    v_ref[...] = lax.dot_general(wv[...], ckv, (((1,), (1,)), ((), ())),
                                 preferred_element_type=F32).astype(BF16)


def _inproj(x2d, ada_l, rope, wts, B, S):
    T, D = x2d.shape
    tm = TM_PROJ
    ns = S // tm
    row = lambda b, s: (b * ns + s, 0)
    full = lambda a: pl.BlockSpec(a.shape, lambda b, s: (0,) * a.ndim, pipeline_mode=pl.Buffered(1))
    names = ["w_cv", "w_ssm", "w_lat", "w_g", "convw", "qn",
             "wq_a", "wq_b", "kvn", "wk", "wv", "w_upc"]
    ws = [wts[n] for n in names]
    out_shape = (
        jax.ShapeDtypeStruct((T, D), BF16),
        jax.ShapeDtypeStruct((T, D), BF16),
        jax.ShapeDtypeStruct((T, D), BF16),
        jax.ShapeDtypeStruct((T, D_SSM), BF16),
        jax.ShapeDtypeStruct((T, MLA_HEADS * HEAD_PAD), BF16),
        jax.ShapeDtypeStruct((T, MLA_HEADS * HEAD_PAD), BF16),
        jax.ShapeDtypeStruct((B, ns, MLA_HEADS * V_HEAD, tm), BF16),
    )
    out_specs = (
        pl.BlockSpec((tm, D), row), pl.BlockSpec((tm, D), row), pl.BlockSpec((tm, D), row),
        pl.BlockSpec((tm, D_SSM), row),
        pl.BlockSpec((tm, MLA_HEADS * HEAD_PAD), row),
        pl.BlockSpec((tm, MLA_HEADS * HEAD_PAD), row),
        pl.BlockSpec((None, None, MLA_HEADS * V_HEAD, tm), lambda b, s: (b, s, 0, 0)),
    )
    return pl.pallas_call(
        _inproj_kernel,
        out_shape=out_shape,
        grid=(B, ns),
        in_specs=[pl.BlockSpec((tm, D), row),
                  pl.BlockSpec((None, 6, D), lambda b, s: (b, 0, 0)),
                  pl.BlockSpec((tm, LANE), row)]
                 + [full(a) for a in ws],
        out_specs=out_specs,
        scratch_shapes=[pltpu.VMEM((8, D_CONV), F32)],
        compiler_params=_cparams(2),
        name="inproj",
    )(x2d, ada_l, rope, *ws)


def _ssm_kernel(u_ref, bre_ref, bim_ref, lre_ref, lim_ref, cre_ref, cim_ref, d_ref, wglu_ref, perm_ref,
                o_ref, are, aim, au, bre_s, bim_s, bu, st_re, st_im, tmaj):
    i = pl.program_id(0)
    batch, nsteps = u_ref.shape[0], u_ref.shape[1]
    rows = batch * nsteps
    nchunk = bre_ref.shape[0]
    cw = bre_ref.shape[2]

    @pl.when(i == 0)
    def _():
        st_re[...] = jnp.zeros_like(st_re)
        st_im[...] = jnp.zeros_like(st_im)
        bre_s[...] = jnp.zeros_like(bre_s)
        bim_s[...] = jnp.zeros_like(bim_s)
        bu[...] = jnp.zeros_like(bu)

    def body(cur_re, cur_im, cur_u, prev_re, prev_im, prev_u):
        for b in range(batch):
            tmaj[:, b, :] = u_ref[b].astype(F32)
        uf = tmaj[...].reshape(rows, D_SSM)
        cur_u[...] = uf
        ub = uf.astype(BF16)
        for c in range(nchunk):
            uc = ub[:, c * LANE:(c + 1) * LANE]
            cur_re[:, c * cw:(c + 1) * cw] = jnp.dot(uc, bre_ref[c], preferred_element_type=F32)
            cur_im[:, c * cw:(c + 1) * cw] = jnp.dot(uc, bim_ref[c], preferred_element_type=F32)

        ys = []

        def out_chunk(c):
            ys.append(_bdot(prev_re[:, c * cw:(c + 1) * cw], cre_ref[c])
                      + _bdot(prev_im[:, c * cw:(c + 1) * cw], cim_ref[c]))

        def out_finish():
            y = jnp.concatenate(ys, axis=1) + d_ref[...] * prev_u[...]
            y = 0.5 * y * (1.0 + jnp.tanh(math.sqrt(2.0 / math.pi) * (y + 0.044715 * (y * y * y))))
            z = _bdot(y, wglu_ref[...])
            g = (z[:, :D_SSM] * jax.nn.sigmoid(z[:, D_SSM:])).astype(BF16)
            o_ref[...] = jnp.dot(perm_ref[...], g, preferred_element_type=F32).astype(BF16).reshape(
                batch, nsteps, D_SSM)

        pieces = [functools.partial(out_chunk, c) for c in range(nchunk)] + [out_finish]
        every = nsteps // len(pieces)
        pr, pi = st_re[...], st_im[...]
        for t in range(nsteps):
            r = t * batch
            lre = lre_ref[...]
            lim = lim_ref[...]
            nr = lre * pr - lim * pi + cur_re[r:r + batch, :]
            ni = lre * pi + lim * pr + cur_im[r:r + batch, :]
            cur_re[r:r + batch, :] = nr
            cur_im[r:r + batch, :] = ni
            pr, pi = nr, ni
            if t % every == every - 1 and pieces:
                pieces.pop(0)()
        while pieces:
            pieces.pop(0)()
        st_re[...] = pr
        st_im[...] = pi

    @pl.when(i % 2 == 0)
    def _():
        body(are, aim, au, bre_s, bim_s, bu)

    @pl.when(i % 2 == 1)
    def _():
        body(bre_s, bim_s, bu, are, aim, au)


def _ssm(u, wts, B, S):
    rows = T_CHUNK * B
    n = S // T_CHUNK
    nstate = SSM_GROUPS * SSM_STATE
    names = ["ssm_bre", "ssm_bim", "ssm_lre", "ssm_lim", "ssm_cre", "ssm_cim", "ssm_d", "ssm_wglu"]
    r = jnp.arange(rows, dtype=jnp.int32)
    src = (r % T_CHUNK) * B + r // T_CHUNK
    perm = (src[:, None] == jnp.arange(rows, dtype=jnp.int32)[None, :]).astype(BF16)
    ws = [wts[k] for k in names] + [perm]
    full = lambda a: pl.BlockSpec(a.shape, lambda i: (0,) * a.ndim, pipeline_mode=pl.Buffered(1))
    return pl.pallas_call(
        _ssm_kernel,
        out_shape=jax.ShapeDtypeStruct((B, S, D_SSM), BF16),
        grid=(n + 1,),
        in_specs=[pl.BlockSpec((B, T_CHUNK, D_SSM), lambda i: (0, jnp.minimum(i, n - 1), 0))]
                 + [full(a) for a in ws],
        out_specs=pl.BlockSpec((B, T_CHUNK, D_SSM), lambda i: (0, jnp.maximum(i - 1, 0), 0)),
        scratch_shapes=[pltpu.VMEM((rows, nstate), F32), pltpu.VMEM((rows, nstate), F32),
                        pltpu.VMEM((rows, D_SSM), F32),
                        pltpu.VMEM((rows, nstate), F32), pltpu.VMEM((rows, nstate), F32),
                        pltpu.VMEM((rows, D_SSM), F32),
                        pltpu.VMEM((B, nstate), F32), pltpu.VMEM((B, nstate), F32),
                        pltpu.VMEM((T_CHUNK, B, D_SSM), F32)],
        compiler_params=_cparams(1),
        name="ssm",
    )(u, *ws)


def _attn_kernel(q_ref, k_ref, vt_ref, o_ref, m_sc, acc_sc, sa_sc, sb_sc, p_sc):
    qi = pl.program_id(2)
    tq = q_ref.shape[0]
    nh = m_sc.shape[0]
    neg = -1e30
    m_sc[...] = jnp.full(m_sc.shape, neg, F32)
    acc_sc[...] = jnp.zeros(acc_sc.shape, F32)
    ones = jnp.ones((DEN_ROWS, tq), BF16)

    def scores(j, s_sc):
        off = pl.multiple_of(j * tq, tq)
        for hh in range(nh):
            q = q_ref[:, hh * HEAD_PAD:(hh + 1) * HEAD_PAD]
            kb = k_ref[pl.ds(off, tq), hh * HEAD_PAD:(hh + 1) * HEAD_PAD]
            s_sc[hh] = lax.dot_general(kb, q, (((1,), (1,)), ((), ())),
                                       preferred_element_type=F32)

    def update(j, s_sc, masked):
        scale = []
        for hh in range(nh):
            if masked:
                keyi = lax.broadcasted_iota(jnp.int32, (tq, tq), 0)
                qryi = lax.broadcasted_iota(jnp.int32, (tq, tq), 1)
                s_sc[hh] = jnp.where(keyi <= qryi, s_sc[hh], neg)
            m_prev = m_sc[hh]
            m_new = jnp.maximum(m_prev, jnp.max(s_sc[hh], axis=0, keepdims=True))
            m_sc[hh] = m_new
            p_sc[hh] = jnp.exp2(s_sc[hh] - m_new).astype(BF16)
            scale.append(jnp.exp2(m_prev - m_new))
        for hh in range(nh):
            vte = jnp.concatenate([vt_ref[j, hh * V_HEAD:(hh + 1) * V_HEAD, :], ones], axis=0)
            acc_sc[hh] = scale[hh] * acc_sc[hh] + jnp.dot(vte, p_sc[hh], preferred_element_type=F32)

    scores(0, sa_sc)

    def pair(pp, c):
        j = 2 * pp
        scores(j + 1, sb_sc)
        update(j, sa_sc, False)
        scores(j + 2, sa_sc)
        update(j + 1, sb_sc, False)
        return c

    npair = qi // 2
    lax.fori_loop(0, npair, pair, 0)
    j0 = 2 * npair

    @pl.when(qi % 2 == 0)
    def _():
        update(j0, sa_sc, True)

    @pl.when(qi % 2 == 1)
    def _():
        scores(j0 + 1, sb_sc)
        update(j0, sa_sc, False)
        update(j0 + 1, sb_sc, True)

    outs = []
    for pr in range(nh // 2):
        halves = []
        for hh in (2 * pr, 2 * pr + 1):
            acc = acc_sc[hh]
            halves.append(acc[:V_HEAD] * (1.0 / acc[V_HEAD:V_HEAD + 1]))
        outs.append(jnp.concatenate(halves, axis=0).T)
    o_ref[...] = jnp.concatenate(outs, axis=1).astype(BF16)


def _attention(q, k, vt, B, S):
    nq = S // TQ
    nh = ATTN_HEADS
    hg = MLA_HEADS // nh
    return pl.pallas_call(
        _attn_kernel,
        out_shape=jax.ShapeDtypeStruct((B * S, MLA_HEADS * V_HEAD), BF16),
        grid=(B, hg, nq),
        in_specs=[pl.BlockSpec((TQ, nh * HEAD_PAD), lambda b, h, i: (b * nq + i, h)),
                  pl.BlockSpec((S, nh * HEAD_PAD), lambda b, h, i: (b, h)),
                  pl.BlockSpec((None, nq, nh * V_HEAD, TQ), lambda b, h, i: (b, 0, h, 0))],
        out_specs=pl.BlockSpec((TQ, nh * V_HEAD), lambda b, h, i: (b * nq + i, h)),
        scratch_shapes=[pltpu.VMEM((nh, 1, TQ), F32),
                        pltpu.VMEM((nh, V_HEAD + DEN_ROWS, TQ), F32),
                        pltpu.VMEM((nh, TQ, TQ), F32), pltpu.VMEM((nh, TQ, TQ), F32),
                        pltpu.VMEM((nh, TQ, TQ), BF16)],
        compiler_params=_cparams(3),
        name="attention",
    )(q, k, vt)


def _merge_kernel(x_ref, ada_ref, pc_ref, sgs_ref, sga_ref, ys_ref, ya_ref,
                  w_ups, w_upa, w_o, lng, lnb, rw, rb, ws1, ws3, ws2,
                  base_ref, hr_ref, rt_ref, *, alpha):
    D = x_ref.shape[1]
    tm = x_ref.shape[0]
    ada = ada_ref[...]
    merged = (pc_ref[...]
              + sgs_ref[...] * jnp.dot(ys_ref[...], w_ups[...], preferred_element_type=F32).astype(BF16)
              + sga_ref[...] * jnp.dot(ya_ref[...], w_upa[...], preferred_element_type=F32).astype(BF16))
    y = jnp.dot(merged, w_o[...], preferred_element_type=F32)
    x1 = _ln_plain(alpha * x_ref[...] + ada[2:3, :] * y) * lng[...] + lnb[...]
    h2 = _ln_plain(x1) * (1.0 + ada[4:5, :]) + ada[3:4, :]
    hr_ref[:, :D] = h2

    G = N_EXPERT_GROUPS
    nt = (((1,), (1,)), ((), ()))
    h2_hi, h2_lo = _split_bf16(h2)
    rw_hi, rw_lo = rw[0], rw[1]
    logits = (lax.dot_general(rw_hi, h2_hi, nt, preferred_element_type=F32)
              + lax.dot_general(rw_lo, h2_hi, nt, preferred_element_type=F32)
              + lax.dot_general(rw_hi, h2_lo, nt, preferred_element_type=F32)
              + lax.dot_general(rw_lo, h2_lo, nt, preferred_element_type=F32))
    sc0 = jax.nn.sigmoid(logits)
    sl0 = sc0 + rb[...]
    sc = [sc0[j * G:(j + 1) * G] for j in range(EXPERTS_PER_GROUP)]
    sl = [sl0[j * G:(j + 1) * G] for j in range(EXPERTS_PER_GROUP)]
    hi1, lo1 = jnp.maximum(sl[0], sl[1]), jnp.minimum(sl[0], sl[1])
    hi2, lo2 = jnp.maximum(sl[2], sl[3]), jnp.minimum(sl[2], sl[3])
    gscore = jnp.maximum(hi1, hi2) + jnp.maximum(jnp.minimum(hi1, hi2), jnp.maximum(lo1, lo2))
    sub = lax.broadcasted_iota(jnp.int32, (G, tm), 0)
    gmax = jnp.max(gscore, axis=0, keepdims=True)
    gidx = jnp.min(jnp.where(gscore == gmax, sub, G), axis=0, keepdims=True)
    pick = sub == gidx
    vs = [jnp.sum(jnp.where(pick, a, 0.0), axis=0, keepdims=True) for a in sl]
    ss = [jnp.sum(jnp.where(pick, a, 0.0), axis=0, keepdims=True) for a in sc]
    chosen = []
    for j in range(EXPERTS_PER_GROUP):
        cnt = jnp.zeros((1, tm), jnp.int32)
        for k in range(EXPERTS_PER_GROUP):
            if k == j:
                continue
            beats = (vs[k] > vs[j]) | ((vs[k] == vs[j]) & (k < j))
            cnt = cnt + beats.astype(jnp.int32)
        chosen.append(cnt < 2)
    wsel = [jnp.where(chosen[j], ss[j], 0.0) for j in range(EXPERTS_PER_GROUP)]
    wsum = wsel[0] + wsel[1] + wsel[2] + wsel[3]
    rows = [wsel[j] / wsum for j in range(EXPERTS_PER_GROUP)] + [gidx.astype(F32)]
    rows += [jnp.zeros((1, tm), F32)] * (8 - len(rows))
    route_t = jnp.concatenate(rows, axis=0)
    rt_ref[...] = route_t
    eye = (lax.broadcasted_iota(jnp.int32, (8, ROUTE_W), 0)
           == lax.broadcasted_iota(jnp.int32, (8, ROUTE_W), 1)).astype(BF16)
    tn = (((0,), (0,)), ((), ()))
    rt_hi, rt_lo = _split_bf16(route_t)
    hr_ref[:, D:] = (lax.dot_general(rt_hi, eye, tn, preferred_element_type=F32)
                     + lax.dot_general(rt_lo, eye, tn, preferred_element_type=F32))

    h2b = h2.astype(BF16)
    a1 = jnp.dot(h2b, ws1[...], preferred_element_type=F32)
    a3 = jnp.dot(h2b, ws3[...], preferred_element_type=F32)
    ysh = _bdot(a1 * jax.nn.sigmoid(a1) * a3, ws2[...])
    base_ref[...] = alpha * x1 + ada[5:6, :] * ysh


def _merge(x2d, ada_l, pc, sgs, sga, ys, ya, wts, B, S, alpha):
    T, D = x2d.shape
    tm = TM_PROJ
    ns = S // tm
    row = lambda b, s: (b * ns + s, 0)
    names = ["w_ups", "w_upa", "w_o", "lng0", "lnb0", "rw", "rb", "ws1", "ws3", "ws2"]
    ws = [wts[n] for n in names]
    full = lambda a: pl.BlockSpec(a.shape, lambda b, s: (0,) * a.ndim, pipeline_mode=pl.Buffered(1))
    return pl.pallas_call(
        functools.partial(_merge_kernel, alpha=alpha),
        out_shape=(jax.ShapeDtypeStruct((T, D), F32),
                   jax.ShapeDtypeStruct((T, D + ROUTE_W), F32),
                   jax.ShapeDtypeStruct((8, T), F32)),
        grid=(B, ns),
        in_specs=[pl.BlockSpec((tm, D), row),
                  pl.BlockSpec((None, 6, D), lambda b, s: (b, 0, 0)),
                  pl.BlockSpec((tm, D), row), pl.BlockSpec((tm, D), row), pl.BlockSpec((tm, D), row),
                  pl.BlockSpec((tm, D_SSM), row),
                  pl.BlockSpec((tm, MLA_HEADS * V_HEAD), row)]
                 + [full(a) for a in ws],
        out_specs=(pl.BlockSpec((tm, D), row), pl.BlockSpec((tm, D + ROUTE_W), row),
                   pl.BlockSpec((8, tm), lambda b, s: (0, b * ns + s))),
        compiler_params=_cparams(2),
        name="merge",
    )(x2d, ada_l, pc, sgs, sga, ys, ya, *ws)


def _route_plan(gid, T):
    R = MOE_ROWS
    G = N_EXPERT_GROUPS
    onehot = (gid[:, None] == jnp.arange(G, dtype=jnp.int32)[None, :]).astype(jnp.int32)
    csum = jnp.cumsum(onehot, axis=0)
    counts = csum[-1]
    rank = jnp.sum(onehot * csum, axis=1) - 1
    padded = (counts + R - 1) // R * R
    pad_end = jnp.cumsum(padded)
    pad_start = pad_end - padded
    dest = jnp.sum(onehot * pad_start[None, :], axis=1) + rank
    nb = T // R + G
    starts = jnp.arange(nb, dtype=jnp.int32) * R
    block_g = jnp.minimum(jnp.sum((starts[:, None] >= pad_end[None, :]).astype(jnp.int32), axis=1), G - 1)
    gsel = (block_g[:, None] == jnp.arange(G, dtype=jnp.int32)[None, :]).astype(jnp.int32)
    group_end = jnp.sum(gsel * (pad_start + counts)[None, :], axis=1)
    nvalid = jnp.clip(group_end - starts, 0, R)
    return dest.astype(jnp.int32), block_g.astype(jnp.int32), nvalid.astype(jnp.int32)


def _permute_kernel(dest_ref, nvalid_ref, hr_ref, hs_hbm, zbuf, sem, zsem):
    i = pl.program_id(0)
    tp = hr_ref.shape[0]
    R = zbuf.shape[0]
    nb = nvalid_ref.shape[0]

    @pl.when(i == 0)
    def _():
        zbuf[...] = jnp.zeros(zbuf.shape, F32)

        def fill(b, n):
            partial = nvalid_ref[b] < R

            @pl.when(partial)
            def _():
                pltpu.make_async_copy(zbuf, hs_hbm.at[pl.ds(b * R, R)], zsem).start()

            return n + partial.astype(jnp.int32)

        nfill = lax.fori_loop(0, nb, fill, 0)

        def drain(k, c):
            pltpu.make_async_copy(zbuf, hs_hbm.at[pl.ds(0, R)], zsem).wait()
            return c

        lax.fori_loop(0, nfill, drain, 0)

    def body(g, c):
        r0 = pl.multiple_of(g * 8, 8)
        for k in range(8):
            d = dest_ref[i * tp + r0 + k]
            pltpu.make_async_copy(hr_ref.at[pl.ds(r0 + k, 1)], hs_hbm.at[pl.ds(d, 1)], sem).start()
        return c

    lax.fori_loop(0, tp // 8, body, 0)
    pltpu.make_async_copy(hr_ref, hs_hbm.at[pl.ds(0, tp)], sem).wait()


def _permute(hr, dest, nvalid):
    T, W = hr.shape
    R = MOE_ROWS
    tp = PERM_ROWS
    nb = nvalid.shape[0]
    return pl.pallas_call(
        _permute_kernel,
        out_shape=jax.ShapeDtypeStruct((nb * R, W), F32),
        grid_spec=pltpu.PrefetchScalarGridSpec(
            num_scalar_prefetch=2,
            grid=(T // tp,),
            in_specs=[pl.BlockSpec((tp, W), lambda i, d, n: (i, 0))],
            out_specs=pl.BlockSpec(memory_space=pl.ANY),
            scratch_shapes=[pltpu.VMEM((R, W), F32), pltpu.SemaphoreType.DMA, pltpu.SemaphoreType.DMA]),
        compiler_params=_cparams(1),
        name="permute",
    )(dest, nvalid, hr)


def _moe_kernel(bg_ref, nvalid_ref, hs_ref, w1_ref, w3_ref, w2_ref, y_ref, w1b, w3b, w2b):
    i = pl.program_id(0)
    R, D = y_ref.shape

    @pl.when((i == 0) | (bg_ref[i] != bg_ref[jnp.maximum(i - 1, 0)]))
    def _():
        w1b[...] = w1_ref[...].astype(BF16)
        w3b[...] = w3_ref[...].astype(BF16)
        w2b[...] = w2_ref[...].astype(BF16)

    @pl.when(nvalid_ref[i] > 0)
    def _():
        xb = hs_ref[:, :D].astype(BF16)
        acc = jnp.zeros((R, D), F32)
        for j in range(EXPERTS_PER_GROUP):
            a1 = jnp.dot(xb, w1b[j], preferred_element_type=F32)
            a3 = jnp.dot(xb, w3b[j], preferred_element_type=F32)
            hj = a1 * jax.nn.sigmoid(a1) * a3 * hs_ref[:, D + j:D + j + 1]
            acc = acc + _bdot(hj, w2b[j])
        y_ref[...] = acc

    @pl.when(nvalid_ref[i] == 0)
    def _():
        y_ref[...] = jnp.zeros((R, D), F32)


def _moe(hs, block_g, nvalid, w1, w3, w2, layer):
    D = w1.shape[2]
    W = hs.shape[1]
    R = MOE_ROWS
    nb = nvalid.shape[0]
    wspec_in = pl.BlockSpec((None, EXPERTS_PER_GROUP, D, D_EXPERT), lambda i, bg, nv: (layer, bg[i], 0, 0))
    wspec_out = pl.BlockSpec((None, EXPERTS_PER_GROUP, D_EXPERT, D), lambda i, bg, nv: (layer, bg[i], 0, 0))
    return pl.pallas_call(
        _moe_kernel,
        out_shape=jax.ShapeDtypeStruct((nb * R, D), F32),
        grid_spec=pltpu.PrefetchScalarGridSpec(
            num_scalar_prefetch=2,
            grid=(nb,),
            in_specs=[pl.BlockSpec((R, W), lambda i, bg, nv: (i, 0)), wspec_in, wspec_in, wspec_out],
            out_specs=pl.BlockSpec((R, D), lambda i, bg, nv: (i, 0)),
            scratch_shapes=[pltpu.VMEM((EXPERTS_PER_GROUP, D, D_EXPERT), BF16),
                            pltpu.VMEM((EXPERTS_PER_GROUP, D, D_EXPERT), BF16),
                            pltpu.VMEM((EXPERTS_PER_GROUP, D_EXPERT, D), BF16)]),
        compiler_params=_cparams(1),
        name="moe",
    )(block_g, nvalid, hs, w1, w3, w2)


def _final_kernel(dest_ref, base_ref, ys_hbm, ada_ref, lng, lnb, o_ref, ybuf, sem):
    i = pl.program_id(0)
    n = pl.num_programs(0)
    tf = base_ref.shape[0]
    slot = i % 2

    def start_gather(step, s):
        def body(g, c):
            r0 = pl.multiple_of(g * 8, 8)
            for k in range(8):
                d = dest_ref[step * tf + r0 + k]
                pltpu.make_async_copy(ys_hbm.at[pl.ds(d, 1)], ybuf.at[s, pl.ds(r0 + k, 1)],
                                      sem.at[s]).start()
            return c

        lax.fori_loop(0, tf // 8, body, 0)

    @pl.when(i == 0)
    def _():
        start_gather(0, 0)

    @pl.when(i + 1 < n)
    def _():
        start_gather(i + 1, 1 - slot)

    pltpu.make_async_copy(ys_hbm.at[pl.ds(0, tf)], ybuf.at[slot], sem.at[slot]).wait()
    ada = ada_ref[...]
    o_ref[...] = _ln_plain(base_ref[...] + ada[5:6, :] * ybuf[slot]) * lng[...] + lnb[...]


def _final(base, ys, dest, ada_l, lng, lnb, B, S):
    T, D = base.shape
    tf = FINAL_ROWS
    per_batch = S // tf
    return pl.pallas_call(
        _final_kernel,
        out_shape=jax.ShapeDtypeStruct((T, D), F32),
        grid_spec=pltpu.PrefetchScalarGridSpec(
            num_scalar_prefetch=1,
            grid=(T // tf,),
            in_specs=[pl.BlockSpec((tf, D), lambda i, d: (i, 0)),
                      pl.BlockSpec(memory_space=pl.ANY),
                      pl.BlockSpec((None, 6, D), lambda i, d: (i // per_batch, 0, 0)),
                      pl.BlockSpec((1, D), lambda i, d: (0, 0)), pl.BlockSpec((1, D), lambda i, d: (0, 0))],
            out_specs=pl.BlockSpec((tf, D), lambda i, d: (i, 0)),
            scratch_shapes=[pltpu.VMEM((2, tf, D), F32), pltpu.SemaphoreType.DMA((2,))]),
        compiler_params=_cparams(1),
        name="final_ln",
    )(dest, base, ys, ada_l, lng, lnb)


def _rope_tables(positions):
    inv_freq = ROPE_THETA ** (-jnp.arange(0, QK_ROPE, 2, dtype=F32) / QK_ROPE)
    ang = positions.astype(F32)[..., None] * inv_freq
    dense = lax.optimization_barrier(ang.reshape(-1, LANE))
    cos, sin = lax.optimization_barrier((jnp.cos(dense), jnp.sin(dense)))
    cos, sin = cos.reshape(ang.shape), sin.reshape(ang.shape)
    ones = jnp.ones(cos.shape[:-1] + (QK_NOPE,), F32)
    table = jnp.concatenate([ones, cos, cos, -sin, sin], axis=-1)
    T = positions.shape[0] * positions.shape[1]
    return table.reshape(T, HEAD_PAD)


def _swap_halves(w):
    half = w.shape[-1] // 2
    return jnp.concatenate([w[..., half:], w[..., :half]], axis=-1)


def _layer_weights(l, p):
    D = p["w_in"].shape[1]
    w_in = p["w_in"][l]
    o = 0
    seg = {}
    for name, width in (("cv", 3 * D_CONV), ("ssm", D_SSM), ("cq", Q_LORA), ("ckv", KV_LORA),
                        ("kr", QK_ROPE), ("g", 3 * D)):
        seg[name] = w_in[:, o:o + width]
        o += width
    pad_r = HEAD_PAD - QK_NOPE - QK_ROPE
    zl = jnp.zeros((D, QK_NOPE), F32)
    zr = jnp.zeros((D, pad_r), F32)
    w = {
        "w_cv": seg["cv"].astype(BF16), "w_ssm": seg["ssm"].astype(BF16),
        "w_lat": jnp.concatenate([seg["cq"], seg["ckv"], zl, seg["kr"], zr,
                                  zl, _swap_halves(seg["kr"]), zr], axis=1).astype(BF16),
        "w_g": seg["g"].astype(BF16),
        "convw": p["conv_w"][l],
        "qn": p["q_norm"][l].reshape(1, Q_LORA), "kvn": p["kv_norm"][l].reshape(1, KV_LORA),
        "w_upc": p["w_up_conv"][l].astype(BF16),
    }
    scale = (QK_NOPE + QK_ROPE) ** -0.5 * math.log2(math.e)
    wq = p["w_uq"][l].reshape(Q_LORA, MLA_HEADS, QK_NOPE + QK_ROPE) * scale
    zq = jnp.zeros((Q_LORA, MLA_HEADS, pad_r), F32)
    zn = jnp.zeros((Q_LORA, MLA_HEADS, QK_NOPE), F32)
    w["wq_a"] = jnp.concatenate([wq, zq], axis=-1).reshape(Q_LORA, -1).astype(BF16)
    w["wq_b"] = jnp.concatenate([zn, _swap_halves(wq[..., QK_NOPE:]), zq],
                                axis=-1).reshape(Q_LORA, -1).astype(BF16)
    wk = p["w_uk"][l].reshape(KV_LORA, MLA_HEADS, QK_NOPE)
    zk = jnp.zeros((KV_LORA, MLA_HEADS, HEAD_PAD - QK_NOPE), F32)
    w["wk"] = jnp.concatenate([wk, zk], axis=-1).reshape(KV_LORA, -1).astype(BF16)
    w["wv"] = p["w_uv"][l].T.astype(BF16)

    ar, ai = p["ssm_a_re"][l], p["ssm_a_im"][l]
    dt = jnp.exp(p["ssm_log_dt"][l])[:, None]
    mag = jnp.exp(ar * dt)
    lb_re, lb_im = mag * jnp.cos(ai * dt), mag * jnp.sin(ai * dt)
    den = ar * ar + ai * ai
    nr, ni = lb_re - 1.0, lb_im
    f_re, f_im = (nr * ar + ni * ai) / den, (ni * ar - nr * ai) / den
    br, bi = p["ssm_b_re"][l], p["ssm_b_im"][l]
    bb_re = f_re[..., None] * br - f_im[..., None] * bi
    bb_im = f_re[..., None] * bi + f_im[..., None] * br
    gpc = LANE // SSM_GROUP_DIM
    nch = SSM_GROUPS // gpc
    eye = jnp.eye(gpc, dtype=F32)

    def bmat(bb):
        return jnp.einsum("cgnp,gh->cgphn", bb.reshape(nch, gpc, SSM_STATE, SSM_GROUP_DIM),
                          eye).reshape(nch, LANE, gpc * SSM_STATE).astype(BF16)

    def cmat(cc):
        return jnp.einsum("cgpn,gh->cgnhp", cc.reshape(nch, gpc, SSM_GROUP_DIM, SSM_STATE),
                          eye).reshape(nch, gpc * SSM_STATE, LANE).astype(BF16)

    w["ssm_bre"], w["ssm_bim"] = bmat(bb_re), bmat(bb_im)
    w["ssm_cre"], w["ssm_cim"] = cmat(p["ssm_c_re"][l]), cmat(-p["ssm_c_im"][l])
    nrow = 8
    w["ssm_lre"] = jnp.broadcast_to(lb_re.reshape(1, -1), (nrow, SSM_GROUPS * SSM_STATE))
    w["ssm_lim"] = jnp.broadcast_to(lb_im.reshape(1, -1), (nrow, SSM_GROUPS * SSM_STATE))
    w["ssm_d"] = p["ssm_d"][l].reshape(1, D_SSM)
    w["ssm_wglu"] = p["ssm_w_glu"][l].astype(BF16)

    w["w_ups"] = p["w_up_ssm"][l].astype(BF16)
    w["w_upa"] = p["w_up_attn"][l].astype(BF16)
    w["w_o"] = p["w_o"][l].astype(BF16)
    w["lng0"], w["lnb0"] = p["ln_g"][l, 0].reshape(1, D), p["ln_b"][l, 0].reshape(1, D)
    w["lng1"], w["lnb1"] = p["ln_g"][l, 1].reshape(1, D), p["ln_b"][l, 1].reshape(1, D)
    rwg = p["router_w"].reshape(D, N_EXPERT_GROUPS, EXPERTS_PER_GROUP)
    rbg = p["router_bias"].reshape(N_EXPERT_GROUPS, EXPERTS_PER_GROUP)
    rwt = jnp.transpose(rwg, (2, 1, 0)).reshape(N_EXPERTS, D)
    rwt_hi = lax.bitcast_convert_type(
        lax.bitcast_convert_type(rwt, jnp.uint32) & jnp.uint32(0xFFFF0000), F32)
    w["rw"] = jnp.stack([rwt_hi.astype(BF16), (rwt - rwt_hi).astype(BF16)])
    w["rb"] = jnp.transpose(rbg, (1, 0)).reshape(N_EXPERTS, 1)
    w["ws1"] = p["shared_w1"][l].astype(BF16)
    w["ws3"] = p["shared_w3"][l].astype(BF16)
    w["ws2"] = p["shared_w2"][l].astype(BF16)
    return w


def kernel(x, c, positions, w_ada, b_ada, w_in, conv_w, ssm_a_re, ssm_a_im, ssm_b_re, ssm_b_im,
           ssm_c_re, ssm_c_im, ssm_d, ssm_log_dt, ssm_w_glu, q_norm, w_uq, kv_norm, w_uk, w_uv,
           w_up_conv, w_up_ssm, w_up_attn, w_o, ln_g, ln_b, router_w, router_bias,
           exp_w1, exp_w3, exp_w2, shared_w1, shared_w3, shared_w2):
    B, S, D = x.shape
    L = w_in.shape[0]
    T = B * S
    assert B == 8 and S % max(TM_PROJ, T_CHUNK, TQ, FINAL_ROWS) == 0 and T % max(MOE_ROWS, PERM_ROWS) == 0
    assert TM_PROJ == TQ
    p = dict(w_in=w_in, conv_w=conv_w, ssm_a_re=ssm_a_re, ssm_a_im=ssm_a_im, ssm_b_re=ssm_b_re,
             ssm_b_im=ssm_b_im, ssm_c_re=ssm_c_re, ssm_c_im=ssm_c_im, ssm_d=ssm_d,
             ssm_log_dt=ssm_log_dt, ssm_w_glu=ssm_w_glu, q_norm=q_norm, w_uq=w_uq, kv_norm=kv_norm,
             w_uk=w_uk, w_uv=w_uv, w_up_conv=w_up_conv, w_up_ssm=w_up_ssm, w_up_attn=w_up_attn,
             w_o=w_o, ln_g=ln_g, ln_b=ln_b, router_w=router_w, router_bias=router_bias,
             exp_w1=exp_w1, exp_w3=exp_w3, exp_w2=exp_w2, shared_w1=shared_w1,
             shared_w3=shared_w3, shared_w2=shared_w2)
    alpha = (2 * L) ** 0.25
    rope = _rope_tables(positions)
    ada = _ada(c, w_ada, b_ada).reshape(L, B, 6, D)
    x2d = x.reshape(T, D)
    for l in range(L):
        wts = _layer_weights(l, p)
        pc, sgs, sga, u, q, k, v = _inproj(x2d, ada[l], rope, wts, B, S)
        ys = _ssm(u.reshape(B, S, D_SSM), wts, B, S).reshape(T, D_SSM)
        ya = _attention(q, k, v, B, S)
        base, hr, rt = _merge(x2d, ada[l], pc, sgs, sga, ys, ya, wts, B, S, alpha)
        gid = rt[EXPERTS_PER_GROUP].astype(jnp.int32)
        dest, block_g, nvalid = _route_plan(gid, T)
        hs = _permute(hr, dest, nvalid)
        ysort = _moe(hs, block_g, nvalid, exp_w1, exp_w3, exp_w2, l)
        x2d = _final(base, ysort, dest, ada[l], wts["lng1"], wts["lnb1"], B, S)
    return x2d.reshape(B, S, D)
```
